```python
import jax, jax.numpy as jnp
from jax import lax
import numpy as np

D_MODEL = 1024
BATCH = 8
SEQ = 4096
DEPTH = 2

CTX_LEN = 256
GRID_W = 64
EPS = 1e-6
N_MOD = 9
D_FF = 2816

FNET_GROUPS = 4
FNET_GROUP_DIM = 128
FNET_DIM = FNET_GROUPS * FNET_GROUP_DIM
NA_HEADS = 8
NA_HEAD_DIM = 64
NA_DIM = NA_HEADS * NA_HEAD_DIM
NA_KH = 8
NA_KW = 16
AB_IN_DIM = FNET_DIM + 3 * NA_DIM
AB_OUT_DIM = FNET_DIM + NA_DIM
MLA_HEADS = 8
MLA_NOPE = 128
MLA_ROPE = 64
MLA_V = 128
MLA_Q_RANK = 384
MLA_KV_RANK = 128
MLA_IN_DIM = MLA_Q_RANK + MLA_KV_RANK + MLA_ROPE
ROPE_BASE = 10000.0
Q_BLOCK = 128

N_EVEN = (DEPTH + 1) // 2
N_ODD = DEPTH // 2

kernel_name = "hybrid_fnet_natten_mla_macaron_dit"


def rmsnorm(x, g):
    xf = x.astype(jnp.float32)
    y = xf * lax.rsqrt(jnp.mean(xf * xf, axis=-1, keepdims=True) + EPS)
    return (y * g.astype(jnp.float32)).astype(x.dtype)


def swiglu(h, w_gate, w_up, w_down):
    return (jax.nn.silu(h @ w_gate) * (h @ w_up)) @ w_down


def half_ffn(x, shift, scale, gate, g, w_gate, w_up, w_down):
    h = rmsnorm(x, g) * (1 + scale) + shift
    return x + 0.5 * gate * swiglu(h, w_gate, w_up, w_down)


def axial_rope_tables(n_tok, rot_dim):
    pos = jnp.arange(n_tok)
    row = (pos // GRID_W).astype(jnp.float32)
    col = (pos % GRID_W).astype(jnp.float32)
    half = rot_dim // 2
    inv = ROPE_BASE ** (-jnp.arange(0, half, 2, dtype=jnp.float32) / half)
    ang_r = row[:, None] * inv[None]
    ang_c = col[:, None] * inv[None]
    ang = jnp.concatenate([ang_r, ang_r, ang_c, ang_c], axis=-1)
    return jnp.cos(ang), jnp.sin(ang)


def apply_axial_rope(x, cos, sin):
    xf = x.astype(jnp.float32)
    a, b, cc, d = jnp.split(xf, 4, axis=-1)
    rot = jnp.concatenate([-b, a, -d, cc], axis=-1)
    return (xf * cos + rot * sin).astype(x.dtype)


def fourier_mix(u):
    b, n, _ = u.shape
    ug = u.astype(jnp.float32).reshape(b, n, FNET_GROUPS, FNET_GROUP_DIM)
    f = jnp.fft.fft2(ug, axes=(1, 3), norm="ortho").real
    return f.reshape(b, n, FNET_DIM).astype(u.dtype)


def dense_attention(q, k, v):
    s = jnp.einsum('bqhd,bkhd->bhqk', q, k, preferred_element_type=jnp.float32) * (q.shape[-1] ** -0.5)
    p = jax.nn.softmax(s, axis=-1).astype(v.dtype)
    return jnp.einsum('bhqk,bkhd->bqhd', p, v)


def neighbourhood_attention(q, k, v, k_ctx, v_ctx, rpb):
    b, s, h, dh = q.shape
    rows = s // GRID_W
    kh = min(NA_KH, rows)
    kw = NA_KW
    scale = dh ** -0.5
    qg = q.reshape(b, rows, GRID_W, h, dh)
    kg = k.reshape(b, rows, GRID_W, h, dh)
    vg = v.reshape(b, rows, GRID_W, h, dh)
    cols = jnp.arange(GRID_W)
    col_start = jnp.clip(cols - kw // 2, 0, GRID_W - kw)
    col_idx = col_start[:, None] + jnp.arange(kw)[None, :]
    col_off = col_idx - cols[:, None] + (NA_KW - 1)

    def one_row(r):
        r0 = jnp.clip(r - kh // 2, 0, rows - kh)
        row_off = r0 + jnp.arange(kh) - r + (NA_KH - 1)
        k_band = lax.dynamic_slice_in_dim(kg, r0, kh, axis=1)
        v_band = lax.dynamic_slice_in_dim(vg, r0, kh, axis=1)
        k_win = k_band[:, :, col_idx]
        v_win = v_band[:, :, col_idx]
        qr = lax.dynamic_index_in_dim(qg, r, axis=1, keepdims=False)
        bias = rpb[:, row_off[:, None, None], col_off[None, :, :]]
        bias = jnp.transpose(bias, (0, 2, 1, 3)).reshape(h, GRID_W, kh * kw).astype(jnp.float32)
        s_win = jnp.einsum('bqhd,biqjhd->bhqij', qr, k_win, preferred_element_type=jnp.float32)
        s_win = s_win.reshape(b, h, GRID_W, kh * kw) * scale + bias[None]
        s_ctx = jnp.einsum('bqhd,blhd->bhql', qr, k_ctx, preferred_element_type=jnp.float32) * scale
        p = jax.nn.softmax(jnp.concatenate([s_win, s_ctx], axis=-1), axis=-1).astype(v.dtype)
        p_win = p[..., :kh * kw].reshape(b, h, GRID_W, kh, kw)
        p_ctx = p[..., kh * kw:]
        return (jnp.einsum('bhqij,biqjhd->bqhd', p_win, v_win)
                + jnp.einsum('bhql,blhd->bqhd', p_ctx, v_ctx))

    o = lax.map(one_row, jnp.arange(rows))
    return jnp.transpose(o, (1, 0, 2, 3, 4)).reshape(b, s, h * dh)


def fourier_na_mixer(h_lat, h_ctx, w_in, rpb, w_out, need_ctx_out):
    def split(z):
        u = z[..., :FNET_DIM]
        qkv = z[..., FNET_DIM:].reshape(z.shape[0], z.shape[1], 3, NA_HEADS, NA_HEAD_DIM)
        return u, qkv[:, :, 0], qkv[:, :, 1], qkv[:, :, 2]

    u_l, q_l, k_l, v_l = split(h_lat @ w_in)
    u_c, q_c, k_c, v_c = split(h_ctx @ w_in)
    a_l = fourier_mix(u_l)
    b_l = neighbourhood_attention(q_l, k_l, v_l, k_c, v_c, rpb)
    y_lat = jnp.concatenate([a_l, b_l], axis=-1) @ w_out
    if not need_ctx_out:
        return y_lat, None
    a_c = fourier_mix(u_c)
    b_c = dense_attention(q_c, k_c, v_c).reshape(h_ctx.shape[0], h_ctx.shape[1], NA_DIM)
    y_ctx = jnp.concatenate([a_c, b_c], axis=-1) @ w_out
    return y_lat, y_ctx


def mla_block_attention(q_nope, q_rope, k_nope, k_rope, v):
    b, n, h, _ = q_nope.shape
    nb = n // Q_BLOCK
    scale = (MLA_NOPE + MLA_ROPE) ** -0.5
    qn = q_nope.reshape(b, nb, Q_BLOCK, h, MLA_NOPE).swapaxes(0, 1)
    qr = q_rope.reshape(b, nb, Q_BLOCK, h, MLA_ROPE).swapaxes(0, 1)

    def one_block(args):
        qn_b, qr_b = args
        s = (jnp.einsum('bqhd,bkhd->bhqk', qn_b, k_nope, preferred_element_type=jnp.float32)
             + jnp.einsum('bqhr,bkr->bhqk', qr_b, k_rope, preferred_element_type=jnp.float32)) * scale
        p = jax.nn.softmax(s, axis=-1).astype(v.dtype)
        return jnp.einsum('bhqk,bkhd->bqhd', p, v)

    o = lax.map(one_block, (qn, qr))
    return o.swapaxes(0, 1).reshape(b, n, h * MLA_V)


def mla_mixer(h_lat, h_ctx, w_in, g_q, g_kv, w_uq, w_uk, w_uv, w_o, cos, sin, need_ctx_out):
    def project(hh):
        bb, nn = hh.shape[0], hh.shape[1]
        z = hh @ w_in
        c_q = rmsnorm(z[..., :MLA_Q_RANK], g_q)
        c_kv = rmsnorm(z[..., MLA_Q_RANK:MLA_Q_RANK + MLA_KV_RANK], g_kv)
        k_rope = z[..., MLA_Q_RANK + MLA_KV_RANK:]
        q = (c_q @ w_uq).reshape(bb, nn, MLA_HEADS, MLA_NOPE + MLA_ROPE)
        k_nope = (c_kv @ w_uk).reshape(bb, nn, MLA_HEADS, MLA_NOPE)
        v = (c_kv @ w_uv).reshape(bb, nn, MLA_HEADS, MLA_V)
        return q[..., :MLA_NOPE], q[..., MLA_NOPE:], k_nope, k_rope, v

    qn_l, qr_l, kn_l, kr_l, v_l = project(h_lat)
    qn_c, qr_c, kn_c, kr_c, v_c = project(h_ctx)
    qr_l = apply_axial_rope(qr_l, cos[:, None, :], sin[:, None, :])
    kr_l = apply_axial_rope(kr_l, cos, sin)
    kn_all = jnp.concatenate([kn_l, kn_c], axis=1)
    kr_all = jnp.concatenate([kr_l, kr_c], axis=1)
    v_all = jnp.concatenate([v_l, v_c], axis=1)
    y_lat = mla_block_attention(qn_l, qr_l, kn_all, kr_all, v_all) @ w_o
    if not need_ctx_out:
        return y_lat, None
    y_ctx = mla_block_attention(qn_c, qr_c, kn_c, kr_c, v_c) @ w_o
    return y_lat, y_ctx


def setup_inputs(seed: int = 0) -> dict:
    key = jax.random.key(seed)
    ks = jax.random.split(key, 24)
    f32 = jnp.float32

    def nrm(k, shape, std):
        return jax.random.normal(k, shape, f32) * std

    return {
        "x": nrm(ks[0], (BATCH, SEQ, D_MODEL), 1.0),
        "c": nrm(ks[1], (BATCH, D_MODEL), 1.0),
        "ctx": nrm(ks[2], (BATCH, CTX_LEN, D_MODEL), 1.0),
        "c_ctx": nrm(ks[3], (D_MODEL,), 1.0),
        "ada_w": nrm(ks[4], (DEPTH, D_MODEL, N_MOD * D_MODEL), 0.5 * D_MODEL ** -0.5),
        "ada_b": nrm(ks[5], (DEPTH, N_MOD * D_MODEL), 0.02),
        "norm_g": 1.0 + nrm(ks[6], (DEPTH, 3, D_MODEL), 0.02),
        "ffn_w_gate": nrm(ks[7], (DEPTH, 2, D_MODEL, D_FF), D_MODEL ** -0.5),
        "ffn_w_up": nrm(ks[8], (DEPTH, 2, D_MODEL, D_FF), D_MODEL ** -0.5),
        "ffn_w_down": nrm(ks[9], (DEPTH, 2, D_FF, D_MODEL), D_FF ** -0.5),
        "ab_w_in": nrm(ks[10], (N_EVEN, D_MODEL, AB_IN_DIM), D_MODEL ** -0.5),
        "ab_rpb": nrm(ks[11], (N_EVEN, NA_HEADS, 2 * NA_KH - 1, 2 * NA_KW - 1), 0.1),
        "ab_w_out": nrm(ks[12], (N_EVEN, AB_OUT_DIM, D_MODEL), AB_OUT_DIM ** -0.5),
        "mla_w_in": nrm(ks[13], (N_ODD, D_MODEL, MLA_IN_DIM), D_MODEL ** -0.5),
        "mla_g_q": 1.0 + nrm(ks[14], (N_ODD, MLA_Q_RANK), 0.02),
        "mla_g_kv": 1.0 + nrm(ks[15], (N_ODD, MLA_KV_RANK), 0.02),
        "mla_w_uq": nrm(ks[16], (N_ODD, MLA_Q_RANK, MLA_HEADS * (MLA_NOPE + MLA_ROPE)), MLA_Q_RANK ** -0.5),
        "mla_w_uk": nrm(ks[17], (N_ODD, MLA_KV_RANK, MLA_HEADS * MLA_NOPE), MLA_KV_RANK ** -0.5),
        "mla_w_uv": nrm(ks[18], (N_ODD, MLA_KV_RANK, MLA_HEADS * MLA_V), MLA_KV_RANK ** -0.5),
        "mla_w_o": nrm(ks[19], (N_ODD, MLA_HEADS * MLA_V, D_MODEL), (MLA_HEADS * MLA_V) ** -0.5),
        "final_g": 1.0 + nrm(ks[20], (D_MODEL,), 0.02),
    }


def reference(x, c, ctx, c_ctx, ada_w, ada_b, norm_g, ffn_w_gate, ffn_w_up, ffn_w_down,
              ab_w_in, ab_rpb, ab_w_out, mla_w_in, mla_g_q, mla_g_kv, mla_w_uq, mla_w_uk,
              mla_w_uv, mla_w_o, final_g):
    n_lat = x.shape[1]
    cos, sin = axial_rope_tables(n_lat, MLA_ROPE)
    sc = jax.nn.silu(c)
    scc = jax.nn.silu(c_ctx)
    x_l, x_c = x, ctx
    for layer in range(DEPTH):
        last = layer == DEPTH - 1
        mod_l = (sc @ ada_w[layer] + ada_b[layer]).reshape(-1, N_MOD, D_MODEL)[:, :, None, :]
        mod_c = (scc @ ada_w[layer] + ada_b[layer]).reshape(N_MOD, D_MODEL)

        x_l = half_ffn(x_l, mod_l[:, 0], mod_l[:, 1], mod_l[:, 2], norm_g[layer, 0],
                       ffn_w_gate[layer, 0], ffn_w_up[layer, 0], ffn_w_down[layer, 0])
        x_c = half_ffn(x_c, mod_c[0], mod_c[1], mod_c[2], norm_g[layer, 0],
                       ffn_w_gate[layer, 0], ffn_w_up[layer, 0], ffn_w_down[layer, 0])

        h_l = rmsnorm(x_l, norm_g[layer, 1]) * (1 + mod_l[:, 4]) + mod_l[:, 3]
        h_c = rmsnorm(x_c, norm_g[layer, 1]) * (1 + mod_c[4]) + mod_c[3]
        i = layer // 2
        if layer % 2 == 0:
            y_l, y_c = fourier_na_mixer(h_l, h_c, ab_w_in[i], ab_rpb[i], ab_w_out[i], not last)
        else:
            y_l, y_c = mla_mixer(h_l, h_c, mla_w_in[i], mla_g_q[i], mla_g_kv[i], mla_w_uq[i],
                                 mla_w_uk[i], mla_w_uv[i], mla_w_o[i], cos, sin, not last)
        x_l = x_l + mod_l[:, 5] * y_l

        x_l = half_ffn(x_l, mod_l[:, 6], mod_l[:, 7], mod_l[:, 8], norm_g[layer, 2],
                       ffn_w_gate[layer, 1], ffn_w_up[layer, 1], ffn_w_down[layer, 1])
        if not last:
            x_c = x_c + mod_c[5] * y_c
            x_c = half_ffn(x_c, mod_c[6], mod_c[7], mod_c[8], norm_g[layer, 2],
                           ffn_w_gate[layer, 1], ffn_w_up[layer, 1], ffn_w_down[layer, 1])
    return rmsnorm(x_l, final_g)
```

```python
import functools
import math

import numpy as np
import jax
import jax.numpy as jnp
from jax import lax
from jax.experimental import pallas as pl
from jax.experimental.pallas import tpu as pltpu

F32 = jnp.float32
BF16 = jnp.bfloat16

GRID_W = 64
EPS = 1e-6
N_MOD = 9
FNET_GROUPS = 4
FNET_GROUP_DIM = 128
FNET_DIM = FNET_GROUPS * FNET_GROUP_DIM
NA_HEADS = 8
NA_HEAD_DIM = 64
NA_DIM = NA_HEADS * NA_HEAD_DIM
NA_KH = 8
NA_KW = 16
MLA_HEADS = 8
MLA_NOPE = 128
MLA_ROPE = 64
MLA_V = 128
MLA_Q_RANK = 384
MLA_KV_RANK = 128
ROPE_BASE = 10000.0

LANES = 128
VMEM_LIMIT_BYTES = 56 * 1024 * 1024
NEG_BIG = -1e30

NA_ROWS_PER_BLOCK = 8
NA_WIN_ROWS = 16
NA_KBLK = 256


def _params(n_axes):
    return pltpu.CompilerParams(
        dimension_semantics=("arbitrary",) * n_axes,
        vmem_limit_bytes=VMEM_LIMIT_BYTES,
    )


def _resident(shape):
    nd = len(shape)
    return pl.BlockSpec(shape, lambda *_: (0,) * nd, pipeline_mode=pl.Buffered(1))


def _dot(a, b):
    return jnp.dot(a, b, preferred_element_type=F32)


def _dot_nt(a, b):
    return lax.dot_general(a, b, (((1,), (1,)), ((), ())), preferred_element_type=F32)


def _rms(x):
    return x * lax.rsqrt(jnp.mean(x * x, axis=-1, keepdims=True) + EPS)


def _norm_mod(x, g, mod_ref, k0):
    shift = mod_ref[k0:k0 + 1, :]
    scale = mod_ref[k0 + 1:k0 + 2, :]
    return (_rms(x) * g) * (1.0 + scale) + shift


def _mod_kernel(c_ref, w_ref, b_ref, o_ref):
    c = c_ref[...]
    sc = c * jax.nn.sigmoid(c)
    o_ref[...] = jnp.dot(sc, w_ref[...], preferred_element_type=F32,
                         precision=lax.Precision.HIGHEST) + b_ref[...]


def _modulation(cvec, ada_w, ada_b):
    depth, d, n = ada_w.shape
    r = cvec.shape[0]
    tn = 1536
    assert n % tn == 0
    out = pl.pallas_call(
        _mod_kernel,
        grid=(depth, n // tn),
        in_specs=[
            pl.BlockSpec((r, d), lambda l, j: (0, 0)),
            pl.BlockSpec((None, d, tn), lambda l, j: (l, 0, j)),
            pl.BlockSpec((None, 1, tn), lambda l, j: (l, 0, j)),
        ],
        out_specs=pl.BlockSpec((None, r, tn), lambda l, j: (l, 0, j)),
        out_shape=jax.ShapeDtypeStruct((depth, r, n), F32),
        compiler_params=_params(2),
        name="adaln_modulation",
    )(cvec, ada_w, ada_b.reshape(depth, 1, n))
    return out.reshape(depth, r, N_MOD, d)


def _mod_spec(d, tm, rows_per_mod, mod_off):
    return pl.BlockSpec((None, N_MOD, d), lambda i: (mod_off + (i * tm) // rows_per_mod, 0, 0))


def _ffn_kernel(x_ref, mod_ref, g_ref, wg_ref, wu_ref, wd_ref, *rest, k0, n_chunks, final):
    if final:
        fg_ref, o_ref = rest
    else:
        (o_ref,) = rest
    x = x_ref[...]
    hb = _norm_mod(x, g_ref[...], mod_ref, k0).astype(BF16)
    d_ff = wg_ref.shape[1]
    tf = d_ff // n_chunks
    y = None
    for j in range(n_chunks):
        a = _dot(hb, wg_ref[:, j * tf:(j + 1) * tf])
        u = _dot(hb, wu_ref[:, j * tf:(j + 1) * tf])
        act = ((a * jax.nn.sigmoid(a)) * u).astype(BF16)
        yj = _dot(act, wd_ref[j * tf:(j + 1) * tf, :])
        y = yj if y is None else y + yj
    gate = mod_ref[k0 + 2:k0 + 3, :]
    o = x + (0.5 * gate) * y
    if final:
        o = _rms(o) * fg_ref[...]
    o_ref[...] = o


def _half_ffn(x, mod, rows_per_mod, mod_off, k0, g, wg, wu, wd, final_g=None, tm=512):
    t, d = x.shape
    d_ff = wg.shape[1]
    tm = min(tm, t)
    assert t % tm == 0 and rows_per_mod % tm == 0
    final = final_g is not None
    in_specs = [
        pl.BlockSpec((tm, d), lambda i: (i, 0)),
        _mod_spec(d, tm, rows_per_mod, mod_off),
        _resident((1, d)),
        _resident((d, d_ff)),
        _resident((d, d_ff)),
        _resident((d_ff, d)),
    ]
    args = [x, mod, g.reshape(1, d), wg, wu, wd]
    if final:
        in_specs.append(_resident((1, d)))
        args.append(final_g.reshape(1, d))
    return pl.pallas_call(
        functools.partial(_ffn_kernel, k0=k0, n_chunks=2, final=final),
        grid=(t // tm,),
        in_specs=in_specs,
        out_specs=pl.BlockSpec((tm, d), lambda i: (i, 0)),
        out_shape=jax.ShapeDtypeStruct((t, d), F32),
        compiler_params=_params(1),
        name="half_ffn_final" if final else "half_ffn",
    )(*args)


def _outproj_kernel(x_ref, mod_ref, w_ref, *rest):
    *y_refs, o_ref = rest
    acc = None
    r0 = 0
    for y_ref in y_refs:
        k = y_ref.shape[1]
        part = _dot(y_ref[...], w_ref[r0:r0 + k, :])
        acc = part if acc is None else acc + part
        r0 += k
    o_ref[...] = x_ref[...] + mod_ref[5:6, :] * acc


def _outproj_residual(x, mod, rows_per_mod, mod_off, w, ys, tm=512):
    t, d = x.shape
    tm = min(tm, t)
    assert t % tm == 0 and rows_per_mod % tm == 0
    in_specs = [
        pl.BlockSpec((tm, d), lambda i: (i, 0)),
        _mod_spec(d, tm, rows_per_mod, mod_off),
        _resident(w.shape),
    ] + [pl.BlockSpec((tm, y.shape[1]), lambda i: (i, 0)) for y in ys]
    return pl.pallas_call(
        _outproj_kernel,
        grid=(t // tm,),
        in_specs=in_specs,
        out_specs=pl.BlockSpec((tm, d), lambda i: (i, 0)),
        out_shape=jax.ShapeDtypeStruct((t, d), F32),
        compiler_params=_params(1),
        name="outproj_residual",
    )(x, mod, w, *ys)


def _proj_ab_kernel(x_ref, mod_ref, g_ref, w_ref, cc_ref, cs_ref, uc_ref, us_ref, q_ref, k_ref, v_ref):
    hb = _norm_mod(x_ref[...], g_ref[...], mod_ref, 3).astype(BF16)
    z = _dot(hb, w_ref[...])
    u = z[:, :FNET_DIM].astype(BF16)
    for g in range(FNET_GROUPS):
        sl = slice(g * FNET_GROUP_DIM, (g + 1) * FNET_GROUP_DIM)
        uc_ref[:, sl] = _dot(u[:, sl], cc_ref[...]).astype(BF16)
        us_ref[:, sl] = _dot(u[:, sl], cs_ref[...]).astype(BF16)
    q_ref[...] = (z[:, FNET_DIM:FNET_DIM + NA_DIM] * (NA_HEAD_DIM ** -0.5)).astype(BF16)
    k_ref[...] = z[:, FNET_DIM + NA_DIM:FNET_DIM + 2 * NA_DIM].astype(BF16)
    v_ref[...] = z[:, FNET_DIM + 2 * NA_DIM:].astype(BF16)


def _dft_real_imag(n, scale):
    j = np.arange(n, dtype=np.int64)
    ang = 2.0 * np.pi * ((j[:, None] * j[None, :]) % n).astype(np.float64) / n
    return (np.cos(ang) * scale).astype(np.float32), (np.sin(ang) * scale).astype(np.float32)


def _proj_ab(x, mod, rows_per_mod, mod_off, g, w_in, tm=512):
    t, d = x.shape
    tm = min(tm, t)
    assert t % tm == 0 and rows_per_mod % tm == 0
    cc, cs = _dft_real_imag(FNET_GROUP_DIM, FNET_GROUP_DIM ** -0.5)
    widths = (FNET_DIM, FNET_DIM, NA_DIM, NA_DIM, NA_DIM)
    return pl.pallas_call(
        _proj_ab_kernel,
        grid=(t // tm,),
        in_specs=[
            pl.BlockSpec((tm, d), lambda i: (i, 0)),
            _mod_spec(d, tm, rows_per_mod, mod_off),
            _resident((1, d)),
            _resident(w_in.shape),
            _resident(cc.shape),
            _resident(cs.shape),
        ],
        out_specs=[pl.BlockSpec((tm, w), lambda i: (i, 0)) for w in widths],
        out_shape=[jax.ShapeDtypeStruct((t, w), BF16) for w in widths],
        compiler_params=_params(1),
        name="proj_fnet_na",
    )(x, mod, g.reshape(1, d), w_in, jnp.asarray(cc).astype(BF16), jnp.asarray(cs).astype(BF16))


def _fnet_kernel(cn_ref, sn_ref, uc_ref, us_ref, o_ref):
    o = _dot(cn_ref[...], uc_ref[...]) - _dot(sn_ref[...], us_ref[...])
    o_ref[...] = o.astype(BF16)


def _position_dft_matrices(n):
    if n <= 512:
        c, s = _dft_real_imag(n, n ** -0.5)
        return jnp.asarray(c).astype(BF16), jnp.asarray(s).astype(BF16)
    n1 = GRID_W
    n2 = n // n1
    k = np.arange(n, dtype=np.int64)
    hi = np.arange(n2, dtype=np.int64)[:, None] * n1
    lo = np.arange(n1, dtype=np.int64)[:, None]
    ang_hi = 2.0 * np.pi * ((hi * k[None, :]) % n).astype(np.float64) / n
    ang_lo = 2.0 * np.pi * ((lo * k[None, :]) % n).astype(np.float64) / n
    ch, sh = jnp.asarray(np.cos(ang_hi), F32)[:, None, :], jnp.asarray(np.sin(ang_hi), F32)[:, None, :]
    cl, sl = jnp.asarray(np.cos(ang_lo), F32)[None, :, :], jnp.asarray(np.sin(ang_lo), F32)[None, :, :]
    scale = n ** -0.5
    c = ((ch * cl - sh * sl) * scale).reshape(n, n).astype(BF16)
    s = ((sh * cl + ch * sl) * scale).reshape(n, n).astype(BF16)
    return c, s


def _fnet_positions(uc, us, batch):
    t, w = uc.shape
    n = t // batch
    cn, sn = _position_dft_matrices(n)
    tn = min(512, n)
    nt = n // tn
    return pl.pallas_call(
        _fnet_kernel,
        grid=(nt, batch),
        in_specs=[
            pl.BlockSpec((tn, n), lambda i, b: (i, 0)),
            pl.BlockSpec((tn, n), lambda i, b: (i, 0)),
            pl.BlockSpec((n, w), lambda i, b: (b, 0)),
            pl.BlockSpec((n, w), lambda i, b: (b, 0)),
        ],
        out_specs=pl.BlockSpec((tn, w), lambda i, b: (b * nt + i, 0)),
        out_shape=jax.ShapeDtypeStruct((t, w), BF16),
        compiler_params=_params(2),
        name="fnet_positions",
    )(cn, sn, uc, us)


def _na_geometry(rows):
    n_blocks = rows // NA_ROWS_PER_BLOCK
    kh = min(NA_KH, rows)
    variants, vi, start_blk = [], [], []
    for blk in range(n_blocks):
        start = int(np.clip(NA_ROWS_PER_BLOCK * blk - kh // 2, 0, rows - NA_WIN_ROWS))
        table = []
        for a in range(NA_ROWS_PER_BLOCK):
            r = NA_ROWS_PER_BLOCK * blk + a
            r0 = int(np.clip(r - kh // 2, 0, rows - kh))
            table.append(tuple((start + i) - r + (NA_KH - 1) if 0 <= (start + i) - r0 < kh else -1
                               for i in range(NA_WIN_ROWS)))
            assert sum(e >= 0 for e in table[-1]) == kh
        table = tuple(table)
        if table not in variants:
            variants.append(table)
        vi.append(variants.index(table))
        start_blk.append(start * GRID_W // NA_KBLK)
    return vi, start_blk, variants


def _na_bias_kernel(rpb_ref, o_ref, *, variants):
    v = pl.program_id(0)
    h = pl.program_id(1)
    n_ro = 2 * NA_KH - 1
    n_co = 2 * NA_KW - 1
    c = lax.broadcasted_iota(jnp.int32, (GRID_W, LANES), 0)
    lane = lax.broadcasted_iota(jnp.int32, (GRID_W, LANES), 1)
    kc = lane & (GRID_W - 1)
    co = kc - c + (NA_KW - 1)
    c0 = jnp.clip(c - NA_KW // 2, 0, GRID_W - NA_KW)
    win = (kc >= c0) & (kc < c0 + NA_KW)
    neg = jnp.full((GRID_W, LANES), NEG_BIG, F32)
    tiles = []
    for ro in range(n_ro):
        t = neg
        for tt in range(n_co):
            t = jnp.where(win & (co == tt), rpb_ref[(h * n_ro + ro) * n_co + tt], t)
        tiles.append(t)
    left = lane < GRID_W
    for vv, table in enumerate(variants):
        @pl.when(v == vv)
        def _(table=table):
            for a in range(NA_ROWS_PER_BLOCK):
                for j in range(NA_WIN_ROWS // 2):
                    r1, r2 = table[a][2 * j], table[a][2 * j + 1]
                    t1 = tiles[r1] if r1 >= 0 else neg
                    t2 = tiles[r2] if r2 >= 0 else neg
                    o_ref[a * GRID_W:(a + 1) * GRID_W, j * LANES:(j + 1) * LANES] = jnp.where(left, t1, t2)


def _na_bias_tables(rpb, variants):
    heads = rpb.shape[0]
    nq = NA_ROWS_PER_BLOCK * GRID_W
    nk = NA_WIN_ROWS * GRID_W
    return pl.pallas_call(
        functools.partial(_na_bias_kernel, variants=variants),
        grid=(len(variants), heads),
        in_specs=[pl.BlockSpec(memory_space=pltpu.SMEM)],
        out_specs=pl.BlockSpec((None, None, nq, nk), lambda v, h: (v, h, 0, 0)),
        out_shape=jax.ShapeDtypeStruct((len(variants), heads, nq, nk), F32),
        compiler_params=_params(2),
        name="na_bias_tables",
    )(rpb.reshape(-1))


def _pair_masks():
    lane = lax.broadcasted_iota(jnp.int32, (1, LANES), 1)
    return [(lane // NA_HEAD_DIM == hh).astype(BF16) for hh in range(LANES // NA_HEAD_DIM)]


def _na_kernel(vi_ref, sb_ref, q_ref, k0_ref, k1_ref, k2_ref, k3_ref, v0_ref, v1_ref, v2_ref, v3_ref,
               kc_ref, vc_ref, bias_ref, o_ref):
    del vi_ref, sb_ref
    k_refs = (k0_ref, k1_ref, k2_ref, k3_ref, kc_ref)
    v_refs = (v0_ref, v1_ref, v2_ref, v3_ref, vc_ref)
    q = q_ref[...]
    out = None
    for hh, hm in enumerate(_pair_masks()):
        qh = q * hm
        s = []
        for j, k_ref in enumerate(k_refs):
            sj = _dot_nt(qh, k_ref[...])
            if j < 4:
                sj = sj + bias_ref[hh, :, j * NA_KBLK:(j + 1) * NA_KBLK]
            s.append(sj)
        m = functools.reduce(jnp.maximum, [jnp.max(sj, axis=-1, keepdims=True) for sj in s])
        acc, l = None, None
        for sj, v_ref in zip(s, v_refs):
            p = jnp.exp(sj - m)
            lj = jnp.sum(p, axis=-1, keepdims=True)
            pv = _dot(p.astype(BF16), v_ref[...] * hm)
            acc = pv if acc is None else acc + pv
            l = lj if l is None else l + lj
        oh = acc / l
        out = oh if out is None else out + oh
    o_ref[...] = out.astype(BF16)


def _neighbourhood_attention(q, k, v, kc, vc, bias, vi, start_blk, batch):
    t, w = q.shape
    s = t // batch
    rows = s // GRID_W
    nq = NA_ROWS_PER_BLOCK * GRID_W
    n_blocks = rows // NA_ROWS_PER_BLOCK
    n_pairs = w // LANES
    lc = kc.shape[0] // batch
    kblk_per_batch = s // NA_KBLK
    heads_per_pair = LANES // NA_HEAD_DIM
    bias5 = bias.reshape(bias.shape[0], n_pairs, heads_per_pair, nq, NA_WIN_ROWS * GRID_W)

    def win_spec(j):
        return pl.BlockSpec((NA_KBLK, LANES),
                            lambda hp, i, b, vi_r, sb_r: (b * kblk_per_batch + sb_r[i] + j, hp))

    grid_spec = pltpu.PrefetchScalarGridSpec(
        num_scalar_prefetch=2,
        grid=(n_pairs, n_blocks, batch),
        in_specs=[pl.BlockSpec((nq, LANES), lambda hp, i, b, vi_r, sb_r: (b * n_blocks + i, hp))]
        + [win_spec(j) for j in range(4)] + [win_spec(j) for j in range(4)]
        + [pl.BlockSpec((lc, LANES), lambda hp, i, b, vi_r, sb_r: (b, hp)),
           pl.BlockSpec((lc, LANES), lambda hp, i, b, vi_r, sb_r: (b, hp)),
           pl.BlockSpec((None, None, heads_per_pair, nq, NA_WIN_ROWS * GRID_W),
                        lambda hp, i, b, vi_r, sb_r: (vi_r[i], hp, 0, 0, 0))],
        out_specs=pl.BlockSpec((nq, LANES), lambda hp, i, b, vi_r, sb_r: (b * n_blocks + i, hp)),
    )
    return pl.pallas_call(
        _na_kernel,
        grid_spec=grid_spec,
        out_shape=jax.ShapeDtypeStruct((t, w), BF16),
        compiler_params=_params(3),
        name="neighbourhood_attention",
    )(jnp.asarray(vi, jnp.int32), jnp.asarray(start_blk, jnp.int32),
      q, k, k, k, k, v, v, v, v, kc, vc, bias5)


def _ctx_attn_kernel(q_ref, k_ref, v_ref, o_ref):
    q = q_ref[...]
    k = k_ref[...]
    v = v_ref[...]
    out = None
    for hm in _pair_masks():
        s = _dot_nt(q * hm, k)
        m = jnp.max(s, axis=-1, keepdims=True)
        p = jnp.exp(s - m)
        l = jnp.sum(p, axis=-1, keepdims=True)
        oh = _dot(p.astype(BF16), v * hm) / l
        out = oh if out is None else out + oh
    o_ref[...] = out.astype(BF16)


def _ctx_dense_attention(q, k, v, batch):
    t, w = q.shape
    lc = t // batch
    spec = pl.BlockSpec((lc, LANES), lambda b, hp: (b, hp))
    return pl.pallas_call(
        _ctx_attn_kernel,
        grid=(batch, w // LANES),
        in_specs=[spec, spec, spec],
        out_specs=spec,
        out_shape=jax.ShapeDtypeStruct((t, w), BF16),
        compiler_params=_params(2),
        name="ctx_dense_attention",
    )(q, k, v)


MLA_QK_PAD = 256


def _rope(x, cos, sin_next, sin_prev):
    quarter = MLA_ROPE // 4
    return (x * cos + pltpu.roll(x, LANES - quarter, 1) * sin_next + pltpu.roll(x, quarter, 1) * sin_prev)


def _proj_mla_kernel(x_ref, mod_ref, g_ref, win_ref, gq_ref, gkv_ref, wqn_ref, wqr_ref, wuk_ref, wuv_ref,
                     *rest, rope, need_q):
    if rope:
        cos_ref, sa_ref, sb_ref, *outs = rest
    else:
        outs = rest
    if need_q:
        q_ref, k_ref, v_ref = outs
    else:
        k_ref, v_ref = outs
    hb = _norm_mod(x_ref[...], g_ref[...], mod_ref, 3).astype(BF16)
    z = _dot(hb, win_ref[...])
    ckv = (_rms(z[:, MLA_Q_RANK:MLA_Q_RANK + MLA_KV_RANK]) * gkv_ref[...]).astype(BF16)
    kr = z[:, MLA_Q_RANK + MLA_KV_RANK:]
    if rope:
        cos, sa, sb = cos_ref[...], sa_ref[...], sb_ref[...]
        kr = _rope(kr, cos, sa, sb)
    kr = kr.astype(BF16)
    kn = _dot(ckv, wuk_ref[...])
    v_ref[...] = _dot(ckv, wuv_ref[...]).astype(BF16)
    for h in range(MLA_HEADS):
        k_ref[:, h * MLA_QK_PAD:h * MLA_QK_PAD + LANES] = kn[:, h * LANES:(h + 1) * LANES].astype(BF16)
        k_ref[:, h * MLA_QK_PAD + LANES:(h + 1) * MLA_QK_PAD] = kr
    if need_q:
        cq = (_rms(z[:, :MLA_Q_RANK]) * gq_ref[...]).astype(BF16)
        scale = (MLA_NOPE + MLA_ROPE) ** -0.5
        qn = _dot(cq, wqn_ref[...])
        qr = _dot(cq, wqr_ref[...])
        for h in range(MLA_HEADS):
            qrh = qr[:, h * LANES:(h + 1) * LANES]
            if rope:
                qrh = _rope(qrh, cos, sa, sb)
            q_ref[:, h * MLA_QK_PAD:h * MLA_QK_PAD + LANES] = (qn[:, h * LANES:(h + 1) * LANES] * scale).astype(BF16)
            q_ref[:, h * MLA_QK_PAD + LANES:(h + 1) * MLA_QK_PAD] = (qrh * scale).astype(BF16)


def _rope_tables(n_tok):
    pos = jnp.arange(n_tok)
    row = (pos // GRID_W).astype(F32)
    col = (pos % GRID_W).astype(F32)
    half = MLA_ROPE // 2
    inv = ROPE_BASE ** (-jnp.arange(0, half, 2, dtype=F32) / half)
    ang_r = row[:, None] * inv[None]
    ang_c = col[:, None] * inv[None]
    ang = jnp.concatenate([ang_r, ang_r, ang_c, ang_c], axis=-1)
    cos, sin = jnp.cos(ang), jnp.sin(ang)
    quarter = MLA_ROPE // 4
    even_q = ((jnp.arange(MLA_ROPE) // quarter) % 2 == 0)[None, :]
    pad = LANES - MLA_ROPE
    cos_p = jnp.pad(cos, ((0, 0), (0, pad)), constant_values=1.0)
    sin_next = jnp.pad(jnp.where(even_q, -sin, 0.0), ((0, 0), (0, pad)))
    sin_prev = jnp.pad(jnp.where(even_q, 0.0, sin), ((0, 0), (0, pad)))
    return cos_p, sin_next, sin_prev


def _proj_mla(x, mod, rows_per_mod, mod_off, g, w, rope_tabs, seq, need_q, tm=512):
    t, d = x.shape
    tm = min(tm, t)
    assert t % tm == 0 and rows_per_mod % tm == 0
    rope = rope_tabs is not None
    in_specs = [
        pl.BlockSpec((tm, d), lambda i: (i, 0)),
        _mod_spec(d, tm, rows_per_mod, mod_off),
        _resident((1, d)),
    ] + [_resident(a.shape) for a in w]
    args = [x, mod, g.reshape(1, d), *w]
    if rope:
        assert seq % tm == 0
        n_rt = seq // tm
        in_specs += [pl.BlockSpec((tm, LANES), lambda i: (i % n_rt, 0))] * 3
        args += list(rope_tabs)
    widths = ([MLA_HEADS * MLA_QK_PAD] if need_q else []) + [MLA_HEADS * MLA_QK_PAD, MLA_HEADS * MLA_V]
    return pl.pallas_call(
        functools.partial(_proj_mla_kernel, rope=rope, need_q=need_q),
        grid=(t // tm,),
        in_specs=in_specs,
        out_specs=[pl.BlockSpec((tm, wd), lambda i: (i, 0)) for wd in widths],
        out_shape=[jax.ShapeDtypeStruct((t, wd), BF16) for wd in widths],
        compiler_params=_params(1),
        name="proj_mla",
    )(*args)


def _mla_attn_kernel(q_ref, *rest, n_src):
    k_refs, v_refs, o_ref = rest[:n_src], rest[n_src:2 * n_src], rest[2 * n_src]
    q = q_ref[...]
    s = [_dot_nt(q, k_ref[...]) for k_ref in k_refs]
    m = functools.reduce(jnp.maximum, [jnp.max(sj, axis=-1, keepdims=True) for sj in s])
    o, l = None, None
    for sj, v_ref in zip(s, v_refs):
        p = jnp.exp(sj - m)
        lj = jnp.sum(p, axis=-1, keepdims=True)
        oj = _dot(p.astype(BF16), v_ref[...])
        o = oj if o is None else o + oj
        l = lj if l is None else l + lj
    o_ref[...] = (o / l).astype(BF16)


def _mla_attention(q, sources, batch, tq=512):
    t = q.shape[0]
    s = t // batch
    tq = min(tq, s)
    nq = s // tq
    k_specs, v_specs = [], []
    for k, _ in sources:
        n = k.shape[0] // batch
        k_specs.append(pl.BlockSpec((n, MLA_QK_PAD), lambda b, h, i: (b, h)))
        v_specs.append(pl.BlockSpec((n, MLA_V), lambda b, h, i: (b, h)))
    return pl.pallas_call(
        functools.partial(_mla_attn_kernel, n_src=len(sources)),
        grid=(batch, MLA_HEADS, nq),
        in_specs=[pl.BlockSpec((tq, MLA_QK_PAD), lambda b, h, i: (b * nq + i, h))] + k_specs + v_specs,
        out_specs=pl.BlockSpec((tq, MLA_V), lambda b, h, i: (b * nq + i, h)),
        out_shape=jax.ShapeDtypeStruct((t, MLA_HEADS * MLA_V), BF16),
        compiler_params=_params(3),
        name="mla_attention",
    )(q, *[k for k, _ in sources], *[v for _, v in sources])


def _mla_weights(w_in, g_q, g_kv, w_uq, w_uk, w_uv):
    w_in_p = jnp.pad(w_in, ((0, 0), (0, 5 * LANES - w_in.shape[1]))).astype(BF16)
    wq = w_uq.reshape(MLA_Q_RANK, MLA_HEADS, MLA_NOPE + MLA_ROPE)
    wqn = wq[:, :, :MLA_NOPE].reshape(MLA_Q_RANK, MLA_HEADS * MLA_NOPE).astype(BF16)
    wqr = jnp.pad(wq[:, :, MLA_NOPE:], ((0, 0), (0, 0), (0, LANES - MLA_ROPE)))
    wqr = wqr.reshape(MLA_Q_RANK, MLA_HEADS * LANES).astype(BF16)
    return (w_in_p, g_q.reshape(1, -1), g_kv.reshape(1, -1), wqn, wqr, w_uk.astype(BF16), w_uv.astype(BF16))


def kernel(x, c, ctx, c_ctx, ada_w, ada_b, norm_g, ffn_w_gate, ffn_w_up, ffn_w_down, ab_w_in, ab_rpb, ab_w_out,
           mla_w_in, mla_g_q, mla_g_kv, mla_w_uq, mla_w_uk, mla_w_uv, mla_w_o, final_g):
    batch, seq, d = x.shape
    lc = ctx.shape[1]
    depth = ada_w.shape[0]
    assert seq % (GRID_W * NA_WIN_ROWS) == 0 and lc % LANES == 0
    t_lat, t_ctx = batch * seq, batch * lc

    n_rows = -(-(batch + 1) // 8) * 8
    cvec = jnp.concatenate([c, c_ctx[None, :], jnp.zeros((n_rows - batch - 1, d), F32)], axis=0)
    mod = _modulation(cvec, ada_w, ada_b)

    x_l = x.reshape(t_lat, d)
    x_c = ctx.reshape(t_ctx, d)
    lat = dict(rows_per_mod=seq, mod_off=0)
    cx = dict(rows_per_mod=t_ctx, mod_off=batch)

    wg, wu, wd = ffn_w_gate.astype(BF16), ffn_w_up.astype(BF16), ffn_w_down.astype(BF16)
    rope_tabs = None

    for layer in range(depth):
        last = layer == depth - 1
        m = mod[layer]
        g = norm_g[layer]
        x_l = _half_ffn(x_l, m, k0=0, g=g[0], wg=wg[layer, 0], wu=wu[layer, 0], wd=wd[layer, 0], **lat)
        x_c = _half_ffn(x_c, m, k0=0, g=g[0], wg=wg[layer, 0], wu=wu[layer, 0], wd=wd[layer, 0], **cx)
        i = layer // 2
        if layer % 2 == 0:
            w_in = ab_w_in[i].astype(BF16)
            w_out = ab_w_out[i].astype(BF16)
            uc_l, us_l, q_l, k_l, v_l = _proj_ab(x_l, m, g=g[1], w_in=w_in, **lat)
            uc_c, us_c, q_c, k_c, v_c = _proj_ab(x_c, m, g=g[1], w_in=w_in, **cx)
            a_l = _fnet_positions(uc_l, us_l, batch)
            vi, start_blk, variants = _na_geometry(seq // GRID_W)
            bias = _na_bias_tables(ab_rpb[i], variants)
            b_l = _neighbourhood_attention(q_l, k_l, v_l, k_c, v_c, bias, vi, start_blk, batch)
            x_l = _outproj_residual(x_l, m, w=w_out, ys=[a_l, b_l], **lat)
            if not last:
                a_c = _fnet_positions(uc_c, us_c, batch)
                b_c = _ctx_dense_attention(q_c, k_c, v_c, batch)
                x_c = _outproj_residual(x_c, m, w=w_out, ys=[a_c, b_c], **cx)
        else:
            if rope_tabs is None:
                rope_tabs = _rope_tables(seq)
            w = _mla_weights(mla_w_in[i], mla_g_q[i], mla_g_kv[i], mla_w_uq[i], mla_w_uk[i], mla_w_uv[i])
            w_o = mla_w_o[i].astype(BF16)
            q_l, k_l, v_l = _proj_mla(x_l, m, g=g[1], w=w, rope_tabs=rope_tabs, seq=seq, need_q=True, **lat)
            outs_c = _proj_mla(x_c, m, g=g[1], w=w, rope_tabs=None, seq=seq, need_q=not last, **cx)
            k_c, v_c = outs_c[-2], outs_c[-1]
            o_l = _mla_attention(q_l, [(k_l, v_l), (k_c, v_c)], batch)
            x_l = _outproj_residual(x_l, m, w=w_o, ys=[o_l], **lat)
            if not last:
                o_c = _mla_attention(outs_c[0], [(k_c, v_c)], batch)
                x_c = _outproj_residual(x_c, m, w=w_o, ys=[o_c], **cx)
        x_l = _half_ffn(x_l, m, k0=6, g=g[2], wg=wg[layer, 1], wu=wu[layer, 1], wd=wd[layer, 1],
                        final_g=final_g if last else None, **lat)
        if not last:
            x_c = _half_ffn(x_c, m, k0=6, g=g[2], wg=wg[layer, 1], wu=wu[layer, 1], wd=wd[layer, 1], **cx)
    return x_l.reshape(batch, seq, d)
```

```python
import functools
import math

import numpy as np
import jax
import jax.numpy as jnp
from jax import lax
from jax.experimental import pallas as pl
from jax.experimental.pallas import tpu as pltpu

F32 = jnp.float32
BF16 = jnp.bfloat16

GRID_W = 64
EPS = 1e-6
N_MOD = 9
FNET_GROUPS = 4
FNET_GROUP_DIM = 128
FNET_DIM = FNET_GROUPS * FNET_GROUP_DIM
NA_HEADS = 8
NA_HEAD_DIM = 64
NA_DIM = NA_HEADS * NA_HEAD_DIM
NA_KH = 8
NA_KW = 16
MLA_HEADS = 8
MLA_NOPE = 128
MLA_ROPE = 64
MLA_V = 128
MLA_Q_RANK = 384
MLA_KV_RANK = 128
ROPE_BASE = 10000.0

LANES = 128
VMEM_LIMIT_BYTES = 56 * 1024 * 1024
NEG_BIG = -1e30
LOG2_E = math.log2(math.e)

NA_ROWS_PER_BLOCK = 8
NA_WIN_ROWS = 16
NA_KBLK = 256


def _params(n_axes):
    return pltpu.CompilerParams(
        dimension_semantics=("arbitrary",) * n_axes,
        vmem_limit_bytes=VMEM_LIMIT_BYTES,
    )


def _resident(shape):
    nd = len(shape)
    return pl.BlockSpec(shape, lambda *_: (0,) * nd, pipeline_mode=pl.Buffered(1))


def _dot(a, b):
    return jnp.dot(a, b, preferred_element_type=F32)


def _dot_nt(a, b):
    return lax.dot_general(a, b, (((1,), (1,)), ((), ())), preferred_element_type=F32)


def _rms(x):
    return x * lax.rsqrt(jnp.mean(x * x, axis=-1, keepdims=True) + EPS)


def _norm_mod(x, g, mod_ref, k0):
    shift = mod_ref[k0:k0 + 1, :]
    scale = mod_ref[k0 + 1:k0 + 2, :]
    return (_rms(x) * g) * (1.0 + scale) + shift


def _mod_kernel(c_ref, w_ref, b_ref, o_ref):
    c = c_ref[...]
    sc = c * jax.nn.sigmoid(c)
    o_ref[...] = jnp.dot(sc, w_ref[...], preferred_element_type=F32,
                         precision=lax.Precision.HIGHEST) + b_ref[...]


def _modulation(cvec, ada_w, ada_b):
    depth, d, n = ada_w.shape
    r = cvec.shape[0]
    tn = 1536
    assert n % tn == 0
    out = pl.pallas_call(
        _mod_kernel,
        grid=(depth, n // tn),
        in_specs=[
            pl.BlockSpec((r, d), lambda l, j: (0, 0)),
            pl.BlockSpec((None, d, tn), lambda l, j: (l, 0, j)),
            pl.BlockSpec((None, 1, tn), lambda l, j: (l, 0, j)),
        ],
        out_specs=pl.BlockSpec((None, r, tn), lambda l, j: (l, 0, j)),
        out_shape=jax.ShapeDtypeStruct((depth, r, n), F32),
        compiler_params=_params(2),
        name="adaln_modulation",
    )(cvec, ada_w, ada_b.reshape(depth, 1, n))
    return out.reshape(depth, r, N_MOD, d)


def _mod_spec(d, tm, rows_per_mod, mod_off):
    return pl.BlockSpec((None, N_MOD, d), lambda i: (mod_off + (i * tm) // rows_per_mod, 0, 0))


class _Section:
    def __init__(self, body, inputs=(), outputs=(), writes_x=False):
        self.body, self.inputs, self.outputs, self.writes_x = body, list(inputs), list(outputs), writes_x


def _stage_kernel(*refs, sections, writes_x):
    x_ref, mod_ref = refs[0], refs[1]
    n_in = sum(len(s.inputs) for s in sections)
    in_refs = refs[2:2 + n_in]
    out_refs = refs[2 + n_in:]
    x = x_ref[...]
    xo_ref = None
    if writes_x:
        xo_ref, out_refs = out_refs[0], out_refs[1:]
    i0 = o0 = 0
    for s in sections:
        x = s.body(x, mod_ref, in_refs[i0:i0 + len(s.inputs)], out_refs[o0:o0 + len(s.outputs)])
        i0 += len(s.inputs)
        o0 += len(s.outputs)
    if writes_x:
        xo_ref[...] = x


def _token_stage(x, mod, rows_per_mod, mod_off, sections, name, seq=None, tm=512):
    t, d = x.shape
    tm = min(tm, t)
    assert t % tm == 0 and rows_per_mod % tm == 0
    writes_x = any(s.writes_x for s in sections)
    in_specs = [pl.BlockSpec((tm, d), lambda i: (i, 0)), _mod_spec(d, tm, rows_per_mod, mod_off)]
    args = [x, mod]
    for s in sections:
        for arr, kind in s.inputs:
            if kind == "res":
                in_specs.append(_resident(arr.shape))
            elif kind == "row":
                in_specs.append(pl.BlockSpec((tm, arr.shape[1]), lambda i: (i, 0)))
            else:
                assert kind == "rope" and seq % tm == 0
                n_rt = seq // tm
                in_specs.append(pl.BlockSpec((tm, arr.shape[1]), lambda i: (i % n_rt, 0)))
            args.append(arr)
    out_specs, out_shape = [], []
    if writes_x:
        out_specs.append(pl.BlockSpec((tm, d), lambda i: (i, 0)))
        out_shape.append(jax.ShapeDtypeStruct((t, d), F32))
    for s in sections:
        for w, kind in s.outputs:
            if kind == "row":
                out_specs.append(pl.BlockSpec((tm, w), lambda i: (i, 0)))
                out_shape.append(jax.ShapeDtypeStruct((t, w), BF16))
            else:
                out_specs.append(pl.BlockSpec((w, tm), lambda i: (0, i)))
                out_shape.append(jax.ShapeDtypeStruct((w, t), BF16))
    return pl.pallas_call(
        functools.partial(_stage_kernel, sections=sections, writes_x=writes_x),
        grid=(t // tm,),
        in_specs=in_specs,
        out_specs=out_specs,
        out_shape=out_shape,
        compiler_params=_params(1),
        name=name,
    )(*args)


FFN_CHUNKS = 2


def _sec_half_ffn(k0, g, wg, wu, wd):
    def body(x, mod_ref, in_refs, out_refs):
        g_ref, wg_ref, wu_ref, wd_ref = in_refs
        hb = _norm_mod(x, g_ref[...], mod_ref, k0).astype(BF16)
        tf = wg_ref.shape[1] // FFN_CHUNKS
        y = None
        for j in range(FFN_CHUNKS):
            a = _dot(hb, wg_ref[:, j * tf:(j + 1) * tf])
            u = _dot(hb, wu_ref[:, j * tf:(j + 1) * tf])
            act = ((a * jax.nn.sigmoid(a)) * u).astype(BF16)
            yj = _dot(act, wd_ref[j * tf:(j + 1) * tf, :])
            y = yj if y is None else y + yj
        return x + (0.5 * mod_ref[k0 + 2:k0 + 3, :]) * y

    assert wg.shape[1] % (FFN_CHUNKS * LANES) == 0
    return _Section(body, [(g.reshape(1, -1), "res"), (wg, "res"), (wu, "res"), (wd, "res")], writes_x=True)


def _sec_outproj(w, ys):
    def body(x, mod_ref, in_refs, out_refs):
        w_ref, *y_refs = in_refs
        acc, r0 = None, 0
        for y_ref in y_refs:
            k = y_ref.shape[1]
            part = _dot(y_ref[...], w_ref[r0:r0 + k, :])
            acc = part if acc is None else acc + part
            r0 += k
        return x + mod_ref[5:6, :] * acc

    return _Section(body, [(w, "res")] + [(y, "row") for y in ys], writes_x=True)


def _sec_final_norm(g):
    def body(x, mod_ref, in_refs, out_refs):
        return _rms(x) * in_refs[0][...]

    return _Section(body, [(g.reshape(1, -1), "res")], writes_x=True)


def _proj_ab_body(x, mod_ref, in_refs, out_refs):
    g_ref, w_ref, wvt_ref, cc_ref, cs_ref = in_refs
    uc_ref, us_ref, q_ref, k_ref, vt_ref = out_refs
    hb = _norm_mod(x, g_ref[...], mod_ref, 3).astype(BF16)
    z = _dot(hb, w_ref[...])
    u = z[:, :FNET_DIM].astype(BF16)
    for g in range(FNET_GROUPS):
        sl = slice(g * FNET_GROUP_DIM, (g + 1) * FNET_GROUP_DIM)
        uc_ref[:, sl] = _dot(u[:, sl], cc_ref[...]).astype(BF16)
        us_ref[:, sl] = _dot(u[:, sl], cs_ref[...]).astype(BF16)
    q_ref[...] = (z[:, FNET_DIM:FNET_DIM + NA_DIM] * (NA_HEAD_DIM ** -0.5 * LOG2_E)).astype(BF16)
    k_ref[...] = z[:, FNET_DIM + NA_DIM:].astype(BF16)
    vt_ref[...] = _dot_nt(wvt_ref[...], hb).astype(BF16)
    return x


def _dft_real_imag(n, scale):
    j = np.arange(n, dtype=np.int64)
    ang = 2.0 * np.pi * ((j[:, None] * j[None, :]) % n).astype(np.float64) / n
    return (np.cos(ang) * scale).astype(np.float32), (np.sin(ang) * scale).astype(np.float32)


def _sec_proj_ab(g, w_in):
    cc, cs = _dft_real_imag(FNET_GROUP_DIM, FNET_GROUP_DIM ** -0.5)
    n_row = FNET_DIM + 2 * NA_DIM
    w_row = w_in[:, :n_row].astype(BF16)
    w_vt = w_in[:, n_row:].T.astype(BF16)
    inputs = [(g.reshape(1, -1), "res"), (w_row, "res"), (w_vt, "res"),
              (jnp.asarray(cc).astype(BF16), "res"), (jnp.asarray(cs).astype(BF16), "res")]
    outputs = [(FNET_DIM, "row"), (FNET_DIM, "row"), (NA_DIM, "row"), (NA_DIM, "row"), (NA_DIM, "col")]
    return _Section(_proj_ab_body, inputs, outputs)


def _fnet_kernel(cn_ref, sn_ref, uc_ref, us_ref, o_ref):
    o = _dot(cn_ref[...], uc_ref[...]) - _dot(sn_ref[...], us_ref[...])
    o_ref[...] = o.astype(BF16)


def _position_dft_matrices(n):
    if n <= 512:
        c, s = _dft_real_imag(n, n ** -0.5)
        return jnp.asarray(c).astype(BF16), jnp.asarray(s).astype(BF16)
    n1 = GRID_W
    n2 = n // n1
    k = np.arange(n, dtype=np.int64)
    hi = np.arange(n2, dtype=np.int64)[:, None] * n1
    lo = np.arange(n1, dtype=np.int64)[:, None]
    ang_hi = 2.0 * np.pi * ((hi * k[None, :]) % n).astype(np.float64) / n
    ang_lo = 2.0 * np.pi * ((lo * k[None, :]) % n).astype(np.float64) / n
    ch, sh = jnp.asarray(np.cos(ang_hi), F32)[:, None, :], jnp.asarray(np.sin(ang_hi), F32)[:, None, :]
    cl, sl = jnp.asarray(np.cos(ang_lo), F32)[None, :, :], jnp.asarray(np.sin(ang_lo), F32)[None, :, :]
    scale = n ** -0.5
    c = ((ch * cl - sh * sl) * scale).reshape(n, n).astype(BF16)
    s = ((sh * cl + ch * sl) * scale).reshape(n, n).astype(BF16)
    return c, s


def _fnet_positions(uc, us, batch):
    t, w = uc.shape
    n = t // batch
    cn, sn = _position_dft_matrices(n)
    tn = min(512, n)
    nt = n // tn
    return pl.pallas_call(
        _fnet_kernel,
        grid=(nt, batch),
        in_specs=[
            pl.BlockSpec((tn, n), lambda i, b: (i, 0)),
            pl.BlockSpec((tn, n), lambda i, b: (i, 0)),
            pl.BlockSpec((n, w), lambda i, b: (b, 0)),
            pl.BlockSpec((n, w), lambda i, b: (b, 0)),
        ],
        out_specs=pl.BlockSpec((tn, w), lambda i, b: (b * nt + i, 0)),
        out_shape=jax.ShapeDtypeStruct((t, w), BF16),
        compiler_params=_params(2),
        name="fnet_positions",
    )(cn, sn, uc, us)


def _na_geometry(rows):
    n_blocks = rows // NA_ROWS_PER_BLOCK
    kh = min(NA_KH, rows)
    variants, vi, start_blk = [], [], []
    for blk in range(n_blocks):
        start = int(np.clip(NA_ROWS_PER_BLOCK * blk - kh // 2, 0, rows - NA_WIN_ROWS))
        table = []
        for a in range(NA_ROWS_PER_BLOCK):
            r = NA_ROWS_PER_BLOCK * blk + a
            r0 = int(np.clip(r - kh // 2, 0, rows - kh))
            table.append(tuple((start + i) - r + (NA_KH - 1) if 0 <= (start + i) - r0 < kh else -1
                               for i in range(NA_WIN_ROWS)))
            assert sum(e >= 0 for e in table[-1]) == kh
        table = tuple(table)
        if table not in variants:
            variants.append(table)
        vi.append(variants.index(table))
        start_blk.append(start * GRID_W // NA_KBLK)
    return vi, start_blk, variants


def _na_bias_kernel(rpb_ref, o_ref, *, variants):
    v = pl.program_id(0)
    h = pl.program_id(1)
    n_ro = 2 * NA_KH - 1
    n_co = 2 * NA_KW - 1
    kc = lax.broadcasted_iota(jnp.int32, (GRID_W, LANES), 0)
    lane = lax.broadcasted_iota(jnp.int32, (GRID_W, LANES), 1)
    c = lane & (GRID_W - 1)
    co = kc - c + (NA_KW - 1)
    c0 = jnp.clip(c - NA_KW // 2, 0, GRID_W - NA_KW)
    win = (kc >= c0) & (kc < c0 + NA_KW)
    neg = jnp.full((GRID_W, LANES), NEG_BIG, F32)
    tiles = []
    for ro in range(n_ro):
        t = neg
        for tt in range(n_co):
            t = jnp.where(win & (co == tt), rpb_ref[(h * n_ro + ro) * n_co + tt] * LOG2_E, t)
        tiles.append(t)
    left = lane < GRID_W
    for vv, table in enumerate(variants):
        @pl.when(v == vv)
        def _(table=table):
            for i in range(NA_WIN_ROWS):
                for j in range(NA_ROWS_PER_BLOCK // 2):
                    r1, r2 = table[2 * j][i], table[2 * j + 1][i]
                    t1 = tiles[r1] if r1 >= 0 else neg
                    t2 = tiles[r2] if r2 >= 0 else neg
                    o_ref[i * GRID_W:(i + 1) * GRID_W, j * LANES:(j + 1) * LANES] = jnp.where(left, t1, t2)


def _na_bias_tables(rpb, variants):
    heads = rpb.shape[0]
    nq = NA_ROWS_PER_BLOCK * GRID_W
    nk = NA_WIN_ROWS * GRID_W
    return pl.pallas_call(
        functools.partial(_na_bias_kernel, variants=variants),
        grid=(len(variants), heads),
        in_specs=[pl.BlockSpec(memory_space=pltpu.SMEM)],
        out_specs=pl.BlockSpec((None, None, nk, nq), lambda v, h: (v, h, 0, 0)),
        out_shape=jax.ShapeDtypeStruct((len(variants), heads, nk, nq), F32),
        compiler_params=_params(2),
        name="na_bias_tables",
    )(rpb.reshape(-1))


def _pair_masks():
    lane = lax.broadcasted_iota(jnp.int32, (1, LANES), 1)
    return [(lane // NA_HEAD_DIM == hh).astype(BF16) for hh in range(LANES // NA_HEAD_DIM)]


def _na_kernel(vi_ref, sb_ref, q_ref, k0_ref, k1_ref, k2_ref, k3_ref, v0_ref, v1_ref, v2_ref, v3_ref,
               kc_ref, vc_ref, bias_ref, o_ref):
    del vi_ref, sb_ref
    k_refs = (k0_ref, k1_ref, k2_ref, k3_ref, kc_ref)
    vt_refs = (v0_ref, v1_ref, v2_ref, v3_ref, vc_ref)
    q = q_ref[...]
    masks = _pair_masks()

    def scores(hh):
        qh = q * masks[hh]
        s = []
        for j, k_ref in enumerate(k_refs):
            sj = _dot_nt(k_ref[...], qh)
            if j < 4:
                sj = sj + bias_ref[hh, j * NA_KBLK:(j + 1) * NA_KBLK, :]
            s.append(sj)
        return s

    outs = []
    s = scores(0)
    for hh in range(len(masks)):
        s_next = scores(hh + 1) if hh + 1 < len(masks) else None
        m = functools.reduce(jnp.maximum, [jnp.max(sj, axis=0, keepdims=True) for sj in s])
        acc, l = None, None
        for sj, vt_ref in zip(s, vt_refs):
            p = jnp.exp2(sj - m)
            lj = jnp.sum(p, axis=0, keepdims=True)
            pv = _dot(vt_ref[hh * NA_HEAD_DIM:(hh + 1) * NA_HEAD_DIM, :], p.astype(BF16))
            acc = pv if acc is None else acc + pv
            l = lj if l is None else l + lj
        outs.append(acc / l)
        s = s_next
    o_ref[...] = jnp.transpose(jnp.concatenate(outs, axis=0)).astype(BF16)


def _neighbourhood_attention(q, k, vt, kc, vct, bias, vi, start_blk, batch):
    t, w = q.shape
    s = t // batch
    rows = s // GRID_W
    nq = NA_ROWS_PER_BLOCK * GRID_W
    n_blocks = rows // NA_ROWS_PER_BLOCK
    n_pairs = w // LANES
    lc = kc.shape[0] // batch
    kblk_per_batch = s // NA_KBLK
    heads_per_pair = LANES // NA_HEAD_DIM
    bias5 = bias.reshape(bias.shape[0], n_pairs, heads_per_pair, NA_WIN_ROWS * GRID_W, nq)

    def k_spec(j):
        return pl.BlockSpec((NA_KBLK, LANES),
                            lambda hp, i, b, vi_r, sb_r: (b * kblk_per_batch + sb_r[i] + j, hp))

    def vt_spec(j):
        return pl.BlockSpec((LANES, NA_KBLK),
                            lambda hp, i, b, vi_r, sb_r: (hp, b * kblk_per_batch + sb_r[i] + j))

    grid_spec = pltpu.PrefetchScalarGridSpec(
        num_scalar_prefetch=2,
        grid=(n_pairs, n_blocks, batch),
        in_specs=[pl.BlockSpec((nq, LANES), lambda hp, i, b, vi_r, sb_r: (b * n_blocks + i, hp))]
        + [k_spec(j) for j in range(4)] + [vt_spec(j) for j in range(4)]
        + [pl.BlockSpec((lc, LANES), lambda hp, i, b, vi_r, sb_r: (b, hp)),
           pl.BlockSpec((LANES, lc), lambda hp, i, b, vi_r, sb_r: (hp, b)),
           pl.BlockSpec((None, None, heads_per_pair, NA_WIN_ROWS * GRID_W, nq),
                        lambda hp, i, b, vi_r, sb_r: (vi_r[i], hp, 0, 0, 0))],
        out_specs=pl.BlockSpec((nq, LANES), lambda hp, i, b, vi_r, sb_r: (b * n_blocks + i, hp)),
    )
    return pl.pallas_call(
        _na_kernel,
        grid_spec=grid_spec,
        out_shape=jax.ShapeDtypeStruct((t, w), BF16),
        compiler_params=_params(3),
        name="neighbourhood_attention",
    )(jnp.asarray(vi, jnp.int32), jnp.asarray(start_blk, jnp.int32),
      q, k, k, k, k, vt, vt, vt, vt, kc, vct, bias5)


def _ctx_attn_kernel(q_ref, k_ref, vt_ref, o_ref):
    q = q_ref[...]
    k = k_ref[...]
    outs = []
    for hh, hm in enumerate(_pair_masks()):
        st = _dot_nt(k, q * hm)
        m = jnp.max(st, axis=0, keepdims=True)
        p = jnp.exp2(st - m)
        l = jnp.sum(p, axis=0, keepdims=True)
        outs.append(_dot(vt_ref[hh * NA_HEAD_DIM:(hh + 1) * NA_HEAD_DIM, :], p.astype(BF16)) / l)
    o_ref[...] = jnp.transpose(jnp.concatenate(outs, axis=0)).astype(BF16)


def _ctx_dense_attention(q, k, vt, batch):
    t, w = q.shape
    lc = t // batch
    spec = pl.BlockSpec((lc, LANES), lambda b, hp: (b, hp))
    return pl.pallas_call(
        _ctx_attn_kernel,
        grid=(batch, w // LANES),
        in_specs=[spec, spec, pl.BlockSpec((LANES, lc), lambda b, hp: (hp, b))],
        out_specs=spec,
        out_shape=jax.ShapeDtypeStruct((t, w), BF16),
        compiler_params=_params(2),
        name="ctx_dense_attention",
    )(q, k, vt)


MLA_QK_PAD = 256


def _rope(x, cos, sin_next, sin_prev):
    quarter = MLA_ROPE // 4
    return (x * cos + pltpu.roll(x, LANES - quarter, 1) * sin_next + pltpu.roll(x, quarter, 1) * sin_prev)


def _proj_mla_body(x, mod_ref, in_refs, out_refs, *, rope, need_q):
    g_ref, win_ref, gq_ref, gkv_ref, wqn_ref, wqr_ref, wuk_ref, wuv_ref, *rope_refs = in_refs
    if rope:
        cos_ref, sa_ref, sb_ref = rope_refs
    if need_q:
        q_ref, k_ref, v_ref = out_refs
    else:
        k_ref, v_ref = out_refs
    hb = _norm_mod(x, g_ref[...], mod_ref, 3).astype(BF16)
    z = _dot(hb, win_ref[...])
    ckv = (_rms(z[:, MLA_Q_RANK:MLA_Q_RANK + MLA_KV_RANK]) * gkv_ref[...]).astype(BF16)
    kr = z[:, MLA_Q_RANK + MLA_KV_RANK:]
    if rope:
        cos, sa, sb = cos_ref[...], sa_ref[...], sb_ref[...]
        kr = _rope(kr, cos, sa, sb)
    kr = kr.astype(BF16)
    kn = _dot(ckv, wuk_ref[...])
    v_ref[...] = _dot_nt(wuv_ref[...], ckv).astype(BF16)
    for h in range(MLA_HEADS):
        k_ref[:, h * MLA_QK_PAD:h * MLA_QK_PAD + LANES] = kn[:, h * LANES:(h + 1) * LANES].astype(BF16)
        k_ref[:, h * MLA_QK_PAD + LANES:(h + 1) * MLA_QK_PAD] = kr
    if need_q:
        cq = (_rms(z[:, :MLA_Q_RANK]) * gq_ref[...]).astype(BF16)
        scale = (MLA_NOPE + MLA_ROPE) ** -0.5 * LOG2_E
        qn = _dot(cq, wqn_ref[...])
        qr = _dot(cq, wqr_ref[...])
        for h in range(MLA_HEADS):
            qrh = qr[:, h * LANES:(h + 1) * LANES]
            if rope:
                qrh = _rope(qrh, cos, sa, sb)
            q_ref[:, h * MLA_QK_PAD:h * MLA_QK_PAD + LANES] = (qn[:, h * LANES:(h + 1) * LANES] * scale).astype(BF16)
            q_ref[:, h * MLA_QK_PAD + LANES:(h + 1) * MLA_QK_PAD] = (qrh * scale).astype(BF16)
    return x


def _rope_tables(n_tok):
    pos = jnp.arange(n_tok)
    row = (pos // GRID_W).astype(F32)
    col = (pos % GRID_W).astype(F32)
    half = MLA_ROPE // 2
    inv = ROPE_BASE ** (-jnp.arange(0, half, 2, dtype=F32) / half)
    ang_r = row[:, None] * inv[None]
    ang_c = col[:, None] * inv[None]
    ang = jnp.concatenate([ang_r, ang_r, ang_c, ang_c], axis=-1)
    cos, sin = jnp.cos(ang), jnp.sin(ang)
    quarter = MLA_ROPE // 4
    even_q = ((jnp.arange(MLA_ROPE) // quarter) % 2 == 0)[None, :]
    pad = LANES - MLA_ROPE
    cos_p = jnp.pad(cos, ((0, 0), (0, pad)), constant_values=1.0)
    sin_next = jnp.pad(jnp.where(even_q, -sin, 0.0), ((0, 0), (0, pad)))
    sin_prev = jnp.pad(jnp.where(even_q, 0.0, sin), ((0, 0), (0, pad)))
    return cos_p, sin_next, sin_prev


def _sec_proj_mla(g, w, rope_tabs, need_q):
    rope = rope_tabs is not None
    inputs = [(g.reshape(1, -1), "res")] + [(a, "res") for a in w]
    if rope:
        inputs += [(tab, "rope") for tab in rope_tabs]
    qk_w = MLA_HEADS * MLA_QK_PAD
    outputs = ([(qk_w, "row")] if need_q else []) + [(qk_w, "row"), (MLA_HEADS * MLA_V, "col")]
    return _Section(functools.partial(_proj_mla_body, rope=rope, need_q=need_q), inputs, outputs)


def _mla_attn_kernel(q_ref, *rest, n_src, chunk):
    k_refs, vt_refs, o_ref = rest[:n_src], rest[n_src:2 * n_src], rest[2 * n_src]
    q = q_ref[...]
    chunks = [(k_ref, vt_ref, c0, min(k_ref.shape[0], c0 + chunk))
              for k_ref, vt_ref in zip(k_refs, vt_refs) for c0 in range(0, k_ref.shape[0], chunk)]

    def scores(j):
        k_ref, _, c0, c1 = chunks[j]
        return _dot_nt(k_ref[c0:c1, :], q)

    m = l = acc = None
    st = scores(0)
    for j, (_, vt_ref, c0, c1) in enumerate(chunks):
        st_next = scores(j + 1) if j + 1 < len(chunks) else None
        mj = jnp.max(st, axis=0, keepdims=True)
        if m is None:
            m_new = mj
        else:
            m_new = jnp.maximum(m, mj)
            alpha = jnp.exp2(m - m_new)
        p = jnp.exp2(st - m_new)
        lj = jnp.sum(p, axis=0, keepdims=True)
        oj = _dot(vt_ref[:, c0:c1], p.astype(BF16))
        if m is None:
            l, acc = lj, oj
        else:
            l, acc = alpha * l + lj, alpha * acc + oj
        m, st = m_new, st_next
    o_ref[...] = jnp.transpose(acc / l).astype(BF16)


def _mla_attention(q, sources, batch, tq=1024, chunk=1024):
    t = q.shape[0]
    s = t // batch
    tq = min(tq, s)
    nq = s // tq
    k_specs, v_specs = [], []
    for k, _ in sources:
        n = k.shape[0] // batch
        k_specs.append(pl.BlockSpec((n, MLA_QK_PAD), lambda b, h, i: (b, h)))
        v_specs.append(pl.BlockSpec((MLA_V, n), lambda b, h, i: (h, b)))
    return pl.pallas_call(
        functools.partial(_mla_attn_kernel, n_src=len(sources), chunk=chunk),
        grid=(batch, MLA_HEADS, nq),
        in_specs=[pl.BlockSpec((tq, MLA_QK_PAD), lambda b, h, i: (b * nq + i, h))] + k_specs + v_specs,
        out_specs=pl.BlockSpec((tq, MLA_V), lambda b, h, i: (b * nq + i, h)),
        out_shape=jax.ShapeDtypeStruct((t, MLA_HEADS * MLA_V), BF16),
        compiler_params=_params(3),
        name="mla_attention",
    )(q, *[k for k, _ in sources], *[v for _, v in sources])


def _mla_weights(w_in, g_q, g_kv, w_uq, w_uk, w_uv):
    w_in_p = jnp.pad(w_in, ((0, 0), (0, 5 * LANES - w_in.shape[1]))).astype(BF16)
    wq = w_uq.reshape(MLA_Q_RANK, MLA_HEADS, MLA_NOPE + MLA_ROPE)
    wqn = wq[:, :, :MLA_NOPE].reshape(MLA_Q_RANK, MLA_HEADS * MLA_NOPE).astype(BF16)
    wqr = jnp.pad(wq[:, :, MLA_NOPE:], ((0, 0), (0, 0), (0, LANES - MLA_ROPE)))
    wqr = wqr.reshape(MLA_Q_RANK, MLA_HEADS * LANES).astype(BF16)
    return (w_in_p, g_q.reshape(1, -1), g_kv.reshape(1, -1), wqn, wqr, w_uk.astype(BF16), w_uv.T.astype(BF16))


def kernel(x, c, ctx, c_ctx, ada_w, ada_b, norm_g, ffn_w_gate, ffn_w_up, ffn_w_down, ab_w_in, ab_rpb, ab_w_out,
           mla_w_in, mla_g_q, mla_g_kv, mla_w_uq, mla_w_uk, mla_w_uv, mla_w_o, final_g):
    batch, seq, d = x.shape
    lc = ctx.shape[1]
    depth = ada_w.shape[0]
    assert seq % (GRID_W * NA_WIN_ROWS) == 0 and lc % LANES == 0
    t_lat, t_ctx = batch * seq, batch * lc

    n_rows = -(-(batch + 1) // 8) * 8
    cvec = jnp.concatenate([c, c_ctx[None, :], jnp.zeros((n_rows - batch - 1, d), F32)], axis=0)
    mod = _modulation(cvec, ada_w, ada_b)

    x_l = x.reshape(t_lat, d)
    x_c = ctx.reshape(t_ctx, d)
    lat = dict(rows_per_mod=seq, mod_off=0)
    cx = dict(rows_per_mod=t_ctx, mod_off=batch)

    wg, wu, wd = ffn_w_gate.astype(BF16), ffn_w_up.astype(BF16), ffn_w_down.astype(BF16)
    rope_tabs = None

    for layer in range(depth):
        last = layer == depth - 1
        m = mod[layer]
        g = norm_g[layer]
        ffn1 = _sec_half_ffn(0, g[0], wg[layer, 0], wu[layer, 0], wd[layer, 0])
        ffn2 = _sec_half_ffn(6, g[2], wg[layer, 1], wu[layer, 1], wd[layer, 1])
        i = layer // 2
        if layer % 2 == 0:
            proj = _sec_proj_ab(g[1], ab_w_in[i])
            w_out = ab_w_out[i].astype(BF16)
            x_l, uc_l, us_l, q_l, k_l, vt_l = _token_stage(x_l, m, sections=[ffn1, proj], name="ffn_proj_ab", **lat)
            x_c, uc_c, us_c, q_c, k_c, vt_c = _token_stage(x_c, m, sections=[ffn1, proj], name="ffn_proj_ab", **cx)
            a_l = _fnet_positions(uc_l, us_l, batch)
            vi, start_blk, variants = _na_geometry(seq // GRID_W)
            bias = _na_bias_tables(ab_rpb[i], variants)
            b_l = _neighbourhood_attention(q_l, k_l, vt_l, k_c, vt_c, bias, vi, start_blk, batch)
            ys_l = [a_l, b_l]
            if not last:
                a_c = _fnet_positions(uc_c, us_c, batch)
                b_c = _ctx_dense_attention(q_c, k_c, vt_c, batch)
                ys_c = [a_c, b_c]
        else:
            if rope_tabs is None:
                rope_tabs = _rope_tables(seq)
            w = _mla_weights(mla_w_in[i], mla_g_q[i], mla_g_kv[i], mla_w_uq[i], mla_w_uk[i], mla_w_uv[i])
            w_out = mla_w_o[i].astype(BF16)
            x_l, q_l, k_l, vt_l = _token_stage(
                x_l, m, sections=[ffn1, _sec_proj_mla(g[1], w, rope_tabs, True)], name="ffn_proj_mla", seq=seq, **lat)
            x_c, *q_c, k_c, vt_c = _token_stage(
                x_c, m, sections=[ffn1, _sec_proj_mla(g[1], w, None, not last)], name="ffn_proj_mla", **cx)
            ys_l = [_mla_attention(q_l, [(k_l, vt_l), (k_c, vt_c)], batch)]
            if not last:
                ys_c = [_mla_attention(q_c[0], [(k_c, vt_c)], batch)]
        tail = [_sec_final_norm(final_g)] if last else []
        (x_l,) = _token_stage(x_l, m, sections=[_sec_outproj(w_out, ys_l), ffn2] + tail, name="outproj_ffn", **lat)
        if not last:
            (x_c,) = _token_stage(x_c, m, sections=[_sec_outproj(w_out, ys_c), ffn2], name="outproj_ffn", **cx)
    return x_l.reshape(batch, seq, d)
```

```python
import functools
import math

import numpy as np
import jax
import jax.numpy as jnp
from jax import lax
from jax.experimental import pallas as pl
from jax.experimental.pallas import tpu as pltpu

F32 = jnp.float32
BF16 = jnp.bfloat16

GRID_W = 64
EPS = 1e-6
N_MOD = 9
FNET_GROUPS = 4
FNET_GROUP_DIM = 128
FNET_DIM = FNET_GROUPS * FNET_GROUP_DIM
NA_HEADS = 8
NA_HEAD_DIM = 64
NA_DIM = NA_HEADS * NA_HEAD_DIM
NA_KH = 8
NA_KW = 16
MLA_HEADS = 8
MLA_NOPE = 128
MLA_ROPE = 64
MLA_V = 128
MLA_Q_RANK = 384
MLA_KV_RANK = 128
ROPE_BASE = 10000.0

LANES = 128
VMEM_LIMIT_BYTES = 56 * 1024 * 1024
NEG_BIG = -1e30
LOG2_E = math.log2(math.e)

NA_ROWS_PER_BLOCK = 8
NA_WIN_ROWS = 16
NA_KBLK = 256


def _params(n_axes):
    return pltpu.CompilerParams(
        dimension_semantics=("arbitrary",) * n_axes,
        vmem_limit_bytes=VMEM_LIMIT_BYTES,
    )


def _resident(shape):
    nd = len(shape)
    return pl.BlockSpec(shape, lambda *_: (0,) * nd, pipeline_mode=pl.Buffered(1))


def _dot(a, b):
    return jnp.dot(a, b, preferred_element_type=F32)


def _dot_nt(a, b):
    return lax.dot_general(a, b, (((1,), (1,)), ((), ())), preferred_element_type=F32)


def _rms(x):
    return x * lax.rsqrt(jnp.mean(x * x, axis=-1, keepdims=True) + EPS)


def _norm_mod(x, g, mod_ref, k0):
    shift = mod_ref[k0:k0 + 1, :]
    scale = mod_ref[k0 + 1:k0 + 2, :]
    return (_rms(x) * g) * (1.0 + scale) + shift


def _mod_kernel(c_ref, w_ref, b_ref, o_ref):
    c = c_ref[...]
    sc = c * jax.nn.sigmoid(c)
    o_ref[...] = jnp.dot(sc, w_ref[...], preferred_element_type=F32,
                         precision=lax.Precision.HIGHEST) + b_ref[...]


def _modulation(cvec, ada_w, ada_b):
    depth, d, n = ada_w.shape
    r = cvec.shape[0]
    tn = 1536
    assert n % tn == 0
    out = pl.pallas_call(
        _mod_kernel,
        grid=(depth, n // tn),
        in_specs=[
            pl.BlockSpec((r, d), lambda l, j: (0, 0)),
            pl.BlockSpec((None, d, tn), lambda l, j: (l, 0, j)),
            pl.BlockSpec((None, 1, tn), lambda l, j: (l, 0, j)),
        ],
        out_specs=pl.BlockSpec((None, r, tn), lambda l, j: (l, 0, j)),
        out_shape=jax.ShapeDtypeStruct((depth, r, n), F32),
        compiler_params=_params(2),
        name="adaln_modulation",
    )(cvec, ada_w, ada_b.reshape(depth, 1, n))
    return out.reshape(depth, r, N_MOD, d)


def _mod_spec(d, tm, rows_per_mod, mod_off):
    return pl.BlockSpec((None, N_MOD, d), lambda i: (mod_off + (i * tm) // rows_per_mod, 0, 0))


class _Section:
    def __init__(self, body, inputs=(), outputs=(), writes_x=False):
        self.body, self.inputs, self.outputs, self.writes_x = body, list(inputs), list(outputs), writes_x


def _stage_kernel(*refs, sections, writes_x):
    x_ref, mod_ref = refs[0], refs[1]
    n_in = sum(len(s.inputs) for s in sections)
    in_refs = refs[2:2 + n_in]
    out_refs = refs[2 + n_in:]
    x = x_ref[...]
    xo_ref = None
    if writes_x:
        xo_ref, out_refs = out_refs[0], out_refs[1:]
    i0 = o0 = 0
    for s in sections:
        x = s.body(x, mod_ref, in_refs[i0:i0 + len(s.inputs)], out_refs[o0:o0 + len(s.outputs)])
        i0 += len(s.inputs)
        o0 += len(s.outputs)
    if writes_x:
        xo_ref[...] = x


def _token_stage(x, mod, rows_per_mod, mod_off, sections, name, seq=None, tm=512):
    t, d = x.shape
    tm = min(tm, t)
    assert t % tm == 0 and rows_per_mod % tm == 0
    writes_x = any(s.writes_x for s in sections)
    in_specs = [pl.BlockSpec((tm, d), lambda i: (i, 0)), _mod_spec(d, tm, rows_per_mod, mod_off)]
    args = [x, mod]
    for s in sections:
        for arr, kind in s.inputs:
            if kind == "res":
                in_specs.append(_resident(arr.shape))
            elif kind == "row":
                in_specs.append(pl.BlockSpec((tm, arr.shape[1]), lambda i: (i, 0)))
            else:
                assert kind == "rope" and seq % tm == 0
                n_rt = seq // tm
                in_specs.append(pl.BlockSpec((tm, arr.shape[1]), lambda i: (i % n_rt, 0)))
            args.append(arr)
    out_specs, out_shape = [], []
    if writes_x:
        out_specs.append(pl.BlockSpec((tm, d), lambda i: (i, 0)))
        out_shape.append(jax.ShapeDtypeStruct((t, d), F32))
    for s in sections:
        for w, kind in s.outputs:
            if kind == "row":
                out_specs.append(pl.BlockSpec((tm, w), lambda i: (i, 0)))
                out_shape.append(jax.ShapeDtypeStruct((t, w), BF16))
            else:
                out_specs.append(pl.BlockSpec((w, tm), lambda i: (0, i)))
                out_shape.append(jax.ShapeDtypeStruct((w, t), BF16))
    return pl.pallas_call(
        functools.partial(_stage_kernel, sections=sections, writes_x=writes_x),
        grid=(t // tm,),
        in_specs=in_specs,
        out_specs=out_specs,
        out_shape=out_shape,
        compiler_params=_params(1),
        name=name,
    )(*args)


FFN_CHUNKS = 2


def _sec_half_ffn(k0, g, wg, wu, wd):
    def body(x, mod_ref, in_refs, out_refs):
        g_ref, wg_ref, wu_ref, wd_ref = in_refs
        hb = _norm_mod(x, g_ref[...], mod_ref, k0).astype(BF16)
        tf = wg_ref.shape[1] // FFN_CHUNKS
        y = None
        for j in range(FFN_CHUNKS):
            a = _dot(hb, wg_ref[:, j * tf:(j + 1) * tf])
            u = _dot(hb, wu_ref[:, j * tf:(j + 1) * tf])
            act = ((a * jax.nn.sigmoid(a)) * u).astype(BF16)
            yj = _dot(act, wd_ref[j * tf:(j + 1) * tf, :])
            y = yj if y is None else y + yj
        return x + (0.5 * mod_ref[k0 + 2:k0 + 3, :]) * y

    assert wg.shape[1] % (FFN_CHUNKS * LANES) == 0
    return _Section(body, [(g.reshape(1, -1), "res"), (wg, "res"), (wu, "res"), (wd, "res")], writes_x=True)


def _sec_outproj(w, ys):
    def body(x, mod_ref, in_refs, out_refs):
        w_ref, *y_refs = in_refs
        acc, r0 = None, 0
        for y_ref in y_refs:
            k = y_ref.shape[1]
            part = _dot(y_ref[...], w_ref[r0:r0 + k, :])
            acc = part if acc is None else acc + part
            r0 += k
        return x + mod_ref[5:6, :] * acc

    return _Section(body, [(w, "res")] + [(y, "row") for y in ys], writes_x=True)


def _sec_final_norm(g):
    def body(x, mod_ref, in_refs, out_refs):
        return _rms(x) * in_refs[0][...]

    return _Section(body, [(g.reshape(1, -1), "res")], writes_x=True)


def _proj_ab_body(x, mod_ref, in_refs, out_refs):
    g_ref, w_ref, wvt_ref, cc_ref, cs_ref = in_refs
    uc_ref, us_ref, q_ref, k_ref, vt_ref = out_refs
    hb = _norm_mod(x, g_ref[...], mod_ref, 3).astype(BF16)
    z = _dot(hb, w_ref[...])
    u = z[:, :FNET_DIM].astype(BF16)
    for g in range(FNET_GROUPS):
        sl = slice(g * FNET_GROUP_DIM, (g + 1) * FNET_GROUP_DIM)
        uc_ref[:, sl] = _dot(u[:, sl], cc_ref[...]).astype(BF16)
        us_ref[:, sl] = _dot(u[:, sl], cs_ref[...]).astype(BF16)
    q_ref[...] = (z[:, FNET_DIM:FNET_DIM + NA_DIM] * (NA_HEAD_DIM ** -0.5 * LOG2_E)).astype(BF16)
    k_ref[...] = z[:, FNET_DIM + NA_DIM:].astype(BF16)
    vt_ref[...] = _dot_nt(wvt_ref[...], hb).astype(BF16)
    return x


def _dft_real_imag(n, scale):
    j = np.arange(n, dtype=np.int64)
    ang = 2.0 * np.pi * ((j[:, None] * j[None, :]) % n).astype(np.float64) / n
    return (np.cos(ang) * scale).astype(np.float32), (np.sin(ang) * scale).astype(np.float32)


def _sec_proj_ab(g, w_in):
    cc, cs = _dft_real_imag(FNET_GROUP_DIM, FNET_GROUP_DIM ** -0.5)
    n_row = FNET_DIM + 2 * NA_DIM
    w_row = w_in[:, :n_row].astype(BF16)
    w_vt = w_in[:, n_row:].T.astype(BF16)
    inputs = [(g.reshape(1, -1), "res"), (w_row, "res"), (w_vt, "res"),
              (jnp.asarray(cc).astype(BF16), "res"), (jnp.asarray(cs).astype(BF16), "res")]
    outputs = [(FNET_DIM, "row"), (FNET_DIM, "row"), (NA_DIM, "row"), (NA_DIM, "row"), (NA_DIM, "col")]
    return _Section(_proj_ab_body, inputs, outputs)


def _fnet_kernel(cn_ref, sn_ref, uc_ref, us_ref, o_ref):
    o = _dot(cn_ref[...], uc_ref[...]) - _dot(sn_ref[...], us_ref[...])
    o_ref[...] = o.astype(BF16)


FNET_LANE_TILE = 8192


def _fnet_stage1_kernel(m1_ref, uc_ref, us_ref, twc_ref, tws_ref, ore_ref, oim_ref):
    r = uc_ref.shape[0]
    w = ore_ref.shape[-1]
    a = _dot(m1_ref[...], jnp.concatenate([uc_ref[...], us_ref[...]], axis=0))
    for cc in range(ore_ref.shape[0]):
        are, aim = a[:r, cc * w:(cc + 1) * w], a[r:, cc * w:(cc + 1) * w]
        cw = jnp.tile(twc_ref[cc], (1, w // LANES))
        sw = jnp.tile(tws_ref[cc], (1, w // LANES))
        ore_ref[cc] = (are * cw + aim * sw).astype(BF16)
        oim_ref[cc] = (aim * cw - are * sw).astype(BF16)


def _fnet_stage2_kernel(t2_ref, re_ref, im_ref, o_ref):
    o_ref[...] = _dot(t2_ref[...], jnp.concatenate([re_ref[...], im_ref[...]], axis=0)).astype(BF16)


def _fnet_positions_two_stage(uc, us, batch):
    t, w = uc.shape
    n = t // batch
    rows = n // GRID_W
    cols_per_step = FNET_LANE_TILE // w
    assert rows * GRID_W == n and GRID_W % cols_per_step == 0 and (rows * w) % FNET_LANE_TILE == 0
    cr, sr = _dft_real_imag(rows, rows ** -0.5)
    cg, sg = _dft_real_imag(GRID_W, GRID_W ** -0.5)
    m1 = jnp.asarray(np.block([[cr, -sr], [-sr, -cr]])).astype(BF16)
    t2 = jnp.asarray(np.concatenate([cg, sg], axis=1)).astype(BF16)
    ang = 2.0 * np.pi * (np.arange(GRID_W)[:, None] * np.arange(rows)[None, :]) / n
    twc = jnp.broadcast_to(jnp.asarray(np.cos(ang), F32)[:, :, None], (GRID_W, rows, LANES))
    tws = jnp.broadcast_to(jnp.asarray(np.sin(ang), F32)[:, :, None], (GRID_W, rows, LANES))
    n_ct = GRID_W // cols_per_step
    in_view = (batch * rows, GRID_W * w)
    mid = jax.ShapeDtypeStruct((batch, GRID_W, rows, w), BF16)
    are, aim = pl.pallas_call(
        _fnet_stage1_kernel,
        grid=(batch, n_ct),
        in_specs=[
            _resident(m1.shape),
            pl.BlockSpec((rows, FNET_LANE_TILE), lambda b, j: (b, j)),
            pl.BlockSpec((rows, FNET_LANE_TILE), lambda b, j: (b, j)),
            pl.BlockSpec((cols_per_step, rows, LANES), lambda b, j: (j, 0, 0)),
            pl.BlockSpec((cols_per_step, rows, LANES), lambda b, j: (j, 0, 0)),
        ],
        out_specs=[pl.BlockSpec((None, cols_per_step, rows, w), lambda b, j: (b, j, 0, 0))] * 2,
        out_shape=[mid, mid],
        compiler_params=_params(2),
        name="fnet_rows_dft",
    )(m1, uc.reshape(in_view), us.reshape(in_view), twc, tws)
    mid_view = (batch * GRID_W, rows * w)
    n_lt = rows * w // FNET_LANE_TILE
    out = pl.pallas_call(
        _fnet_stage2_kernel,
        grid=(batch, n_lt),
        in_specs=[
            _resident(t2.shape),
            pl.BlockSpec((GRID_W, FNET_LANE_TILE), lambda b, j: (b, j)),
            pl.BlockSpec((GRID_W, FNET_LANE_TILE), lambda b, j: (b, j)),
        ],
        out_specs=pl.BlockSpec((GRID_W, FNET_LANE_TILE), lambda b, j: (b, j)),
        out_shape=jax.ShapeDtypeStruct(mid_view, BF16),
        compiler_params=_params(2),
        name="fnet_cols_dft",
    )(t2, are.reshape(mid_view), aim.reshape(mid_view))
    return out.reshape(t, w)


def _fnet_positions(uc, us, batch):
    t, w = uc.shape
    n = t // batch
    if n > 512:
        return _fnet_positions_two_stage(uc, us, batch)
    c, s = _dft_real_imag(n, n ** -0.5)
    cn, sn = jnp.asarray(c).astype(BF16), jnp.asarray(s).astype(BF16)
    tn = n
    nt = 1
    return pl.pallas_call(
        _fnet_kernel,
        grid=(nt, batch),
        in_specs=[
            pl.BlockSpec((tn, n), lambda i, b: (i, 0)),
            pl.BlockSpec((tn, n), lambda i, b: (i, 0)),
            pl.BlockSpec((n, w), lambda i, b: (b, 0)),
            pl.BlockSpec((n, w), lambda i, b: (b, 0)),
        ],
        out_specs=pl.BlockSpec((tn, w), lambda i, b: (b * nt + i, 0)),
        out_shape=jax.ShapeDtypeStruct((t, w), BF16),
        compiler_params=_params(2),
        name="fnet_positions",
    )(cn, sn, uc, us)


def _na_geometry(rows):
    n_blocks = rows // NA_ROWS_PER_BLOCK
    kh = min(NA_KH, rows)
    variants, vi, start_blk = [], [], []
    for blk in range(n_blocks):
        start = int(np.clip(NA_ROWS_PER_BLOCK * blk - kh // 2, 0, rows - NA_WIN_ROWS))
        table = []
        for a in range(NA_ROWS_PER_BLOCK):
            r = NA_ROWS_PER_BLOCK * blk + a
            r0 = int(np.clip(r - kh // 2, 0, rows - kh))
            table.append(tuple((start + i) - r + (NA_KH - 1) if 0 <= (start + i) - r0 < kh else -1
                               for i in range(NA_WIN_ROWS)))
            assert sum(e >= 0 for e in table[-1]) == kh
        table = tuple(table)
        if table not in variants:
            variants.append(table)
        vi.append(variants.index(table))
        start_blk.append(start * GRID_W // NA_KBLK)
    return vi, start_blk, variants


def _na_bias_kernel(rpb_ref, o_ref, *, variants):
    v = pl.program_id(0)
    h = pl.program_id(1)
    n_ro = 2 * NA_KH - 1
    n_co = 2 * NA_KW - 1
    kc = lax.broadcasted_iota(jnp.int32, (GRID_W, LANES), 0)
    lane = lax.broadcasted_iota(jnp.int32, (GRID_W, LANES), 1)
    c = lane & (GRID_W - 1)
    co = kc - c + (NA_KW - 1)
    c0 = jnp.clip(c - NA_KW // 2, 0, GRID_W - NA_KW)
    win = (kc >= c0) & (kc < c0 + NA_KW)
    neg = jnp.full((GRID_W, LANES), NEG_BIG, F32)
    tiles = []
    for ro in range(n_ro):
        t = neg
        for tt in range(n_co):
            t = jnp.where(win & (co == tt), rpb_ref[(h * n_ro + ro) * n_co + tt] * LOG2_E, t)
        tiles.append(t)
    left = lane < GRID_W
    for vv, table in enumerate(variants):
        @pl.when(v == vv)
        def _(table=table):
            for i in range(NA_WIN_ROWS):
                for j in range(NA_ROWS_PER_BLOCK // 2):
                    r1, r2 = table[2 * j][i], table[2 * j + 1][i]
                    t1 = tiles[r1] if r1 >= 0 else neg
                    t2 = tiles[r2] if r2 >= 0 else neg
                    o_ref[i * GRID_W:(i + 1) * GRID_W, j * LANES:(j + 1) * LANES] = jnp.where(left, t1, t2)


def _na_bias_tables(rpb, variants):
    heads = rpb.shape[0]
    nq = NA_ROWS_PER_BLOCK * GRID_W
    nk = NA_WIN_ROWS * GRID_W
    return pl.pallas_call(
        functools.partial(_na_bias_kernel, variants=variants),
        grid=(len(variants), heads),
        in_specs=[pl.BlockSpec(memory_space=pltpu.SMEM)],
        out_specs=pl.BlockSpec((None, None, nk, nq), lambda v, h: (v, h, 0, 0)),
        out_shape=jax.ShapeDtypeStruct((len(variants), heads, nk, nq), F32),
        compiler_params=_params(2),
        name="na_bias_tables",
    )(rpb.reshape(-1))


def _pair_masks():
    lane = lax.broadcasted_iota(jnp.int32, (1, LANES), 1)
    return [(lane // NA_HEAD_DIM == hh).astype(BF16) for hh in range(LANES // NA_HEAD_DIM)]


def _na_kernel(vi_ref, sb_ref, q_ref, k0_ref, k1_ref, k2_ref, k3_ref, v0_ref, v1_ref, v2_ref, v3_ref,
               kc_ref, vc_ref, bias_ref, o_ref):
    del vi_ref, sb_ref
    k_refs = (k0_ref, k1_ref, k2_ref, k3_ref, kc_ref)
    vt_refs = (v0_ref, v1_ref, v2_ref, v3_ref, vc_ref)
    q = q_ref[...]
    masks = _pair_masks()

    def scores(hh):
        qh = q * masks[hh]
        s = []
        for j, k_ref in enumerate(k_refs):
            sj = _dot_nt(k_ref[...], qh)
            if j < 4:
                sj = sj + bias_ref[hh, j * NA_KBLK:(j + 1) * NA_KBLK, :]
            s.append(sj)
        return s

    outs = []
    s = scores(0)
    for hh in range(len(masks)):
        s_next = scores(hh + 1) if hh + 1 < len(masks) else None
        m = functools.reduce(jnp.maximum, [jnp.max(sj, axis=0, keepdims=True) for sj in s])
        acc, l = None, None
        for sj, vt_ref in zip(s, vt_refs):
            p = jnp.exp2(sj - m)
            lj = jnp.sum(p, axis=0, keepdims=True)
            pv = _dot(vt_ref[hh * NA_HEAD_DIM:(hh + 1) * NA_HEAD_DIM, :], p.astype(BF16))
            acc = pv if acc is None else acc + pv
            l = lj if l is None else l + lj
        outs.append(acc / l)
        s = s_next
    o_ref[...] = jnp.transpose(jnp.concatenate(outs, axis=0)).astype(BF16)


def _neighbourhood_attention(q, k, vt, kc, vct, bias, vi, start_blk, batch):
    t, w = q.shape
    s = t // batch
    rows = s // GRID_W
    nq = NA_ROWS_PER_BLOCK * GRID_W
    n_blocks = rows // NA_ROWS_PER_BLOCK
    n_pairs = w // LANES
    lc = kc.shape[0] // batch
    kblk_per_batch = s // NA_KBLK
    heads_per_pair = LANES // NA_HEAD_DIM
    bias5 = bias.reshape(bias.shape[0], n_pairs, heads_per_pair, NA_WIN_ROWS * GRID_W, nq)

    def k_spec(j):
        return pl.BlockSpec((NA_KBLK, LANES),
                            lambda hp, i, b, vi_r, sb_r: (b * kblk_per_batch + sb_r[i] + j, hp))

    def vt_spec(j):
        return pl.BlockSpec((LANES, NA_KBLK),
                            lambda hp, i, b, vi_r, sb_r: (hp, b * kblk_per_batch + sb_r[i] + j))

    grid_spec = pltpu.PrefetchScalarGridSpec(
        num_scalar_prefetch=2,
        grid=(n_pairs, n_blocks, batch),
        in_specs=[pl.BlockSpec((nq, LANES), lambda hp, i, b, vi_r, sb_r: (b * n_blocks + i, hp))]
        + [k_spec(j) for j in range(4)] + [vt_spec(j) for j in range(4)]
        + [pl.BlockSpec((lc, LANES), lambda hp, i, b, vi_r, sb_r: (b, hp)),
           pl.BlockSpec((LANES, lc), lambda hp, i, b, vi_r, sb_r: (hp, b)),
           pl.BlockSpec((None, None, heads_per_pair, NA_WIN_ROWS * GRID_W, nq),
                        lambda hp, i, b, vi_r, sb_r: (vi_r[i], hp, 0, 0, 0))],
        out_specs=pl.BlockSpec((nq, LANES), lambda hp, i, b, vi_r, sb_r: (b * n_blocks + i, hp)),
    )
    return pl.pallas_call(
        _na_kernel,
        grid_spec=grid_spec,
        out_shape=jax.ShapeDtypeStruct((t, w), BF16),
        compiler_params=_params(3),
        name="neighbourhood_attention",
    )(jnp.asarray(vi, jnp.int32), jnp.asarray(start_blk, jnp.int32),
      q, k, k, k, k, vt, vt, vt, vt, kc, vct, bias5)


def _ctx_attn_kernel(q_ref, k_ref, vt_ref, o_ref):
    q = q_ref[...]
    k = k_ref[...]
    outs = []
    for hh, hm in enumerate(_pair_masks()):
        st = _dot_nt(k, q * hm)
        m = jnp.max(st, axis=0, keepdims=True)
        p = jnp.exp2(st - m)
        l = jnp.sum(p, axis=0, keepdims=True)
        outs.append(_dot(vt_ref[hh * NA_HEAD_DIM:(hh + 1) * NA_HEAD_DIM, :], p.astype(BF16)) / l)
    o_ref[...] = jnp.transpose(jnp.concatenate(outs, axis=0)).astype(BF16)


def _ctx_dense_attention(q, k, vt, batch):
    t, w = q.shape
    lc = t // batch
    spec = pl.BlockSpec((lc, LANES), lambda b, hp: (b, hp))
    return pl.pallas_call(
        _ctx_attn_kernel,
        grid=(batch, w // LANES),
        in_specs=[spec, spec, pl.BlockSpec((LANES, lc), lambda b, hp: (hp, b))],
        out_specs=spec,
        out_shape=jax.ShapeDtypeStruct((t, w), BF16),
        compiler_params=_params(2),
        name="ctx_dense_attention",
    )(q, k, vt)


MLA_QK_PAD = 256


def _rope(x, cos, sin_next, sin_prev):
    quarter = MLA_ROPE // 4
    return (x * cos + pltpu.roll(x, LANES - quarter, 1) * sin_next + pltpu.roll(x, quarter, 1) * sin_prev)


def _proj_mla_body(x, mod_ref, in_refs, out_refs, *, rope, need_q):
    g_ref, win_ref, gq_ref, gkv_ref, wqn_ref, wqr_ref, wuk_ref, wuv_ref, *rope_refs = in_refs
    if rope:
        cos_ref, sa_ref, sb_ref = rope_refs
    if need_q:
        q_ref, k_ref, v_ref = out_refs
    else:
        k_ref, v_ref = out_refs
    hb = _norm_mod(x, g_ref[...], mod_ref, 3).astype(BF16)
    z = _dot(hb, win_ref[...])
    ckv = (_rms(z[:, MLA_Q_RANK:MLA_Q_RANK + MLA_KV_RANK]) * gkv_ref[...]).astype(BF16)
    kr = z[:, MLA_Q_RANK + MLA_KV_RANK:]
    if rope:
        cos, sa, sb = cos_ref[...], sa_ref[...], sb_ref[...]
        kr = _rope(kr, cos, sa, sb)
    kr = kr.astype(BF16)
    kn = _dot(ckv, wuk_ref[...])
    v_ref[...] = _dot_nt(wuv_ref[...], ckv).astype(BF16)
    for h in range(MLA_HEADS):
        k_ref[:, h * MLA_QK_PAD:h * MLA_QK_PAD + LANES] = kn[:, h * LANES:(h + 1) * LANES].astype(BF16)
        k_ref[:, h * MLA_QK_PAD + LANES:(h + 1) * MLA_QK_PAD] = kr
    if need_q:
        cq = (_rms(z[:, :MLA_Q_RANK]) * gq_ref[...]).astype(BF16)
        scale = (MLA_NOPE + MLA_ROPE) ** -0.5 * LOG2_E
        qn = _dot(cq, wqn_ref[...])
        qr = _dot(cq, wqr_ref[...])
        for h in range(MLA_HEADS):
            qrh = qr[:, h * LANES:(h + 1) * LANES]
            if rope:
                qrh = _rope(qrh, cos, sa, sb)
            q_ref[:, h * MLA_QK_PAD:h * MLA_QK_PAD + LANES] = (qn[:, h * LANES:(h + 1) * LANES] * scale).astype(BF16)
            q_ref[:, h * MLA_QK_PAD + LANES:(h + 1) * MLA_QK_PAD] = (qrh * scale).astype(BF16)
    return x


def _rope_tables(n_tok):
    pos = jnp.arange(n_tok)
    row = (pos // GRID_W).astype(F32)
    col = (pos % GRID_W).astype(F32)
    half = MLA_ROPE // 2
    inv = ROPE_BASE ** (-jnp.arange(0, half, 2, dtype=F32) / half)
    ang_r = row[:, None] * inv[None]
    ang_c = col[:, None] * inv[None]
    ang = jnp.concatenate([ang_r, ang_r, ang_c, ang_c], axis=-1)
    cos, sin = jnp.cos(ang), jnp.sin(ang)
    quarter = MLA_ROPE // 4
    even_q = ((jnp.arange(MLA_ROPE) // quarter) % 2 == 0)[None, :]
    pad = LANES - MLA_ROPE
    cos_p = jnp.pad(cos, ((0, 0), (0, pad)), constant_values=1.0)
    sin_next = jnp.pad(jnp.where(even_q, -sin, 0.0), ((0, 0), (0, pad)))
    sin_prev = jnp.pad(jnp.where(even_q, 0.0, sin), ((0, 0), (0, pad)))
    return cos_p, sin_next, sin_prev


def _sec_proj_mla(g, w, rope_tabs, need_q):
    rope = rope_tabs is not None
    inputs = [(g.reshape(1, -1), "res")] + [(a, "res") for a in w]
    if rope:
        inputs += [(tab, "rope") for tab in rope_tabs]
    qk_w = MLA_HEADS * MLA_QK_PAD
    outputs = ([(qk_w, "row")] if need_q else []) + [(qk_w, "row"), (MLA_HEADS * MLA_V, "col")]
    return _Section(functools.partial(_proj_mla_body, rope=rope, need_q=need_q), inputs, outputs)


def _mla_attn_kernel(q_ref, *rest, n_src, chunk):
    k_refs, vt_refs, o_ref = rest[:n_src], rest[n_src:2 * n_src], rest[2 * n_src]
    q = q_ref[...]
    chunks = [(k_ref, vt_ref, c0, min(k_ref.shape[0], c0 + chunk))
              for k_ref, vt_ref in zip(k_refs, vt_refs) for c0 in range(0, k_ref.shape[0], chunk)]

    def scores(j):
        k_ref, _, c0, c1 = chunks[j]
        return _dot_nt(k_ref[c0:c1, :], q)

    m = l = acc = None
    st = scores(0)
    for j, (_, vt_ref, c0, c1) in enumerate(chunks):
        st_next = scores(j + 1) if j + 1 < len(chunks) else None
        mj = jnp.max(st, axis=0, keepdims=True)
        if m is None:
            m_new = mj
        else:
            m_new = jnp.maximum(m, mj)
            alpha = jnp.exp2(m - m_new)
        p = jnp.exp2(st - m_new)
        lj = jnp.sum(p, axis=0, keepdims=True)
        oj = _dot(vt_ref[:, c0:c1], p.astype(BF16))
        if m is None:
            l, acc = lj, oj
        else:
            l, acc = alpha * l + lj, alpha * acc + oj
        m, st = m_new, st_next
    o_ref[...] = jnp.transpose(acc / l).astype(BF16)


def _mla_attention(q, sources, batch, tq=1024, chunk=1024):
    t = q.shape[0]
    s = t // batch
    tq = min(tq, s)
    nq = s // tq
    k_specs, v_specs = [], []
    for k, _ in sources:
        n = k.shape[0] // batch
        k_specs.append(pl.BlockSpec((n, MLA_QK_PAD), lambda b, h, i: (b, h)))
        v_specs.append(pl.BlockSpec((MLA_V, n), lambda b, h, i: (h, b)))
    return pl.pallas_call(
        functools.partial(_mla_attn_kernel, n_src=len(sources), chunk=chunk),
        grid=(batch, MLA_HEADS, nq),
        in_specs=[pl.BlockSpec((tq, MLA_QK_PAD), lambda b, h, i: (b * nq + i, h))] + k_specs + v_specs,
        out_specs=pl.BlockSpec((tq, MLA_V), lambda b, h, i: (b * nq + i, h)),
        out_shape=jax.ShapeDtypeStruct((t, MLA_HEADS * MLA_V), BF16),
        compiler_params=_params(3),
        name="mla_attention",
    )(q, *[k for k, _ in sources], *[v for _, v in sources])


def _mla_weights(w_in, g_q, g_kv, w_uq, w_uk, w_uv):
    w_in_p = jnp.pad(w_in, ((0, 0), (0, 5 * LANES - w_in.shape[1]))).astype(BF16)
    wq = w_uq.reshape(MLA_Q_RANK, MLA_HEADS, MLA_NOPE + MLA_ROPE)
    wqn = wq[:, :, :MLA_NOPE].reshape(MLA_Q_RANK, MLA_HEADS * MLA_NOPE).astype(BF16)
    wqr = jnp.pad(wq[:, :, MLA_NOPE:], ((0, 0), (0, 0), (0, LANES - MLA_ROPE)))
    wqr = wqr.reshape(MLA_Q_RANK, MLA_HEADS * LANES).astype(BF16)
    return (w_in_p, g_q.reshape(1, -1), g_kv.reshape(1, -1), wqn, wqr, w_uk.astype(BF16), w_uv.T.astype(BF16))


def kernel(x, c, ctx, c_ctx, ada_w, ada_b, norm_g, ffn_w_gate, ffn_w_up, ffn_w_down, ab_w_in, ab_rpb, ab_w_out,
           mla_w_in, mla_g_q, mla_g_kv, mla_w_uq, mla_w_uk, mla_w_uv, mla_w_o, final_g):
    batch, seq, d = x.shape
    lc = ctx.shape[1]
    depth = ada_w.shape[0]
    assert seq % (GRID_W * NA_WIN_ROWS) == 0 and lc % LANES == 0
    t_lat, t_ctx = batch * seq, batch * lc

    n_rows = -(-(batch + 1) // 8) * 8
    cvec = jnp.concatenate([c, c_ctx[None, :], jnp.zeros((n_rows - batch - 1, d), F32)], axis=0)
    mod = _modulation(cvec, ada_w, ada_b)

    x_l = x.reshape(t_lat, d)
    x_c = ctx.reshape(t_ctx, d)
    lat = dict(rows_per_mod=seq, mod_off=0)
    cx = dict(rows_per_mod=t_ctx, mod_off=batch)

    wg, wu, wd = ffn_w_gate.astype(BF16), ffn_w_up.astype(BF16), ffn_w_down.astype(BF16)
    rope_tabs = None

    for layer in range(depth):
        last = layer == depth - 1
        m = mod[layer]
        g = norm_g[layer]
        ffn1 = _sec_half_ffn(0, g[0], wg[layer, 0], wu[layer, 0], wd[layer, 0])
        ffn2 = _sec_half_ffn(6, g[2], wg[layer, 1], wu[layer, 1], wd[layer, 1])
        i = layer // 2
        if layer % 2 == 0:
            proj = _sec_proj_ab(g[1], ab_w_in[i])
            w_out = ab_w_out[i].astype(BF16)
            x_l, uc_l, us_l, q_l, k_l, vt_l = _token_stage(x_l, m, sections=[ffn1, proj], name="ffn_proj_ab", **lat)
            x_c, uc_c, us_c, q_c, k_c, vt_c = _token_stage(x_c, m, sections=[ffn1, proj], name="ffn_proj_ab", **cx)
            a_l = _fnet_positions(uc_l, us_l, batch)
            vi, start_blk, variants = _na_geometry(seq // GRID_W)
            bias = _na_bias_tables(ab_rpb[i], variants)
            b_l = _neighbourhood_attention(q_l, k_l, vt_l, k_c, vt_c, bias, vi, start_blk, batch)
            ys_l = [a_l, b_l]
            if not last:
                a_c = _fnet_positions(uc_c, us_c, batch)
                b_c = _ctx_dense_attention(q_c, k_c, vt_c, batch)
                ys_c = [a_c, b_c]
        else:
            if rope_tabs is None:
                rope_tabs = _rope_tables(seq)
            w = _mla_weights(mla_w_in[i], mla_g_q[i], mla_g_kv[i], mla_w_uq[i], mla_w_uk[i], mla_w_uv[i])
            w_out = mla_w_o[i].astype(BF16)
            x_l, q_l, k_l, vt_l = _token_stage(
                x_l, m, sections=[ffn1, _sec_proj_mla(g[1], w, rope_tabs, True)], name="ffn_proj_mla", seq=seq, **lat)
            x_c, *q_c, k_c, vt_c = _token_stage(
                x_c, m, sections=[ffn1, _sec_proj_mla(g[1], w, None, not last)], name="ffn_proj_mla", **cx)
            ys_l = [_mla_attention(q_l, [(k_l, vt_l), (k_c, vt_c)], batch)]
            if not last:
                ys_c = [_mla_attention(q_c[0], [(k_c, vt_c)], batch)]
        tail = [_sec_final_norm(final_g)] if last else []
        (x_l,) = _token_stage(x_l, m, sections=[_sec_outproj(w_out, ys_l), ffn2] + tail, name="outproj_ffn", **lat)
        if not last:
            (x_c,) = _token_stage(x_c, m, sections=[_sec_outproj(w_out, ys_c), ffn2], name="outproj_ffn", **cx)
    return x_l.reshape(batch, seq, d)
```

```python
import functools
import math

import numpy as np
import jax
import jax.numpy as jnp
from jax import lax
from jax.experimental import pallas as pl
from jax.experimental.pallas import tpu as pltpu

F32 = jnp.float32
BF16 = jnp.bfloat16

GRID_W = 64
EPS = 1e-6
N_MOD = 9
FNET_GROUPS = 4
FNET_GROUP_DIM = 128
FNET_DIM = FNET_GROUPS * FNET_GROUP_DIM
NA_HEADS = 8
NA_HEAD_DIM = 64
NA_DIM = NA_HEADS * NA_HEAD_DIM
NA_KH = 8
NA_KW = 16
MLA_HEADS = 8
MLA_NOPE = 128
MLA_ROPE = 64
MLA_V = 128
MLA_Q_RANK = 384
MLA_KV_RANK = 128
ROPE_BASE = 10000.0

LANES = 128
VMEM_LIMIT_BYTES = 56 * 1024 * 1024
NEG_BIG = -1e30
LOG2_E = math.log2(math.e)

NA_ROWS_PER_BLOCK = 8
NA_WIN_ROWS = 16
NA_KBLK = 256
NA_PAIR_ROWS = NA_KH + 1


def _params(n_axes):
    return pltpu.CompilerParams(
        dimension_semantics=("arbitrary",) * n_axes,
        vmem_limit_bytes=VMEM_LIMIT_BYTES,
    )


def _resident(shape):
    nd = len(shape)
    return pl.BlockSpec(shape, lambda *_: (0,) * nd, pipeline_mode=pl.Buffered(1))


def _dot(a, b):
    return jnp.dot(a, b, preferred_element_type=F32)


def _dot_nt(a, b):
    return lax.dot_general(a, b, (((1,), (1,)), ((), ())), preferred_element_type=F32)


def _rms(x):
    return x * lax.rsqrt(jnp.mean(x * x, axis=-1, keepdims=True) + EPS)


def _norm_mod(x, g, mod_ref, k0):
    shift = mod_ref[k0:k0 + 1, :]
    scale = mod_ref[k0 + 1:k0 + 2, :]
    return (_rms(x) * g) * (1.0 + scale) + shift


def _mod_kernel(c_ref, w_ref, b_ref, o_ref):
    c = c_ref[...]
    sc = c * jax.nn.sigmoid(c)
    o_ref[...] = jnp.dot(sc, w_ref[...], preferred_element_type=F32,
                         precision=lax.Precision.HIGHEST) + b_ref[...]


def _modulation(cvec, ada_w, ada_b):
    depth, d, n = ada_w.shape
    r = cvec.shape[0]
    tn = 1536
    assert n % tn == 0
    out = pl.pallas_call(
        _mod_kernel,
        grid=(depth, n // tn),
        in_specs=[
            pl.BlockSpec((r, d), lambda l, j: (0, 0)),
            pl.BlockSpec((None, d, tn), lambda l, j: (l, 0, j)),
            pl.BlockSpec((None, 1, tn), lambda l, j: (l, 0, j)),
        ],
        out_specs=pl.BlockSpec((None, r, tn), lambda l, j: (l, 0, j)),
        out_shape=jax.ShapeDtypeStruct((depth, r, n), F32),
        compiler_params=_params(2),
        name="adaln_modulation",
    )(cvec, ada_w, ada_b.reshape(depth, 1, n))
    return out.reshape(depth, r, N_MOD, d)


def _mod_spec(d, tm, rows_per_mod, mod_off):
    return pl.BlockSpec((None, N_MOD, d), lambda i: (mod_off + (i * tm) // rows_per_mod, 0, 0))


class _Section:
    def __init__(self, body, inputs=(), outputs=(), writes_x=False):
        self.body, self.inputs, self.outputs, self.writes_x = body, list(inputs), list(outputs), writes_x


def _stage_kernel(*refs, sections, writes_x):
    x_ref, mod_ref = refs[0], refs[1]
    n_in = sum(len(s.inputs) for s in sections)
    in_refs = refs[2:2 + n_in]
    out_refs = refs[2 + n_in:]
    x = x_ref[...]
    xo_ref = None
    if writes_x:
        xo_ref, out_refs = out_refs[0], out_refs[1:]
    i0 = o0 = 0
    for s in sections:
        x = s.body(x, mod_ref, in_refs[i0:i0 + len(s.inputs)], out_refs[o0:o0 + len(s.outputs)])
        i0 += len(s.inputs)
        o0 += len(s.outputs)
    if writes_x:
        xo_ref[...] = x


def _token_stage(x, mod, rows_per_mod, mod_off, sections, name, seq=None, tm=512):
    t, d = x.shape
    tm = min(tm, t)
    assert t % tm == 0 and rows_per_mod % tm == 0
    writes_x = any(s.writes_x for s in sections)
    in_specs = [pl.BlockSpec((tm, d), lambda i: (i, 0)), _mod_spec(d, tm, rows_per_mod, mod_off)]
    args = [x, mod]
    for s in sections:
        for arr, kind in s.inputs:
            if kind == "res":
                in_specs.append(_resident(arr.shape))
            elif kind == "row":
                in_specs.append(pl.BlockSpec((tm, arr.shape[1]), lambda i: (i, 0)))
            elif kind == "head":
                in_specs.append(pl.BlockSpec((arr.shape[0], tm, arr.shape[2]), lambda i: (0, i, 0)))
            else:
                assert kind == "rope" and seq % tm == 0
                n_rt = seq // tm
                in_specs.append(pl.BlockSpec((tm, arr.shape[1]), lambda i: (i % n_rt, 0)))
            args.append(arr)
    out_specs, out_shape = [], []
    if writes_x:
        out_specs.append(pl.BlockSpec((tm, d), lambda i: (i, 0)))
        out_shape.append(jax.ShapeDtypeStruct((t, d), F32))
    for s in sections:
        for w, kind in s.outputs:
            if kind == "row":
                out_specs.append(pl.BlockSpec((tm, w), lambda i: (i, 0)))
                out_shape.append(jax.ShapeDtypeStruct((t, w), BF16))
            elif kind == "head":
                heads, hw = w
                out_specs.append(pl.BlockSpec((heads, tm, hw), lambda i: (0, i, 0)))
                out_shape.append(jax.ShapeDtypeStruct((heads, t, hw), BF16))
            else:
                out_specs.append(pl.BlockSpec((w, tm), lambda i: (0, i)))
                out_shape.append(jax.ShapeDtypeStruct((w, t), BF16))
    return pl.pallas_call(
        functools.partial(_stage_kernel, sections=sections, writes_x=writes_x),
        grid=(t // tm,),
        in_specs=in_specs,
        out_specs=out_specs,
        out_shape=out_shape,
        compiler_params=_params(1),
        name=name,
    )(*args)


FFN_CHUNKS = 2


def _sec_half_ffn(k0, g, wg, wu, wd):
    def body(x, mod_ref, in_refs, out_refs):
        g_ref, wg_ref, wu_ref, wd_ref = in_refs
        hb = _norm_mod(x, g_ref[...], mod_ref, k0).astype(BF16)
        tf = wg_ref.shape[1] // FFN_CHUNKS
        y = None
        for j in range(FFN_CHUNKS):
            a = _dot(hb, wg_ref[:, j * tf:(j + 1) * tf])
            u = _dot(hb, wu_ref[:, j * tf:(j + 1) * tf])
            act = ((a * jax.nn.sigmoid(a)) * u).astype(BF16)
            yj = _dot(act, wd_ref[j * tf:(j + 1) * tf, :])
            y = yj if y is None else y + yj
        return x + (0.5 * mod_ref[k0 + 2:k0 + 3, :]) * y

    assert wg.shape[1] % (FFN_CHUNKS * LANES) == 0
    return _Section(body, [(g.reshape(1, -1), "res"), (wg, "res"), (wu, "res"), (wd, "res")], writes_x=True)


def _sec_outproj(w, ys):
    def body(x, mod_ref, in_refs, out_refs):
        w_ref, *y_refs = in_refs
        acc, r0 = None, 0
        for y_ref in y_refs:
            if len(y_ref.shape) == 3:
                y = jnp.concatenate([y_ref[h] for h in range(y_ref.shape[0])], axis=1)
            else:
                y = y_ref[...]
            k = y.shape[1]
            part = _dot(y, w_ref[r0:r0 + k, :])
            acc = part if acc is None else acc + part
            r0 += k
        return x + mod_ref[5:6, :] * acc

    return _Section(body, [(w, "res")] + [(y, "head" if y.ndim == 3 else "row") for y in ys], writes_x=True)


def _sec_final_norm(g):
    def body(x, mod_ref, in_refs, out_refs):
        return _rms(x) * in_refs[0][...]

    return _Section(body, [(g.reshape(1, -1), "res")], writes_x=True)


def _proj_ab_body(x, mod_ref, in_refs, out_refs):
    g_ref, w_ref, wvt_ref, cc_ref, cs_ref = in_refs
    uc_ref, us_ref, q_ref, k_ref, vt_ref = out_refs
    hb = _norm_mod(x, g_ref[...], mod_ref, 3).astype(BF16)
    z = _dot(hb, w_ref[...])
    u = z[:, :FNET_DIM].astype(BF16)
    for g in range(FNET_GROUPS):
        sl = slice(g * FNET_GROUP_DIM, (g + 1) * FNET_GROUP_DIM)
        uc_ref[:, sl] = _dot(u[:, sl], cc_ref[...]).astype(BF16)
        us_ref[:, sl] = _dot(u[:, sl], cs_ref[...]).astype(BF16)
    for hp in range(NA_DIM // LANES):
        q0, k0 = FNET_DIM + hp * LANES, FNET_DIM + NA_DIM + hp * LANES
        q_ref[hp] = (z[:, q0:q0 + LANES] * (NA_HEAD_DIM ** -0.5 * LOG2_E)).astype(BF16)
        k_ref[hp] = z[:, k0:k0 + LANES].astype(BF16)
    vt_ref[...] = _dot_nt(wvt_ref[...], hb).astype(BF16)
    return x


def _dft_real_imag(n, scale):
    j = np.arange(n, dtype=np.int64)
    ang = 2.0 * np.pi * ((j[:, None] * j[None, :]) % n).astype(np.float64) / n
    return (np.cos(ang) * scale).astype(np.float32), (np.sin(ang) * scale).astype(np.float32)


def _sec_proj_ab(g, w_in):
    cc, cs = _dft_real_imag(FNET_GROUP_DIM, FNET_GROUP_DIM ** -0.5)
    n_row = FNET_DIM + 2 * NA_DIM
    w_row = w_in[:, :n_row].astype(BF16)
    w_vt = w_in[:, n_row:].T.astype(BF16)
    inputs = [(g.reshape(1, -1), "res"), (w_row, "res"), (w_vt, "res"),
              (jnp.asarray(cc).astype(BF16), "res"), (jnp.asarray(cs).astype(BF16), "res")]
    pairs = ((NA_DIM // LANES, LANES), "head")
    outputs = [(FNET_DIM, "row"), (FNET_DIM, "row"), pairs, pairs, (NA_DIM, "col")]
    return _Section(_proj_ab_body, inputs, outputs)


def _fnet_kernel(cn_ref, sn_ref, uc_ref, us_ref, o_ref):
    o = _dot(cn_ref[...], uc_ref[...]) - _dot(sn_ref[...], us_ref[...])
    o_ref[...] = o.astype(BF16)


def _position_dft_matrices(n):
    if n <= 512:
        c, s = _dft_real_imag(n, n ** -0.5)
        return jnp.asarray(c).astype(BF16), jnp.asarray(s).astype(BF16)
    n1 = GRID_W
    n2 = n // n1
    k = np.arange(n, dtype=np.int64)
    hi = np.arange(n2, dtype=np.int64)[:, None] * n1
    lo = np.arange(n1, dtype=np.int64)[:, None]
    ang_hi = 2.0 * np.pi * ((hi * k[None, :]) % n).astype(np.float64) / n
    ang_lo = 2.0 * np.pi * ((lo * k[None, :]) % n).astype(np.float64) / n
    ch, sh = jnp.asarray(np.cos(ang_hi), F32)[:, None, :], jnp.asarray(np.sin(ang_hi), F32)[:, None, :]
    cl, sl = jnp.asarray(np.cos(ang_lo), F32)[None, :, :], jnp.asarray(np.sin(ang_lo), F32)[None, :, :]
    scale = n ** -0.5
    c = ((ch * cl - sh * sl) * scale).reshape(n, n).astype(BF16)
    s = ((sh * cl + ch * sl) * scale).reshape(n, n).astype(BF16)
    return c, s


def _fnet_positions(uc, us, batch):
    t, w = uc.shape
    n = t // batch
    cn, sn = _position_dft_matrices(n)
    tn = min(512, n)
    nt = n // tn
    return pl.pallas_call(
        _fnet_kernel,
        grid=(nt, batch),
        in_specs=[
            pl.BlockSpec((tn, n), lambda i, b: (i, 0)),
            pl.BlockSpec((tn, n), lambda i, b: (i, 0)),
            pl.BlockSpec((n, w), lambda i, b: (b, 0)),
            pl.BlockSpec((n, w), lambda i, b: (b, 0)),
        ],
        out_specs=pl.BlockSpec((tn, w), lambda i, b: (b * nt + i, 0)),
        out_shape=jax.ShapeDtypeStruct((t, w), BF16),
        compiler_params=_params(2),
        name="fnet_positions",
    )(cn, sn, uc, us)


def _na_geometry(rows):
    n_blocks = rows // NA_ROWS_PER_BLOCK
    kh = min(NA_KH, rows)
    variants, vi, start_blk = [], [], []
    for blk in range(n_blocks):
        start = int(np.clip(NA_ROWS_PER_BLOCK * blk - kh // 2, 0, rows - NA_WIN_ROWS))
        table = []
        for a in range(NA_ROWS_PER_BLOCK):
            r = NA_ROWS_PER_BLOCK * blk + a
            r0 = int(np.clip(r - kh // 2, 0, rows - kh))
            table.append(tuple((start + i) - r + (NA_KH - 1) if 0 <= (start + i) - r0 < kh else -1
                               for i in range(NA_WIN_ROWS)))
            assert sum(e >= 0 for e in table[-1]) == kh
        table = tuple(table)
        if table not in variants:
            variants.append(table)
        vi.append(variants.index(table))
        start_blk.append(start * GRID_W // NA_KBLK)
    return vi, start_blk, variants


def _na_pair_windows(variants):
    lows = []
    for table in variants:
        for j in range(NA_ROWS_PER_BLOCK // 2):
            valid = [i for i in range(NA_WIN_ROWS) if table[2 * j][i] >= 0 or table[2 * j + 1][i] >= 0]
            lo = min(min(valid), NA_WIN_ROWS - NA_PAIR_ROWS)
            assert max(valid) < lo + NA_PAIR_ROWS
            lows.append(lo)
    return lows


def _na_bias_kernel(rpb_ref, o_ref, *, variants, pair_lows):
    v = pl.program_id(0)
    h = pl.program_id(1)
    n_ro = 2 * NA_KH - 1
    n_co = 2 * NA_KW - 1
    kc = lax.broadcasted_iota(jnp.int32, (GRID_W, LANES), 0)
    lane = lax.broadcasted_iota(jnp.int32, (GRID_W, LANES), 1)
    c = lane & (GRID_W - 1)
    co = kc - c + (NA_KW - 1)
    c0 = jnp.clip(c - NA_KW // 2, 0, GRID_W - NA_KW)
    win = (kc >= c0) & (kc < c0 + NA_KW)
    neg = jnp.full((GRID_W, LANES), NEG_BIG, F32)
    tiles = []
    for ro in range(n_ro):
        t = neg
        for tt in range(n_co):
            t = jnp.where(win & (co == tt), rpb_ref[(h * n_ro + ro) * n_co + tt] * LOG2_E, t)
        tiles.append(t)
    left = lane < GRID_W
    for vv, table in enumerate(variants):
        @pl.when(v == vv)
        def _(vv=vv, table=table):
            for j in range(NA_ROWS_PER_BLOCK // 2):
                lo = pair_lows[vv * (NA_ROWS_PER_BLOCK // 2) + j]
                for ii in range(NA_PAIR_ROWS):
                    r1, r2 = table[2 * j][lo + ii], table[2 * j + 1][lo + ii]
                    t1 = tiles[r1] if r1 >= 0 else neg
                    t2 = tiles[r2] if r2 >= 0 else neg
                    o_ref[j, ii * GRID_W:(ii + 1) * GRID_W, :] = jnp.where(left, t1, t2)


def _na_bias_tables(rpb, variants, pair_lows):
    heads = rpb.shape[0]
    n_pairs = NA_ROWS_PER_BLOCK // 2
    nk = NA_PAIR_ROWS * GRID_W
    return pl.pallas_call(
        functools.partial(_na_bias_kernel, variants=variants, pair_lows=pair_lows),
        grid=(len(variants), heads),
        in_specs=[pl.BlockSpec(memory_space=pltpu.SMEM)],
        out_specs=pl.BlockSpec((None, None, n_pairs, nk, LANES), lambda v, h: (v, h, 0, 0, 0)),
        out_shape=jax.ShapeDtypeStruct((len(variants), heads, n_pairs, nk, LANES), F32),
        compiler_params=_params(2),
        name="na_bias_tables",
    )(rpb.reshape(-1))


def _pair_masks():
    lane = lax.broadcasted_iota(jnp.int32, (1, LANES), 1)
    return [(lane // NA_HEAD_DIM == hh).astype(BF16) for hh in range(LANES // NA_HEAD_DIM)]


def _na_body(q_ref, k_refs, vt_refs, kc_ref, vc_ref, bias_ref, o_ref, lows):
    span = NA_PAIR_ROWS * GRID_W
    nk = NA_WIN_ROWS * GRID_W
    q = q_ref[...]
    masks = _pair_masks()

    def scores(hh):
        qh = q * masks[hh]
        st = jnp.concatenate([_dot_nt(k_ref[...], qh) for k_ref in k_refs], axis=0)
        return st, _dot_nt(kc_ref[...], qh)

    outs = []
    st, sc = scores(0)
    for hh in range(len(masks)):
        nxt = scores(hh + 1) if hh + 1 < len(masks) else None
        ps, pcs, ls = [], [], []
        for j, lo in enumerate(lows):
            lanes = slice(j * LANES, (j + 1) * LANES)
            r0 = lo * GRID_W
            sj = st[r0:r0 + span, lanes] + bias_ref[hh, j]
            scj = sc[:, lanes]
            m = jnp.maximum(jnp.max(sj, axis=0, keepdims=True), jnp.max(scj, axis=0, keepdims=True))
            pj = jnp.exp2(sj - m)
            pcj = jnp.exp2(scj - m)
            ls.append(jnp.sum(pj, axis=0, keepdims=True) + jnp.sum(pcj, axis=0, keepdims=True))
            pieces = [jnp.zeros((r0, LANES), BF16), pj.astype(BF16), jnp.zeros((nk - r0 - span, LANES), BF16)]
            ps.append(jnp.concatenate([x for x in pieces if x.shape[0]], axis=0))
            pcs.append(pcj.astype(BF16))
        p = jnp.concatenate(ps, axis=1)
        pc = jnp.concatenate(pcs, axis=1)
        rows = slice(hh * NA_HEAD_DIM, (hh + 1) * NA_HEAD_DIM)
        acc = _dot(vc_ref[rows, :], pc)
        for b, vt_ref in enumerate(vt_refs):
            acc = acc + _dot(vt_ref[rows, :], p[b * NA_KBLK:(b + 1) * NA_KBLK])
        outs.append(acc / jnp.concatenate(ls, axis=1))
        if nxt is not None:
            st, sc = nxt
    o_ref[...] = jnp.transpose(jnp.concatenate(outs, axis=0)).astype(BF16)


def _na_kernel(vi_ref, sb_ref, q_ref, k0_ref, k1_ref, k2_ref, k3_ref, v0_ref, v1_ref, v2_ref, v3_ref,
               kc_ref, vc_ref, bias_ref, o_ref, *, pair_lows):
    del sb_ref
    variant = vi_ref[pl.program_id(1)]
    n_lp = NA_ROWS_PER_BLOCK // 2
    for vv in range(len(pair_lows) // n_lp):
        pl.when(variant == vv)(functools.partial(
            _na_body, q_ref, (k0_ref, k1_ref, k2_ref, k3_ref), (v0_ref, v1_ref, v2_ref, v3_ref),
            kc_ref, vc_ref, bias_ref, o_ref, pair_lows[vv * n_lp:(vv + 1) * n_lp]))


def _neighbourhood_attention(q, k, vt, kc, vct, bias, vi, start_blk, pair_lows, batch):
    n_pairs, t, _ = q.shape
    s = t // batch
    rows = s // GRID_W
    nq = NA_ROWS_PER_BLOCK * GRID_W
    nk = NA_WIN_ROWS * GRID_W
    n_blocks = rows // NA_ROWS_PER_BLOCK
    lc = kc.shape[1] // batch
    kblk_per_batch = s // NA_KBLK
    heads_per_pair = LANES // NA_HEAD_DIM
    bias6 = bias.reshape(bias.shape[0], n_pairs, heads_per_pair, *bias.shape[2:])

    assert nk == 4 * NA_KBLK

    def k_spec(j):
        return pl.BlockSpec((None, NA_KBLK, LANES),
                            lambda hp, i, b, vi_r, sb_r: (hp, b * kblk_per_batch + sb_r[i] + j, 0))

    def vt_spec(j):
        return pl.BlockSpec((LANES, NA_KBLK),
                            lambda hp, i, b, vi_r, sb_r: (hp, b * kblk_per_batch + sb_r[i] + j))

    grid_spec = pltpu.PrefetchScalarGridSpec(
        num_scalar_prefetch=2,
        grid=(n_pairs, n_blocks, batch),
        in_specs=[pl.BlockSpec((None, nq, LANES), lambda hp, i, b, vi_r, sb_r: (hp, b * n_blocks + i, 0))]
        + [k_spec(j) for j in range(4)] + [vt_spec(j) for j in range(4)]
        + [pl.BlockSpec((None, lc, LANES), lambda hp, i, b, vi_r, sb_r: (hp, b, 0)),
           pl.BlockSpec((LANES, lc), lambda hp, i, b, vi_r, sb_r: (hp, b)),
           pl.BlockSpec((None, None, heads_per_pair, *bias.shape[2:]),
                        lambda hp, i, b, vi_r, sb_r: (vi_r[i], hp, 0, 0, 0, 0))],
        out_specs=pl.BlockSpec((None, nq, LANES), lambda hp, i, b, vi_r, sb_r: (hp, b * n_blocks + i, 0)),
    )
    return pl.pallas_call(
        functools.partial(_na_kernel, pair_lows=tuple(pair_lows)),
        grid_spec=grid_spec,
        out_shape=jax.ShapeDtypeStruct((n_pairs, t, LANES), BF16),
        compiler_params=_params(3),
        name="neighbourhood_attention",
    )(jnp.asarray(vi, jnp.int32), jnp.asarray(start_blk, jnp.int32),
      q, k, k, k, k, vt, vt, vt, vt, kc, vct, bias6)


def _ctx_attn_kernel(q_ref, k_ref, vt_ref, o_ref):
    q = q_ref[...]
    k = k_ref[...]
    outs = []
    for hh, hm in enumerate(_pair_masks()):
        st = _dot_nt(k, q * hm)
        m = jnp.max(st, axis=0, keepdims=True)
        p = jnp.exp2(st - m)
        l = jnp.sum(p, axis=0, keepdims=True)
        outs.append(_dot(vt_ref[hh * NA_HEAD_DIM:(hh + 1) * NA_HEAD_DIM, :], p.astype(BF16)) / l)
    o_ref[...] = jnp.transpose(jnp.concatenate(outs, axis=0)).astype(BF16)


def _ctx_dense_attention(q, k, vt, batch):
    n_pairs, t, _ = q.shape
    lc = t // batch
    spec = pl.BlockSpec((None, lc, LANES), lambda b, hp: (hp, b, 0))
    return pl.pallas_call(
        _ctx_attn_kernel,
        grid=(batch, n_pairs),
        in_specs=[spec, spec, pl.BlockSpec((LANES, lc), lambda b, hp: (hp, b))],
        out_specs=spec,
        out_shape=jax.ShapeDtypeStruct((n_pairs, t, LANES), BF16),
        compiler_params=_params(2),
        name="ctx_dense_attention",
    )(q, k, vt)


MLA_QK_PAD = 256


def _rope(x, cos, sin_next, sin_prev):
    quarter = MLA_ROPE // 4
    return (x * cos + pltpu.roll(x, LANES - quarter, 1) * sin_next + pltpu.roll(x, quarter, 1) * sin_prev)


def _proj_mla_body(x, mod_ref, in_refs, out_refs, *, rope, need_q):
    g_ref, win_ref, gq_ref, gkv_ref, wqn_ref, wqr_ref, wuk_ref, wuv_ref, *rope_refs = in_refs
    if rope:
        cos_ref, sa_ref, sb_ref = rope_refs
    if need_q:
        q_ref, k_ref, v_ref = out_refs
    else:
        k_ref, v_ref = out_refs
    hb = _norm_mod(x, g_ref[...], mod_ref, 3).astype(BF16)
    z = _dot(hb, win_ref[...])
    ckv = (_rms(z[:, MLA_Q_RANK:MLA_Q_RANK + MLA_KV_RANK]) * gkv_ref[...]).astype(BF16)
    kr = z[:, MLA_Q_RANK + MLA_KV_RANK:]
    if rope:
        cos, sa, sb = cos_ref[...], sa_ref[...], sb_ref[...]
        kr = _rope(kr, cos, sa, sb)
    kr = kr.astype(BF16)
    kn = _dot(ckv, wuk_ref[...])
    v_ref[...] = _dot_nt(wuv_ref[...], ckv).astype(BF16)
    for h in range(MLA_HEADS):
        k_ref[h, :, :LANES] = kn[:, h * LANES:(h + 1) * LANES].astype(BF16)
        k_ref[h, :, LANES:] = kr
    if need_q:
        cq = (_rms(z[:, :MLA_Q_RANK]) * gq_ref[...]).astype(BF16)
        scale = (MLA_NOPE + MLA_ROPE) ** -0.5 * LOG2_E
        qn = _dot(cq, wqn_ref[...])
        qr = _dot(cq, wqr_ref[...])
        for h in range(MLA_HEADS):
            qrh = qr[:, h * LANES:(h + 1) * LANES]
            if rope:
                qrh = _rope(qrh, cos, sa, sb)
            q_ref[h, :, :LANES] = (qn[:, h * LANES:(h + 1) * LANES] * scale).astype(BF16)
            q_ref[h, :, LANES:] = (qrh * scale).astype(BF16)
    return x


def _rope_tables(n_tok):
    pos = jnp.arange(n_tok)
    row = (pos // GRID_W).astype(F32)
    col = (pos % GRID_W).astype(F32)
    half = MLA_ROPE // 2
    inv = ROPE_BASE ** (-jnp.arange(0, half, 2, dtype=F32) / half)
    ang_r = row[:, None] * inv[None]
    ang_c = col[:, None] * inv[None]
    ang = jnp.concatenate([ang_r, ang_r, ang_c, ang_c], axis=-1)
    cos, sin = jnp.cos(ang), jnp.sin(ang)
    quarter = MLA_ROPE // 4
    even_q = ((jnp.arange(MLA_ROPE) // quarter) % 2 == 0)[None, :]
    pad = LANES - MLA_ROPE
    cos_p = jnp.pad(cos, ((0, 0), (0, pad)), constant_values=1.0)
    sin_next = jnp.pad(jnp.where(even_q, -sin, 0.0), ((0, 0), (0, pad)))
    sin_prev = jnp.pad(jnp.where(even_q, 0.0, sin), ((0, 0), (0, pad)))
    return cos_p, sin_next, sin_prev


def _sec_proj_mla(g, w, rope_tabs, need_q):
    rope = rope_tabs is not None
    inputs = [(g.reshape(1, -1), "res")] + [(a, "res") for a in w]
    if rope:
        inputs += [(tab, "rope") for tab in rope_tabs]
    qk = ((MLA_HEADS, MLA_QK_PAD), "head")
    outputs = ([qk] if need_q else []) + [qk, (MLA_HEADS * MLA_V, "col")]
    return _Section(functools.partial(_proj_mla_body, rope=rope, need_q=need_q), inputs, outputs)


def _mla_attn_kernel(q_ref, *rest, n_src, chunk):
    k_refs, vt_refs, o_ref = rest[:n_src], rest[n_src:2 * n_src], rest[2 * n_src]
    q = q_ref[...]
    chunks = [(k_ref, vt_ref, c0, min(k_ref.shape[0], c0 + chunk))
              for k_ref, vt_ref in zip(k_refs, vt_refs) for c0 in range(0, k_ref.shape[0], chunk)]

    def scores(j):
        k_ref, _, c0, c1 = chunks[j]
        return _dot_nt(k_ref[c0:c1, :], q)

    m = l = acc = None
    st = scores(0)
    for j, (_, vt_ref, c0, c1) in enumerate(chunks):
        st_next = scores(j + 1) if j + 1 < len(chunks) else None
        mj = jnp.max(st, axis=0, keepdims=True)
        if m is None:
            m_new = mj
        else:
            m_new = jnp.maximum(m, mj)
            alpha = jnp.exp2(m - m_new)
        p = jnp.exp2(st - m_new)
        lj = jnp.sum(p, axis=0, keepdims=True)
        oj = _dot(vt_ref[:, c0:c1], p.astype(BF16))
        if m is None:
            l, acc = lj, oj
        else:
            l, acc = alpha * l + lj, alpha * acc + oj
        m, st = m_new, st_next
    o_ref[...] = jnp.transpose(acc / l).astype(BF16)


def _mla_attention(q, sources, batch, tq=1024, chunk=1024):
    t = q.shape[1]
    s = t // batch
    tq = min(tq, s)
    nq = s // tq
    k_specs, v_specs = [], []
    for k, _ in sources:
        n = k.shape[1] // batch
        k_specs.append(pl.BlockSpec((None, n, MLA_QK_PAD), lambda b, h, i: (h, b, 0)))
        v_specs.append(pl.BlockSpec((MLA_V, n), lambda b, h, i: (h, b)))
    return pl.pallas_call(
        functools.partial(_mla_attn_kernel, n_src=len(sources), chunk=chunk),
        grid=(batch, MLA_HEADS, nq),
        in_specs=[pl.BlockSpec((None, tq, MLA_QK_PAD), lambda b, h, i: (h, b * nq + i, 0))] + k_specs + v_specs,
        out_specs=pl.BlockSpec((None, tq, MLA_V), lambda b, h, i: (h, b * nq + i, 0)),
        out_shape=jax.ShapeDtypeStruct((MLA_HEADS, t, MLA_V), BF16),
        compiler_params=_params(3),
        name="mla_attention",
    )(q, *[k for k, _ in sources], *[v for _, v in sources])


def _mla_weights(w_in, g_q, g_kv, w_uq, w_uk, w_uv):
    w_in_p = jnp.pad(w_in, ((0, 0), (0, 5 * LANES - w_in.shape[1]))).astype(BF16)
    wq = w_uq.reshape(MLA_Q_RANK, MLA_HEADS, MLA_NOPE + MLA_ROPE)
    wqn = wq[:, :, :MLA_NOPE].reshape(MLA_Q_RANK, MLA_HEADS * MLA_NOPE).astype(BF16)
    wqr = jnp.pad(wq[:, :, MLA_NOPE:], ((0, 0), (0, 0), (0, LANES - MLA_ROPE)))
    wqr = wqr.reshape(MLA_Q_RANK, MLA_HEADS * LANES).astype(BF16)
    return (w_in_p, g_q.reshape(1, -1), g_kv.reshape(1, -1), wqn, wqr, w_uk.astype(BF16), w_uv.T.astype(BF16))


def kernel(x, c, ctx, c_ctx, ada_w, ada_b, norm_g, ffn_w_gate, ffn_w_up, ffn_w_down, ab_w_in, ab_rpb, ab_w_out,
           mla_w_in, mla_g_q, mla_g_kv, mla_w_uq, mla_w_uk, mla_w_uv, mla_w_o, final_g):
    batch, seq, d = x.shape
    lc = ctx.shape[1]
    depth = ada_w.shape[0]
    assert seq % (GRID_W * NA_WIN_ROWS) == 0 and lc % LANES == 0
    t_lat, t_ctx = batch * seq, batch * lc

    n_rows = -(-(batch + 1) // 8) * 8
    cvec = jnp.concatenate([c, c_ctx[None, :], jnp.zeros((n_rows - batch - 1, d), F32)], axis=0)
    mod = _modulation(cvec, ada_w, ada_b)

    x_l = x.reshape(t_lat, d)
    x_c = ctx.reshape(t_ctx, d)
    lat = dict(rows_per_mod=seq, mod_off=0)
    cx = dict(rows_per_mod=t_ctx, mod_off=batch)

    wg, wu, wd = ffn_w_gate.astype(BF16), ffn_w_up.astype(BF16), ffn_w_down.astype(BF16)
    rope_tabs = None

    for layer in range(depth):
        last = layer == depth - 1
        m = mod[layer]
        g = norm_g[layer]
        ffn1 = _sec_half_ffn(0, g[0], wg[layer, 0], wu[layer, 0], wd[layer, 0])
        ffn2 = _sec_half_ffn(6, g[2], wg[layer, 1], wu[layer, 1], wd[layer, 1])
        i = layer // 2
        if layer % 2 == 0:
            proj = _sec_proj_ab(g[1], ab_w_in[i])
            w_out = ab_w_out[i].astype(BF16)
            x_l, uc_l, us_l, q_l, k_l, vt_l = _token_stage(x_l, m, sections=[ffn1, proj], name="ffn_proj_ab", **lat)
            x_c, uc_c, us_c, q_c, k_c, vt_c = _token_stage(x_c, m, sections=[ffn1, proj], name="ffn_proj_ab", **cx)
            a_l = _fnet_positions(uc_l, us_l, batch)
            vi, start_blk, variants = _na_geometry(seq // GRID_W)
            pair_lows = _na_pair_windows(variants)
            bias = _na_bias_tables(ab_rpb[i], variants, pair_lows)
            b_l = _neighbourhood_attention(q_l, k_l, vt_l, k_c, vt_c, bias, vi, start_blk, pair_lows, batch)
            ys_l = [a_l, b_l]
            if not last:
                a_c = _fnet_positions(uc_c, us_c, batch)
                b_c = _ctx_dense_attention(q_c, k_c, vt_c, batch)
                ys_c = [a_c, b_c]
        else:
            if rope_tabs is None:
                rope_tabs = _rope_tables(seq)
            w = _mla_weights(mla_w_in[i], mla_g_q[i], mla_g_kv[i], mla_w_uq[i], mla_w_uk[i], mla_w_uv[i])
            w_out = mla_w_o[i].astype(BF16)
            x_l, q_l, k_l, vt_l = _token_stage(
                x_l, m, sections=[ffn1, _sec_proj_mla(g[1], w, rope_tabs, True)], name="ffn_proj_mla", seq=seq, **lat)
            x_c, *q_c, k_c, vt_c = _token_stage(
                x_c, m, sections=[ffn1, _sec_proj_mla(g[1], w, None, not last)], name="ffn_proj_mla", **cx)
            ys_l = [_mla_attention(q_l, [(k_l, vt_l), (k_c, vt_c)], batch)]
            if not last:
                ys_c = [_mla_attention(q_c[0], [(k_c, vt_c)], batch)]
        tail = [_sec_final_norm(final_g)] if last else []
        (x_l,) = _token_stage(x_l, m, sections=[_sec_outproj(w_out, ys_l), ffn2] + tail, name="outproj_ffn", **lat)
        if not last:
            (x_c,) = _token_stage(x_c, m, sections=[_sec_outproj(w_out, ys_c), ffn2], name="outproj_ffn", **cx)
    return x_l.reshape(batch, seq, d)
```

```python
import functools
import math

import numpy as np
import jax
import jax.numpy as jnp
from jax import lax
from jax.experimental import pallas as pl
from jax.experimental.pallas import tpu as pltpu

F32 = jnp.float32
BF16 = jnp.bfloat16

GRID_W = 64
EPS = 1e-6
N_MOD = 9
FNET_GROUPS = 4
FNET_GROUP_DIM = 128
FNET_DIM = FNET_GROUPS * FNET_GROUP_DIM
NA_HEADS = 8
NA_HEAD_DIM = 64
NA_DIM = NA_HEADS * NA_HEAD_DIM
NA_KH = 8
NA_KW = 16
MLA_HEADS = 8
MLA_NOPE = 128
MLA_ROPE = 64
MLA_V = 128
MLA_Q_RANK = 384
MLA_KV_RANK = 128
ROPE_BASE = 10000.0

LANES = 128
VMEM_LIMIT_BYTES = 56 * 1024 * 1024
NEG_BIG = -1e30
LOG2_E = math.log2(math.e)

NA_ROWS_PER_BLOCK = 8
NA_WIN_ROWS = 16
NA_KBLK = 256
NA_PAIR_ROWS = NA_KH + 1


def _params(n_axes):
    return pltpu.CompilerParams(
        dimension_semantics=("arbitrary",) * n_axes,
        vmem_limit_bytes=VMEM_LIMIT_BYTES,
    )


def _resident(shape):
    nd = len(shape)
    return pl.BlockSpec(shape, lambda *_: (0,) * nd, pipeline_mode=pl.Buffered(1))


def _dot(a, b):
    return jnp.dot(a, b, preferred_element_type=F32)


def _dot_nt(a, b):
    return lax.dot_general(a, b, (((1,), (1,)), ((), ())), preferred_element_type=F32)


def _rms(x):
    return x * lax.rsqrt(jnp.mean(x * x, axis=-1, keepdims=True) + EPS)


def _norm_mod(x, g, mod_ref, k0):
    shift = mod_ref[k0:k0 + 1, :]
    scale = mod_ref[k0 + 1:k0 + 2, :]
    return (_rms(x) * g) * (1.0 + scale) + shift


def _mod_kernel(c_ref, w_ref, b_ref, o_ref):
    c = c_ref[...]
    sc = c * jax.nn.sigmoid(c)
    w = w_ref[...]
    sc_hi, w_hi = sc.astype(BF16), w.astype(BF16)
    sc_lo = (sc - sc_hi.astype(F32)).astype(BF16)
    w_lo = (w - w_hi.astype(F32)).astype(BF16)
    r = sc.shape[0]
    a = _dot(jnp.concatenate([sc_hi, sc_lo], axis=0), w_hi)
    o_ref[...] = a[:r] + a[r:] + _dot(sc_hi, w_lo) + b_ref[...]


def _modulation(cvec, ada_w, ada_b):
    depth, d, n = ada_w.shape
    r = cvec.shape[0]
    tn = 1536
    assert n % tn == 0
    out = pl.pallas_call(
        _mod_kernel,
        grid=(depth, n // tn),
        in_specs=[
            pl.BlockSpec((r, d), lambda l, j: (0, 0)),
            pl.BlockSpec((None, d, tn), lambda l, j: (l, 0, j)),
            pl.BlockSpec((None, 1, tn), lambda l, j: (l, 0, j)),
        ],
        out_specs=pl.BlockSpec((None, r, tn), lambda l, j: (l, 0, j)),
        out_shape=jax.ShapeDtypeStruct((depth, r, n), F32),
        compiler_params=_params(2),
        name="adaln_modulation",
    )(cvec, ada_w, ada_b.reshape(depth, 1, n))
    return out.reshape(depth, r, N_MOD, d)


def _mod_spec(d, tm, rows_per_mod, mod_off):
    return pl.BlockSpec((None, N_MOD, d), lambda i: (mod_off + (i * tm) // rows_per_mod, 0, 0))


class _Section:
    def __init__(self, body, inputs=(), outputs=(), writes_x=False):
        self.body, self.inputs, self.outputs, self.writes_x = body, list(inputs), list(outputs), writes_x


def _stage_kernel(*refs, sections, writes_x):
    x_ref, mod_ref = refs[0], refs[1]
    n_in = sum(len(s.inputs) for s in sections)
    in_refs = refs[2:2 + n_in]
    out_refs = refs[2 + n_in:]
    x = x_ref[...]
    xo_ref = None
    if writes_x:
        xo_ref, out_refs = out_refs[0], out_refs[1:]
    i0 = o0 = 0
    for s in sections:
        x = s.body(x, mod_ref, in_refs[i0:i0 + len(s.inputs)], out_refs[o0:o0 + len(s.outputs)])
        i0 += len(s.inputs)
        o0 += len(s.outputs)
    if writes_x:
        xo_ref[...] = x


def _token_stage(x, mod, rows_per_mod, mod_off, sections, name, seq=None, tm=512):
    t, d = x.shape
    tm = min(tm, t)
    assert t % tm == 0 and rows_per_mod % tm == 0
    writes_x = any(s.writes_x for s in sections)
    in_specs = [pl.BlockSpec((tm, d), lambda i: (i, 0)), _mod_spec(d, tm, rows_per_mod, mod_off)]
    args = [x, mod]
    for s in sections:
        for arr, kind in s.inputs:
            if kind == "res":
                in_specs.append(_resident(arr.shape))
            elif kind == "row":
                in_specs.append(pl.BlockSpec((tm, arr.shape[1]), lambda i: (i, 0)))
            elif kind == "head":
                in_specs.append(pl.BlockSpec((arr.shape[0], tm, arr.shape[2]), lambda i: (0, i, 0)))
            else:
                assert kind == "rope" and seq % tm == 0
                n_rt = seq // tm
                in_specs.append(pl.BlockSpec((tm, arr.shape[1]), lambda i: (i % n_rt, 0)))
            args.append(arr)
    out_specs, out_shape = [], []
    if writes_x:
        out_specs.append(pl.BlockSpec((tm, d), lambda i: (i, 0)))
        out_shape.append(jax.ShapeDtypeStruct((t, d), F32))
    for s in sections:
        for w, kind in s.outputs:
            if kind == "row":
                out_specs.append(pl.BlockSpec((tm, w), lambda i: (i, 0)))
                out_shape.append(jax.ShapeDtypeStruct((t, w), BF16))
            elif kind == "head":
                heads, hw = w
                out_specs.append(pl.BlockSpec((heads, tm, hw), lambda i: (0, i, 0)))
                out_shape.append(jax.ShapeDtypeStruct((heads, t, hw), BF16))
            else:
                out_specs.append(pl.BlockSpec((w, tm), lambda i: (0, i)))
                out_shape.append(jax.ShapeDtypeStruct((w, t), BF16))
    return pl.pallas_call(
        functools.partial(_stage_kernel, sections=sections, writes_x=writes_x),
        grid=(t // tm,),
        in_specs=in_specs,
        out_specs=out_specs,
        out_shape=out_shape,
        compiler_params=_params(1),
        name=name,
    )(*args)


FFN_CHUNKS = 2


def _sec_half_ffn(k0, g, wg, wu, wd):
    def body(x, mod_ref, in_refs, out_refs):
        g_ref, wg_ref, wu_ref, wd_ref = in_refs
        hb = _norm_mod(x, g_ref[...], mod_ref, k0).astype(BF16)
        tf = wg_ref.shape[1] // FFN_CHUNKS
        y = None
        for j in range(FFN_CHUNKS):
            a = _dot(hb, wg_ref[:, j * tf:(j + 1) * tf])
            u = _dot(hb, wu_ref[:, j * tf:(j + 1) * tf])
            act = ((a * jax.nn.sigmoid(a)) * u).astype(BF16)
            yj = _dot(act, wd_ref[j * tf:(j + 1) * tf, :])
            y = yj if y is None else y + yj
        return x + (0.5 * mod_ref[k0 + 2:k0 + 3, :]) * y

    assert wg.shape[1] % (FFN_CHUNKS * LANES) == 0
    return _Section(body, [(g.reshape(1, -1), "res"), (wg, "res"), (wu, "res"), (wd, "res")], writes_x=True)


def _sec_outproj(w, ys):
    def body(x, mod_ref, in_refs, out_refs):
        w_ref, *y_refs = in_refs
        acc, r0 = None, 0
        for y_ref in y_refs:
            if len(y_ref.shape) == 3:
                y = jnp.concatenate([y_ref[h] for h in range(y_ref.shape[0])], axis=1)
            else:
                y = y_ref[...]
            k = y.shape[1]
            part = _dot(y, w_ref[r0:r0 + k, :])
            acc = part if acc is None else acc + part
            r0 += k
        return x + mod_ref[5:6, :] * acc

    return _Section(body, [(w, "res")] + [(y, "head" if y.ndim == 3 else "row") for y in ys], writes_x=True)


def _sec_final_norm(g):
    def body(x, mod_ref, in_refs, out_refs):
        return _rms(x) * in_refs[0][...]

    return _Section(body, [(g.reshape(1, -1), "res")], writes_x=True)


def _proj_ab_body(x, mod_ref, in_refs, out_refs):
    g_ref, w_ref, wvt_ref, cc_ref, cs_ref = in_refs
    uc_ref, us_ref, q_ref, k_ref, vt_ref = out_refs
    hb = _norm_mod(x, g_ref[...], mod_ref, 3).astype(BF16)
    z = _dot(hb, w_ref[...])
    u = z[:, :FNET_DIM].astype(BF16)
    for g in range(FNET_GROUPS):
        sl = slice(g * FNET_GROUP_DIM, (g + 1) * FNET_GROUP_DIM)
        uc_ref[:, sl] = _dot(u[:, sl], cc_ref[...]).astype(BF16)
        us_ref[:, sl] = _dot(u[:, sl], cs_ref[...]).astype(BF16)
    for hp in range(NA_DIM // LANES):
        q0, k0 = FNET_DIM + hp * LANES, FNET_DIM + NA_DIM + hp * LANES
        q_ref[hp] = (z[:, q0:q0 + LANES] * (NA_HEAD_DIM ** -0.5 * LOG2_E)).astype(BF16)
        k_ref[hp] = z[:, k0:k0 + LANES].astype(BF16)
    vt_ref[...] = _dot_nt(wvt_ref[...], hb).astype(BF16)
    return x


def _dft_real_imag(n, scale):
    j = np.arange(n, dtype=np.int64)
    ang = 2.0 * np.pi * ((j[:, None] * j[None, :]) % n).astype(np.float64) / n
    return (np.cos(ang) * scale).astype(np.float32), (np.sin(ang) * scale).astype(np.float32)


def _sec_proj_ab(g, w_in):
    cc, cs = _dft_real_imag(FNET_GROUP_DIM, FNET_GROUP_DIM ** -0.5)
    n_row = FNET_DIM + 2 * NA_DIM
    w_row = w_in[:, :n_row].astype(BF16)
    w_vt = w_in[:, n_row:].T.astype(BF16)
    inputs = [(g.reshape(1, -1), "res"), (w_row, "res"), (w_vt, "res"),
              (jnp.asarray(cc).astype(BF16), "res"), (jnp.asarray(cs).astype(BF16), "res")]
    pairs = ((NA_DIM // LANES, LANES), "head")
    outputs = [(FNET_DIM, "row"), (FNET_DIM, "row"), pairs, pairs, (NA_DIM, "col")]
    return _Section(_proj_ab_body, inputs, outputs)


def _fnet_kernel(cn_ref, sn_ref, uc_ref, us_ref, o_ref):
    o = _dot(cn_ref[...], uc_ref[...]) - _dot(sn_ref[...], us_ref[...])
    o_ref[...] = o.astype(BF16)


def _position_dft_matrices(n):
    if n <= 512:
        c, s = _dft_real_imag(n, n ** -0.5)
        return jnp.asarray(c).astype(BF16), jnp.asarray(s).astype(BF16)
    n1 = GRID_W
    n2 = n // n1
    k = np.arange(n, dtype=np.int64)
    hi = np.arange(n2, dtype=np.int64)[:, None] * n1
    lo = np.arange(n1, dtype=np.int64)[:, None]
    ang_hi = 2.0 * np.pi * ((hi * k[None, :]) % n).astype(np.float64) / n
    ang_lo = 2.0 * np.pi * ((lo * k[None, :]) % n).astype(np.float64) / n
    ch, sh = jnp.asarray(np.cos(ang_hi), F32)[:, None, :], jnp.asarray(np.sin(ang_hi), F32)[:, None, :]
    cl, sl = jnp.asarray(np.cos(ang_lo), F32)[None, :, :], jnp.asarray(np.sin(ang_lo), F32)[None, :, :]
    scale = n ** -0.5
    c = ((ch * cl - sh * sl) * scale).reshape(n, n).astype(BF16)
    s = ((sh * cl + ch * sl) * scale).reshape(n, n).astype(BF16)
    return c, s


def _fnet_positions(uc, us, batch):
    t, w = uc.shape
    n = t // batch
    cn, sn = _position_dft_matrices(n)
    tn = min(512, n)
    nt = n // tn
    return pl.pallas_call(
        _fnet_kernel,
        grid=(nt, batch),
        in_specs=[
            pl.BlockSpec((tn, n), lambda i, b: (i, 0)),
            pl.BlockSpec((tn, n), lambda i, b: (i, 0)),
            pl.BlockSpec((n, w), lambda i, b: (b, 0)),
            pl.BlockSpec((n, w), lambda i, b: (b, 0)),
        ],
        out_specs=pl.BlockSpec((tn, w), lambda i, b: (b * nt + i, 0)),
        out_shape=jax.ShapeDtypeStruct((t, w), BF16),
        compiler_params=_params(2),
        name="fnet_positions",
    )(cn, sn, uc, us)


def _na_geometry(rows):
    n_blocks = rows // NA_ROWS_PER_BLOCK
    kh = min(NA_KH, rows)
    variants, vi, start_blk = [], [], []
    for blk in range(n_blocks):
        start = int(np.clip(NA_ROWS_PER_BLOCK * blk - kh // 2, 0, rows - NA_WIN_ROWS))
        table = []
        for a in range(NA_ROWS_PER_BLOCK):
            r = NA_ROWS_PER_BLOCK * blk + a
            r0 = int(np.clip(r - kh // 2, 0, rows - kh))
            table.append(tuple((start + i) - r + (NA_KH - 1) if 0 <= (start + i) - r0 < kh else -1
                               for i in range(NA_WIN_ROWS)))
            assert sum(e >= 0 for e in table[-1]) == kh
        table = tuple(table)
        if table not in variants:
            variants.append(table)
        vi.append(variants.index(table))
        start_blk.append(start * GRID_W // NA_KBLK)
    return vi, start_blk, variants


def _na_pair_windows(variants):
    lows = []
    for table in variants:
        for j in range(NA_ROWS_PER_BLOCK // 2):
            valid = [i for i in range(NA_WIN_ROWS) if table[2 * j][i] >= 0 or table[2 * j + 1][i] >= 0]
            lo = min(min(valid), NA_WIN_ROWS - NA_PAIR_ROWS)
            assert max(valid) < lo + NA_PAIR_ROWS
            lows.append(lo)
    return lows


def _na_bias_kernel(rpb_ref, o_ref, *, variants, pair_lows):
    v = pl.program_id(0)
    h = pl.program_id(1)
    n_ro = 2 * NA_KH - 1
    n_co = 2 * NA_KW - 1
    kc = lax.broadcasted_iota(jnp.int32, (GRID_W, LANES), 0)
    lane = lax.broadcasted_iota(jnp.int32, (GRID_W, LANES), 1)
    c = lane & (GRID_W - 1)
    co = kc - c + (NA_KW - 1)
    c0 = jnp.clip(c - NA_KW // 2, 0, GRID_W - NA_KW)
    win = (kc >= c0) & (kc < c0 + NA_KW)
    neg = jnp.full((GRID_W, LANES), NEG_BIG, F32)
    tiles = []
    for ro in range(n_ro):
        t = neg
        for tt in range(n_co):
            t = jnp.where(win & (co == tt), rpb_ref[(h * n_ro + ro) * n_co + tt] * LOG2_E, t)
        tiles.append(t)
    left = lane < GRID_W
    for vv, table in enumerate(variants):
        @pl.when(v == vv)
        def _(vv=vv, table=table):
            for j in range(NA_ROWS_PER_BLOCK // 2):
                lo = pair_lows[vv * (NA_ROWS_PER_BLOCK // 2) + j]
                for ii in range(NA_PAIR_ROWS):
                    r1, r2 = table[2 * j][lo + ii], table[2 * j + 1][lo + ii]
                    t1 = tiles[r1] if r1 >= 0 else neg
                    t2 = tiles[r2] if r2 >= 0 else neg
                    o_ref[j, ii * GRID_W:(ii + 1) * GRID_W, :] = jnp.where(left, t1, t2)


def _na_bias_tables(rpb, variants, pair_lows):
    heads = rpb.shape[0]
    n_pairs = NA_ROWS_PER_BLOCK // 2
    nk = NA_PAIR_ROWS * GRID_W
    return pl.pallas_call(
        functools.partial(_na_bias_kernel, variants=variants, pair_lows=pair_lows),
        grid=(len(variants), heads),
        in_specs=[pl.BlockSpec(memory_space=pltpu.SMEM)],
        out_specs=pl.BlockSpec((None, None, n_pairs, nk, LANES), lambda v, h: (v, h, 0, 0, 0)),
        out_shape=jax.ShapeDtypeStruct((len(variants), heads, n_pairs, nk, LANES), F32),
        compiler_params=_params(2),
        name="na_bias_tables",
    )(rpb.reshape(-1))


def _pair_masks():
    lane = lax.broadcasted_iota(jnp.int32, (1, LANES), 1)
    return [(lane // NA_HEAD_DIM == hh).astype(BF16) for hh in range(LANES // NA_HEAD_DIM)]


def _na_body(q_ref, k_refs, vt_refs, kc_ref, vc_ref, bias_ref, o_ref, lows):
    span = NA_PAIR_ROWS * GRID_W
    nk = NA_WIN_ROWS * GRID_W
    q = q_ref[...]
    masks = _pair_masks()

    def scores(hh):
        qh = q * masks[hh]
        st = jnp.concatenate([_dot_nt(k_ref[...], qh) for k_ref in k_refs], axis=0)
        return st, _dot_nt(kc_ref[...], qh)

    outs = []
    st, sc = scores(0)
    for hh in range(len(masks)):
        nxt = scores(hh + 1) if hh + 1 < len(masks) else None
        ps, pcs, ls = [], [], []
        for j, lo in enumerate(lows):
            lanes = slice(j * LANES, (j + 1) * LANES)
            r0 = lo * GRID_W
            sj = st[r0:r0 + span, lanes] + bias_ref[hh, j]
            scj = sc[:, lanes]
            m = jnp.maximum(jnp.max(sj, axis=0, keepdims=True), jnp.max(scj, axis=0, keepdims=True))
            pj = jnp.exp2(sj - m)
            pcj = jnp.exp2(scj - m)
            ls.append(jnp.sum(pj, axis=0, keepdims=True) + jnp.sum(pcj, axis=0, keepdims=True))
            pieces = [jnp.zeros((r0, LANES), BF16), pj.astype(BF16), jnp.zeros((nk - r0 - span, LANES), BF16)]
            ps.append(jnp.concatenate([x for x in pieces if x.shape[0]], axis=0))
            pcs.append(pcj.astype(BF16))
        p = jnp.concatenate(ps, axis=1)
        pc = jnp.concatenate(pcs, axis=1)
        rows = slice(hh * NA_HEAD_DIM, (hh + 1) * NA_HEAD_DIM)
        acc = _dot(vc_ref[rows, :], pc)
        for b, vt_ref in enumerate(vt_refs):
            acc = acc + _dot(vt_ref[rows, :], p[b * NA_KBLK:(b + 1) * NA_KBLK])
        outs.append(acc / jnp.concatenate(ls, axis=1))
        if nxt is not None:
            st, sc = nxt
    o_ref[...] = jnp.transpose(jnp.concatenate(outs, axis=0)).astype(BF16)


def _na_kernel(vi_ref, sb_ref, q_ref, k0_ref, k1_ref, k2_ref, k3_ref, v0_ref, v1_ref, v2_ref, v3_ref,
               kc_ref, vc_ref, bias_ref, o_ref, *, pair_lows):
    del sb_ref
    variant = vi_ref[pl.program_id(1)]
    n_lp = NA_ROWS_PER_BLOCK // 2
    for vv in range(len(pair_lows) // n_lp):
        pl.when(variant == vv)(functools.partial(
            _na_body, q_ref, (k0_ref, k1_ref, k2_ref, k3_ref), (v0_ref, v1_ref, v2_ref, v3_ref),
            kc_ref, vc_ref, bias_ref, o_ref, pair_lows[vv * n_lp:(vv + 1) * n_lp]))


def _neighbourhood_attention(q, k, vt, kc, vct, bias, vi, start_blk, pair_lows, batch):
    n_pairs, t, _ = q.shape
    s = t // batch
    rows = s // GRID_W
    nq = NA_ROWS_PER_BLOCK * GRID_W
    nk = NA_WIN_ROWS * GRID_W
    n_blocks = rows // NA_ROWS_PER_BLOCK
    lc = kc.shape[1] // batch
    kblk_per_batch = s // NA_KBLK
    heads_per_pair = LANES // NA_HEAD_DIM
    bias6 = bias.reshape(bias.shape[0], n_pairs, heads_per_pair, *bias.shape[2:])

    assert nk == 4 * NA_KBLK

    def k_spec(j):
        return pl.BlockSpec((None, NA_KBLK, LANES),
                            lambda hp, i, b, vi_r, sb_r: (hp, b * kblk_per_batch + sb_r[i] + j, 0))

    def vt_spec(j):
        return pl.BlockSpec((LANES, NA_KBLK),
                            lambda hp, i, b, vi_r, sb_r: (hp, b * kblk_per_batch + sb_r[i] + j))

    grid_spec = pltpu.PrefetchScalarGridSpec(
        num_scalar_prefetch=2,
        grid=(n_pairs, n_blocks, batch),
        in_specs=[pl.BlockSpec((None, nq, LANES), lambda hp, i, b, vi_r, sb_r: (hp, b * n_blocks + i, 0))]
        + [k_spec(j) for j in range(4)] + [vt_spec(j) for j in range(4)]
        + [pl.BlockSpec((None, lc, LANES), lambda hp, i, b, vi_r, sb_r: (hp, b, 0)),
           pl.BlockSpec((LANES, lc), lambda hp, i, b, vi_r, sb_r: (hp, b)),
           pl.BlockSpec((None, None, heads_per_pair, *bias.shape[2:]),
                        lambda hp, i, b, vi_r, sb_r: (vi_r[i], hp, 0, 0, 0, 0))],
        out_specs=pl.BlockSpec((None, nq, LANES), lambda hp, i, b, vi_r, sb_r: (hp, b * n_blocks + i, 0)),
    )
    return pl.pallas_call(
        functools.partial(_na_kernel, pair_lows=tuple(pair_lows)),
        grid_spec=grid_spec,
        out_shape=jax.ShapeDtypeStruct((n_pairs, t, LANES), BF16),
        compiler_params=_params(3),
        name="neighbourhood_attention",
    )(jnp.asarray(vi, jnp.int32), jnp.asarray(start_blk, jnp.int32),
      q, k, k, k, k, vt, vt, vt, vt, kc, vct, bias6)


def _ctx_attn_kernel(q_ref, k_ref, vt_ref, o_ref):
    q = q_ref[...]
    k = k_ref[...]
    outs = []
    for hh, hm in enumerate(_pair_masks()):
        st = _dot_nt(k, q * hm)
        m = jnp.max(st, axis=0, keepdims=True)
        p = jnp.exp2(st - m)
        l = jnp.sum(p, axis=0, keepdims=True)
        outs.append(_dot(vt_ref[hh * NA_HEAD_DIM:(hh + 1) * NA_HEAD_DIM, :], p.astype(BF16)) / l)
    o_ref[...] = jnp.transpose(jnp.concatenate(outs, axis=0)).astype(BF16)


def _ctx_dense_attention(q, k, vt, batch):
    n_pairs, t, _ = q.shape
    lc = t // batch
    spec = pl.BlockSpec((None, lc, LANES), lambda b, hp: (hp, b, 0))
    return pl.pallas_call(
        _ctx_attn_kernel,
        grid=(batch, n_pairs),
        in_specs=[spec, spec, pl.BlockSpec((LANES, lc), lambda b, hp: (hp, b))],
        out_specs=spec,
        out_shape=jax.ShapeDtypeStruct((n_pairs, t, LANES), BF16),
        compiler_params=_params(2),
        name="ctx_dense_attention",
    )(q, k, vt)


MLA_QK_PAD = 256
MLA_LOOKAHEAD = 1


def _rope(x, cos, sin_next, sin_prev):
    quarter = MLA_ROPE // 4
    return (x * cos + pltpu.roll(x, LANES - quarter, 1) * sin_next + pltpu.roll(x, quarter, 1) * sin_prev)


def _proj_mla_body(x, mod_ref, in_refs, out_refs, *, rope, need_q):
    g_ref, win_ref, gq_ref, gkv_ref, wqn_ref, wqr_ref, wuk_ref, wuv_ref, *rope_refs = in_refs
    if rope:
        cos_ref, sa_ref, sb_ref = rope_refs
    if need_q:
        q_ref, k_ref, v_ref = out_refs
    else:
        k_ref, v_ref = out_refs
    hb = _norm_mod(x, g_ref[...], mod_ref, 3).astype(BF16)
    z = _dot(hb, win_ref[...])
    ckv = (_rms(z[:, MLA_Q_RANK:MLA_Q_RANK + MLA_KV_RANK]) * gkv_ref[...]).astype(BF16)
    kr = z[:, MLA_Q_RANK + MLA_KV_RANK:]
    if rope:
        cos, sa, sb = cos_ref[...], sa_ref[...], sb_ref[...]
        kr = _rope(kr, cos, sa, sb)
    kr = kr.astype(BF16)
    kn = _dot(ckv, wuk_ref[...])
    v_ref[...] = _dot_nt(wuv_ref[...], ckv).astype(BF16)
    for h in range(MLA_HEADS):
        k_ref[h, :, :LANES] = kn[:, h * LANES:(h + 1) * LANES].astype(BF16)
        k_ref[h, :, LANES:] = kr
    if need_q:
        cq = (_rms(z[:, :MLA_Q_RANK]) * gq_ref[...]).astype(BF16)
        scale = (MLA_NOPE + MLA_ROPE) ** -0.5 * LOG2_E
        qn = _dot(cq, wqn_ref[...])
        qr = _dot(cq, wqr_ref[...])
        for h in range(MLA_HEADS):
            qrh = qr[:, h * LANES:(h + 1) * LANES]
            if rope:
                qrh = _rope(qrh, cos, sa, sb)
            q_ref[h, :, :LANES] = (qn[:, h * LANES:(h + 1) * LANES] * scale).astype(BF16)
            q_ref[h, :, LANES:] = (qrh * scale).astype(BF16)
    return x


def _rope_tables(n_tok):
    pos = jnp.arange(n_tok)
    row = (pos // GRID_W).astype(F32)
    col = (pos % GRID_W).astype(F32)
    half = MLA_ROPE // 2
    inv = ROPE_BASE ** (-jnp.arange(0, half, 2, dtype=F32) / half)
    ang_r = row[:, None] * inv[None]
    ang_c = col[:, None] * inv[None]
    ang = jnp.concatenate([ang_r, ang_r, ang_c, ang_c], axis=-1)
    cos, sin = jnp.cos(ang), jnp.sin(ang)
    quarter = MLA_ROPE // 4
    even_q = ((jnp.arange(MLA_ROPE) // quarter) % 2 == 0)[None, :]
    pad = LANES - MLA_ROPE
    cos_p = jnp.pad(cos, ((0, 0), (0, pad)), constant_values=1.0)
    sin_next = jnp.pad(jnp.where(even_q, -sin, 0.0), ((0, 0), (0, pad)))
    sin_prev = jnp.pad(jnp.where(even_q, 0.0, sin), ((0, 0), (0, pad)))
    return cos_p, sin_next, sin_prev


def _sec_proj_mla(g, w, rope_tabs, need_q):
    rope = rope_tabs is not None
    inputs = [(g.reshape(1, -1), "res")] + [(a, "res") for a in w]
    if rope:
        inputs += [(tab, "rope") for tab in rope_tabs]
    qk = ((MLA_HEADS, MLA_QK_PAD), "head")
    outputs = ([qk] if need_q else []) + [qk, (MLA_HEADS * MLA_V, "col")]
    return _Section(functools.partial(_proj_mla_body, rope=rope, need_q=need_q), inputs, outputs)


def _mla_attn_kernel(q_ref, *rest, n_src, chunk):
    k_refs, vt_refs, o_ref = rest[:n_src], rest[n_src:2 * n_src], rest[2 * n_src]
    q = q_ref[...]
    chunks = [(k_ref, vt_ref, c0, min(k_ref.shape[0], c0 + chunk))
              for k_ref, vt_ref in zip(k_refs, vt_refs) for c0 in range(0, k_ref.shape[0], chunk)]

    def scores(j):
        k_ref, _, c0, c1 = chunks[j]
        return _dot_nt(k_ref[c0:c1, :], q)

    m = l = acc = None
    ahead =[scores(j) for j in range(min(MLA_LOOKAHEAD, len(chunks)))]
    for j, (_, vt_ref, c0, c1) in enumerate(chunks):
        if j + MLA_LOOKAHEAD < len(chunks):
            ahead.append(scores(j + MLA_LOOKAHEAD))
        st = ahead.pop(0)
        mj = jnp.max(st, axis=0, keepdims=True)
        if m is None:
            m_new = mj
        else:
            m_new = jnp.maximum(m, mj)
            alpha = jnp.exp2(m - m_new)
        p = jnp.exp2(st - m_new)
        lj = jnp.sum(p, axis=0, keepdims=True)
        oj = _dot(vt_ref[:, c0:c1], p.astype(BF16))
        if m is None:
            l, acc = lj, oj
        else:
            l, acc = alpha * l + lj, alpha * acc + oj
        m = m_new
    o_ref[...] = jnp.transpose(acc / l).astype(BF16)


def _mla_attention(q, sources, batch, tq=1024, chunk=1024):
    t = q.shape[1]
    s = t // batch
    tq = min(tq, s)
    nq = s // tq
    k_specs, v_specs = [], []
    for k, _ in sources:
        n = k.shape[1] // batch
        k_specs.append(pl.BlockSpec((None, n, MLA_QK_PAD), lambda b, h, i: (h, b, 0)))
        v_specs.append(pl.BlockSpec((MLA_V, n), lambda b, h, i: (h, b)))
    return pl.pallas_call(
        functools.partial(_mla_attn_kernel, n_src=len(sources), chunk=chunk),
        grid=(batch, MLA_HEADS, nq),
        in_specs=[pl.BlockSpec((None, tq, MLA_QK_PAD), lambda b, h, i: (h, b * nq + i, 0))] + k_specs + v_specs,
        out_specs=pl.BlockSpec((None, tq, MLA_V), lambda b, h, i: (h, b * nq + i, 0)),
        out_shape=jax.ShapeDtypeStruct((MLA_HEADS, t, MLA_V), BF16),
        compiler_params=_params(3),
        name="mla_attention",
    )(q, *[k for k, _ in sources], *[v for _, v in sources])


def _mla_weights(w_in, g_q, g_kv, w_uq, w_uk, w_uv):
    w_in_p = jnp.pad(w_in, ((0, 0), (0, 5 * LANES - w_in.shape[1]))).astype(BF16)
    wq = w_uq.reshape(MLA_Q_RANK, MLA_HEADS, MLA_NOPE + MLA_ROPE)
    wqn = wq[:, :, :MLA_NOPE].reshape(MLA_Q_RANK, MLA_HEADS * MLA_NOPE).astype(BF16)
    wqr = jnp.pad(wq[:, :, MLA_NOPE:], ((0, 0), (0, 0), (0, LANES - MLA_ROPE)))
    wqr = wqr.reshape(MLA_Q_RANK, MLA_HEADS * LANES).astype(BF16)
    return (w_in_p, g_q.reshape(1, -1), g_kv.reshape(1, -1), wqn, wqr, w_uk.astype(BF16), w_uv.T.astype(BF16))


def kernel(x, c, ctx, c_ctx, ada_w, ada_b, norm_g, ffn_w_gate, ffn_w_up, ffn_w_down, ab_w_in, ab_rpb, ab_w_out,
           mla_w_in, mla_g_q, mla_g_kv, mla_w_uq, mla_w_uk, mla_w_uv, mla_w_o, final_g):
    batch, seq, d = x.shape
    lc = ctx.shape[1]
    depth = ada_w.shape[0]
    assert seq % (GRID_W * NA_WIN_ROWS) == 0 and lc % LANES == 0
    t_lat, t_ctx = batch * seq, batch * lc

    n_rows = -(-(batch + 1) // 8) * 8
    cvec = jnp.concatenate([c, c_ctx[None, :], jnp.zeros((n_rows - batch - 1, d), F32)], axis=0)
    mod = _modulation(cvec, ada_w, ada_b)

    x_l = x.reshape(t_lat, d)
    x_c = ctx.reshape(t_ctx, d)
    lat = dict(rows_per_mod=seq, mod_off=0)
    cx = dict(rows_per_mod=t_ctx, mod_off=batch)

    rope_tabs = None

    for layer in range(depth):
        last = layer == depth - 1
        m = mod[layer]
        g = norm_g[layer]
        ffn1, ffn2 = (
            _sec_half_ffn(k0, g[gi], *(w[layer, j].astype(BF16) for w in (ffn_w_gate, ffn_w_up, ffn_w_down)))
            for j, (k0, gi) in enumerate(((0, 0), (6, 2))))
        i = layer // 2
        if layer % 2 == 0:
            proj = _sec_proj_ab(g[1], ab_w_in[i])
            w_out = ab_w_out[i].astype(BF16)
            x_l, uc_l, us_l, q_l, k_l, vt_l = _token_stage(x_l, m, sections=[ffn1, proj], name="ffn_proj_ab", **lat)
            x_c, uc_c, us_c, q_c, k_c, vt_c = _token_stage(x_c, m, sections=[ffn1, proj], name="ffn_proj_ab", **cx)
            a_l = _fnet_positions(uc_l, us_l, batch)
            vi, start_blk, variants = _na_geometry(seq // GRID_W)
            pair_lows = _na_pair_windows(variants)
            bias = _na_bias_tables(ab_rpb[i], variants, pair_lows)
            b_l = _neighbourhood_attention(q_l, k_l, vt_l, k_c, vt_c, bias, vi, start_blk, pair_lows, batch)
            ys_l = [a_l, b_l]
            if not last:
                a_c = _fnet_positions(uc_c, us_c, batch)
                b_c = _ctx_dense_attention(q_c, k_c, vt_c, batch)
                ys_c = [a_c, b_c]
        else:
            if rope_tabs is None:
                rope_tabs = _rope_tables(seq)
            w = _mla_weights(mla_w_in[i], mla_g_q[i], mla_g_kv[i], mla_w_uq[i], mla_w_uk[i], mla_w_uv[i])
            w_out = mla_w_o[i].astype(BF16)
            x_l, q_l, k_l, vt_l = _token_stage(
                x_l, m, sections=[ffn1, _sec_proj_mla(g[1], w, rope_tabs, True)], name="ffn_proj_mla", seq=seq, **lat)
            x_c, *q_c, k_c, vt_c = _token_stage(
                x_c, m, sections=[ffn1, _sec_proj_mla(g[1], w, None, not last)], name="ffn_proj_mla", **cx)
            ys_l = [_mla_attention(q_l, [(k_l, vt_l), (k_c, vt_c)], batch)]
            if not last:
                ys_c = [_mla_attention(q_c[0], [(k_c, vt_c)], batch)]
        tail = [_sec_final_norm(final_g)] if last else []
        (x_l,) = _token_stage(x_l, m, sections=[_sec_outproj(w_out, ys_l), ffn2] + tail, name="outproj_ffn", **lat)
        if not last:
            (x_c,) = _token_stage(x_c, m, sections=[_sec_outproj(w_out, ys_c), ffn2], name="outproj_ffn", **cx)
    return x_l.reshape(batch, seq, d)
```

```python
import functools
import math

import numpy as np
import jax
import jax.numpy as jnp
from jax import lax
from jax.experimental import pallas as pl
from jax.experimental.pallas import tpu as pltpu

F32 = jnp.float32
BF16 = jnp.bfloat16

GRID_W = 64
EPS = 1e-6
N_MOD = 9
FNET_GROUPS = 4
FNET_GROUP_DIM = 128
FNET_DIM = FNET_GROUPS * FNET_GROUP_DIM
NA_HEADS = 8
NA_HEAD_DIM = 64
NA_DIM = NA_HEADS * NA_HEAD_DIM
NA_KH = 8
NA_KW = 16
MLA_HEADS = 8
MLA_NOPE = 128
MLA_ROPE = 64
MLA_V = 128
MLA_Q_RANK = 384
MLA_KV_RANK = 128
ROPE_BASE = 10000.0

LANES = 128
VMEM_LIMIT_BYTES = 56 * 1024 * 1024
NEG_BIG = -1e30
LOG2_E = math.log2(math.e)

NA_ROWS_PER_BLOCK = 8
NA_WIN_ROWS = 16
NA_KBLK = 256
NA_PAIR_ROWS = NA_KH + 1


def _params(n_axes):
    return pltpu.CompilerParams(
        dimension_semantics=("arbitrary",) * n_axes,
        vmem_limit_bytes=VMEM_LIMIT_BYTES,
    )


def _resident(shape, lead=()):
    nd = len(shape) - len(lead)
    return pl.BlockSpec((None,) * len(lead) + tuple(shape[len(lead):]), lambda *_: tuple(lead) + (0,) * nd,
                        pipeline_mode=pl.Buffered(1))


def _dot(a, b):
    return jnp.dot(a, b, preferred_element_type=F32)


def _dot_nt(a, b):
    return lax.dot_general(a, b, (((1,), (1,)), ((), ())), preferred_element_type=F32)


def _rms(x):
    return x * lax.rsqrt(jnp.mean(x * x, axis=-1, keepdims=True) + EPS)


def _norm_mod(x, g, mod_ref, k0):
    shift = mod_ref[k0:k0 + 1, :]
    scale = mod_ref[k0 + 1:k0 + 2, :]
    return (_rms(x) * g) * (1.0 + scale) + shift


def _mod_kernel(c_ref, w_ref, b_ref, o_ref):
    c = c_ref[...]
    sc = c * jax.nn.sigmoid(c)
    w = w_ref[...]
    sc_hi, w_hi = sc.astype(BF16), w.astype(BF16)
    sc_lo = (sc - sc_hi.astype(F32)).astype(BF16)
    w_lo = (w - w_hi.astype(F32)).astype(BF16)
    r = sc.shape[0]
    a = _dot(jnp.concatenate([sc_hi, sc_lo], axis=0), w_hi)
    o_ref[...] = a[:r] + a[r:] + _dot(sc_hi, w_lo) + b_ref[...]


def _modulation(cvec, ada_w, ada_b):
    depth, d, n = ada_w.shape
    r = cvec.shape[0]
    tn = 1536
    assert n % tn == 0
    out = pl.pallas_call(
        _mod_kernel,
        grid=(depth, n // tn),
        in_specs=[
            pl.BlockSpec((r, d), lambda l, j: (0, 0)),
            pl.BlockSpec((None, d, tn), lambda l, j: (l, 0, j)),
            pl.BlockSpec((None, 1, tn), lambda l, j: (l, 0, j)),
        ],
        out_specs=pl.BlockSpec((None, r, tn), lambda l, j: (l, 0, j)),
        out_shape=jax.ShapeDtypeStruct((depth, r, n), F32),
        compiler_params=_params(2),
        name="adaln_modulation",
    )(cvec, ada_w, ada_b.reshape(depth, 1, n))
    return out.reshape(depth, r, N_MOD, d)


def _mod_spec(d, tm, rows_per_mod, mod_off):
    return pl.BlockSpec((None, N_MOD, d), lambda i: (mod_off + (i * tm) // rows_per_mod, 0, 0))


class _Section:
    def __init__(self, body, inputs=(), outputs=(), writes_x=False):
        self.body, self.inputs, self.outputs, self.writes_x = body, list(inputs), list(outputs), writes_x


def _stage_kernel(*refs, sections, writes_x):
    x_ref, mod_ref = refs[0], refs[1]
    n_in = sum(len(s.inputs) for s in sections)
    in_refs = refs[2:2 + n_in]
    out_refs = refs[2 + n_in:]
    x = x_ref[...]
    xo_ref = None
    if writes_x:
        xo_ref, out_refs = out_refs[0], out_refs[1:]
    i0 = o0 = 0
    for s in sections:
        x = s.body(x, mod_ref, in_refs[i0:i0 + len(s.inputs)], out_refs[o0:o0 + len(s.outputs)])
        i0 += len(s.inputs)
        o0 += len(s.outputs)
    if writes_x:
        xo_ref[...] = x


def _token_stage(x, mod, rows_per_mod, mod_off, sections, name, seq=None, tm=512):
    t, d = x.shape
    tm = min(tm, t)
    assert t % tm == 0 and rows_per_mod % tm == 0
    writes_x = any(s.writes_x for s in sections)
    in_specs = [pl.BlockSpec((tm, d), lambda i: (i, 0)), _mod_spec(d, tm, rows_per_mod, mod_off)]
    args = [x, mod]
    for s in sections:
        for arr, kind in s.inputs:
            if kind == "res":
                in_specs.append(_resident(arr.shape))
            elif isinstance(kind, tuple):
                in_specs.append(_resident(arr.shape, kind[1]))
            elif kind == "row":
                in_specs.append(pl.BlockSpec((tm, arr.shape[1]), lambda i: (i, 0)))
            elif kind == "head":
                in_specs.append(pl.BlockSpec((arr.shape[0], tm, arr.shape[2]), lambda i: (0, i, 0)))
            else:
                assert kind == "rope" and seq % tm == 0
                n_rt = seq // tm
                in_specs.append(pl.BlockSpec((tm, arr.shape[1]), lambda i: (i % n_rt, 0)))
            args.append(arr)
    out_specs, out_shape = [], []
    if writes_x:
        out_specs.append(pl.BlockSpec((tm, d), lambda i: (i, 0)))
        out_shape.append(jax.ShapeDtypeStruct((t, d), F32))
    for s in sections:
        for w, kind in s.outputs:
            if kind == "row":
                out_specs.append(pl.BlockSpec((tm, w), lambda i: (i, 0)))
                out_shape.append(jax.ShapeDtypeStruct((t, w), BF16))
            elif kind == "head":
                heads, hw = w
                out_specs.append(pl.BlockSpec((heads, tm, hw), lambda i: (0, i, 0)))
                out_shape.append(jax.ShapeDtypeStruct((heads, t, hw), BF16))
            else:
                out_specs.append(pl.BlockSpec((w, tm), lambda i: (0, i)))
                out_shape.append(jax.ShapeDtypeStruct((w, t), BF16))
    return pl.pallas_call(
        functools.partial(_stage_kernel, sections=sections, writes_x=writes_x),
        grid=(t // tm,),
        in_specs=in_specs,
        out_specs=out_specs,
        out_shape=out_shape,
        compiler_params=_params(1),
        name=name,
    )(*args)


FFN_CHUNKS = 2


def _sec_half_ffn(k0, g, wg, wu, wd, lead):
    def body(x, mod_ref, in_refs, out_refs):
        g_ref, wg_ref, wu_ref, wd_ref = in_refs
        hb = _norm_mod(x, g_ref[...], mod_ref, k0).astype(BF16)
        tf = wg_ref.shape[1] // FFN_CHUNKS
        y = None
        for j in range(FFN_CHUNKS):
            a = _dot(hb, wg_ref[:, j * tf:(j + 1) * tf])
            u = _dot(hb, wu_ref[:, j * tf:(j + 1) * tf])
            act = ((a * jax.nn.sigmoid(a)) * u).astype(BF16)
            yj = _dot(act, wd_ref[j * tf:(j + 1) * tf, :])
            y = yj if y is None else y + yj
        return x + (0.5 * mod_ref[k0 + 2:k0 + 3, :]) * y

    assert wg.shape[-1] % (FFN_CHUNKS * LANES) == 0
    at = ("res_at", tuple(lead))
    return _Section(body, [(g.reshape(1, -1), "res"), (wg, at), (wu, at), (wd, at)], writes_x=True)


def _sec_outproj(w, ys):
    def body(x, mod_ref, in_refs, out_refs):
        w_ref, *y_refs = in_refs
        acc, r0 = None, 0
        for y_ref in y_refs:
            if len(y_ref.shape) == 3:
                y = jnp.concatenate([y_ref[h] for h in range(y_ref.shape[0])], axis=1)
            else:
                y = y_ref[...]
            k = y.shape[1]
            part = _dot(y, w_ref[r0:r0 + k, :])
            acc = part if acc is None else acc + part
            r0 += k
        return x + mod_ref[5:6, :] * acc

    return _Section(body, [(w, "res")] + [(y, "head" if y.ndim == 3 else "row") for y in ys], writes_x=True)


def _sec_final_norm(g):
    def body(x, mod_ref, in_refs, out_refs):
        return _rms(x) * in_refs[0][...]

    return _Section(body, [(g.reshape(1, -1), "res")], writes_x=True)


def _proj_ab_body(x, mod_ref, in_refs, out_refs):
    g_ref, w_ref, wvt_ref, cc_ref, cs_ref = in_refs
    uc_ref, us_ref, q_ref, k_ref, vt_ref = out_refs
    hb = _norm_mod(x, g_ref[...], mod_ref, 3).astype(BF16)
    z = _dot(hb, w_ref[...])
    u = z[:, :FNET_DIM].astype(BF16)
    for g in range(FNET_GROUPS):
        sl = slice(g * FNET_GROUP_DIM, (g + 1) * FNET_GROUP_DIM)
        uc_ref[:, sl] = _dot(u[:, sl], cc_ref[...]).astype(BF16)
        us_ref[:, sl] = _dot(u[:, sl], cs_ref[...]).astype(BF16)
    for hp in range(NA_DIM // LANES):
        q0, k0 = FNET_DIM + hp * LANES, FNET_DIM + NA_DIM + hp * LANES
        q_ref[hp] = (z[:, q0:q0 + LANES] * (NA_HEAD_DIM ** -0.5 * LOG2_E)).astype(BF16)
        k_ref[hp] = z[:, k0:k0 + LANES].astype(BF16)
    vt_ref[...] = _dot_nt(wvt_ref[...], hb).astype(BF16)
    return x


def _dft_real_imag(n, scale):
    j = np.arange(n, dtype=np.int64)
    ang = 2.0 * np.pi * ((j[:, None] * j[None, :]) % n).astype(np.float64) / n
    return (np.cos(ang) * scale).astype(np.float32), (np.sin(ang) * scale).astype(np.float32)


def _sec_proj_ab(g, w_in):
    cc, cs = _dft_real_imag(FNET_GROUP_DIM, FNET_GROUP_DIM ** -0.5)
    n_row = FNET_DIM + 2 * NA_DIM
    w_row = w_in[:, :n_row].astype(BF16)
    w_vt = w_in[:, n_row:].T.astype(BF16)
    inputs = [(g.reshape(1, -1), "res"), (w_row, "res"), (w_vt, "res"),
              (jnp.asarray(cc).astype(BF16), "res"), (jnp.asarray(cs).astype(BF16), "res")]
    pairs = ((NA_DIM // LANES, LANES), "head")
    outputs = [(FNET_DIM, "row"), (FNET_DIM, "row"), pairs, pairs, (NA_DIM, "col")]
    return _Section(_proj_ab_body, inputs, outputs)


def _fnet_kernel(cn_ref, sn_ref, uc_ref, us_ref, o_ref):
    o = _dot(cn_ref[...], uc_ref[...]) - _dot(sn_ref[...], us_ref[...])
    o_ref[...] = o.astype(BF16)


def _position_dft_matrices(n):
    if n <= 512:
        c, s = _dft_real_imag(n, n ** -0.5)
        return jnp.asarray(c).astype(BF16), jnp.asarray(s).astype(BF16)
    n1 = GRID_W
    n2 = n // n1
    k = np.arange(n, dtype=np.int64)
    hi = np.arange(n2, dtype=np.int64)[:, None] * n1
    lo = np.arange(n1, dtype=np.int64)[:, None]
    ang_hi = 2.0 * np.pi * ((hi * k[None, :]) % n).astype(np.float64) / n
    ang_lo = 2.0 * np.pi * ((lo * k[None, :]) % n).astype(np.float64) / n
    ch, sh = jnp.asarray(np.cos(ang_hi), F32)[:, None, :], jnp.asarray(np.sin(ang_hi), F32)[:, None, :]
    cl, sl = jnp.asarray(np.cos(ang_lo), F32)[None, :, :], jnp.asarray(np.sin(ang_lo), F32)[None, :, :]
    scale = n ** -0.5
    c = ((ch * cl - sh * sl) * scale).reshape(n, n).astype(BF16)
    s = ((sh * cl + ch * sl) * scale).reshape(n, n).astype(BF16)
    return c, s


def _fnet_positions(uc, us, batch):
    t, w = uc.shape
    n = t // batch
    cn, sn = _position_dft_matrices(n)
    tn = min(512, n)
    nt = n // tn
    return pl.pallas_call(
        _fnet_kernel,
        grid=(nt, batch),
        in_specs=[
            pl.BlockSpec((tn, n), lambda i, b: (i, 0)),
            pl.BlockSpec((tn, n), lambda i, b: (i, 0)),
            pl.BlockSpec((n, w), lambda i, b: (b, 0)),
            pl.BlockSpec((n, w), lambda i, b: (b, 0)),
        ],
        out_specs=pl.BlockSpec((tn, w), lambda i, b: (b * nt + i, 0)),
        out_shape=jax.ShapeDtypeStruct((t, w), BF16),
        compiler_params=_params(2),
        name="fnet_positions",
    )(cn, sn, uc, us)


def _na_geometry(rows):
    n_blocks = rows // NA_ROWS_PER_BLOCK
    kh = min(NA_KH, rows)
    variants, vi, start_blk = [], [], []
    for blk in range(n_blocks):
        start = int(np.clip(NA_ROWS_PER_BLOCK * blk - kh // 2, 0, rows - NA_WIN_ROWS))
        table = []
        for a in range(NA_ROWS_PER_BLOCK):
            r = NA_ROWS_PER_BLOCK * blk + a
            r0 = int(np.clip(r - kh // 2, 0, rows - kh))
            table.append(tuple((start + i) - r + (NA_KH - 1) if 0 <= (start + i) - r0 < kh else -1
                               for i in range(NA_WIN_ROWS)))
            assert sum(e >= 0 for e in table[-1]) == kh
        table = tuple(table)
        if table not in variants:
            variants.append(table)
        vi.append(variants.index(table))
        start_blk.append(start * GRID_W // NA_KBLK)
    return vi, start_blk, variants


def _na_pair_windows(variants):
    lows = []
    for table in variants:
        for j in range(NA_ROWS_PER_BLOCK // 2):
            valid = [i for i in range(NA_WIN_ROWS) if table[2 * j][i] >= 0 or table[2 * j + 1][i] >= 0]
            lo = min(min(valid), NA_WIN_ROWS - NA_PAIR_ROWS)
            assert max(valid) < lo + NA_PAIR_ROWS
            lows.append(lo)
    return lows


def _na_bias_kernel(rpb_ref, o_ref, *, variants, pair_lows):
    v = pl.program_id(0)
    h = pl.program_id(1)
    n_ro = 2 * NA_KH - 1
    n_co = 2 * NA_KW - 1
    kc = lax.broadcasted_iota(jnp.int32, (GRID_W, LANES), 0)
    lane = lax.broadcasted_iota(jnp.int32, (GRID_W, LANES), 1)
    c = lane & (GRID_W - 1)
    co = kc - c + (NA_KW - 1)
    c0 = jnp.clip(c - NA_KW // 2, 0, GRID_W - NA_KW)
    win = (kc >= c0) & (kc < c0 + NA_KW)
    neg = jnp.full((GRID_W, LANES), NEG_BIG, F32)
    tiles = []
    for ro in range(n_ro):
        t = neg
        for tt in range(n_co):
            t = jnp.where(win & (co == tt), rpb_ref[(h * n_ro + ro) * n_co + tt] * LOG2_E, t)
        tiles.append(t)
    left = lane < GRID_W
    for vv, table in enumerate(variants):
        @pl.when(v == vv)
        def _(vv=vv, table=table):
            for j in range(NA_ROWS_PER_BLOCK // 2):
                lo = pair_lows[vv * (NA_ROWS_PER_BLOCK // 2) + j]
                for ii in range(NA_PAIR_ROWS):
                    r1, r2 = table[2 * j][lo + ii], table[2 * j + 1][lo + ii]
                    t1 = tiles[r1] if r1 >= 0 else neg
                    t2 = tiles[r2] if r2 >= 0 else neg
                    o_ref[j, ii * GRID_W:(ii + 1) * GRID_W, :] = jnp.where(left, t1, t2)


def _na_bias_tables(rpb, variants, pair_lows):
    heads = rpb.shape[0]
    n_pairs = NA_ROWS_PER_BLOCK // 2
    nk = NA_PAIR_ROWS * GRID_W
    return pl.pallas_call(
        functools.partial(_na_bias_kernel, variants=variants, pair_lows=pair_lows),
        grid=(len(variants), heads),
        in_specs=[pl.BlockSpec(memory_space=pltpu.SMEM)],
        out_specs=pl.BlockSpec((None, None, n_pairs, nk, LANES), lambda v, h: (v, h, 0, 0, 0)),
        out_shape=jax.ShapeDtypeStruct((len(variants), heads, n_pairs, nk, LANES), F32),
        compiler_params=_params(2),
        name="na_bias_tables",
    )(rpb.reshape(-1))


def _pair_masks():
    lane = lax.broadcasted_iota(jnp.int32, (1, LANES), 1)
    return [(lane // NA_HEAD_DIM == hh).astype(BF16) for hh in range(LANES // NA_HEAD_DIM)]


def _na_body(q_ref, k_ref, vt_ref, kc_ref, vc_ref, bias_ref, o_ref, lows):
    span = NA_PAIR_ROWS * GRID_W
    nk = NA_WIN_ROWS * GRID_W
    q = q_ref[...]
    masks = _pair_masks()

    def scores(hh):
        qh = q * masks[hh]
        st = _dot_nt(k_ref[...], qh)
        return st, _dot_nt(kc_ref[...], qh)

    outs = []
    st, sc = scores(0)
    for hh in range(len(masks)):
        nxt = scores(hh + 1) if hh + 1 < len(masks) else None
        ps, pcs, ls = [], [], []
        for j, lo in enumerate(lows):
            lanes = slice(j * LANES, (j + 1) * LANES)
            r0 = lo * GRID_W
            sj = st[r0:r0 + span, lanes] + bias_ref[hh, j]
            scj = sc[:, lanes]
            m = jnp.maximum(jnp.max(sj, axis=0, keepdims=True), jnp.max(scj, axis=0, keepdims=True))
            pj = jnp.exp2(sj - m)
            pcj = jnp.exp2(scj - m)
            ls.append(jnp.sum(pj, axis=0, keepdims=True) + jnp.sum(pcj, axis=0, keepdims=True))
            pieces = [jnp.zeros((r0, LANES), BF16), pj.astype(BF16), jnp.zeros((nk - r0 - span, LANES), BF16)]
            ps.append(jnp.concatenate([x for x in pieces if x.shape[0]], axis=0))
            pcs.append(pcj.astype(BF16))
        p = jnp.concatenate(ps, axis=1)
        pc = jnp.concatenate(pcs, axis=1)
        rows = slice(hh * NA_HEAD_DIM, (hh + 1) * NA_HEAD_DIM)
        acc = _dot(vt_ref[rows, :], p) + _dot(vc_ref[rows, :], pc)
        outs.append(acc / jnp.concatenate(ls, axis=1))
        if nxt is not None:
            st, sc = nxt
    o_ref[...] = jnp.transpose(jnp.concatenate(outs, axis=0)).astype(BF16)


def _na_kernel(vi_ref, sb_ref, q_ref, k_ref, vt_ref, kc_ref, vc_ref, bias_ref, o_ref, *, pair_lows):
    del sb_ref
    variant = vi_ref[pl.program_id(1)]
    n_lp = NA_ROWS_PER_BLOCK // 2
    for vv in range(len(pair_lows) // n_lp):
        pl.when(variant == vv)(functools.partial(
            _na_body, q_ref, k_ref, vt_ref, kc_ref, vc_ref, bias_ref, o_ref, pair_lows[vv * n_lp:(vv + 1) * n_lp]))


def _neighbourhood_attention(q, k, vt, kc, vct, bias, vi, start_blk, pair_lows, batch):
    n_pairs, t, _ = q.shape
    s = t // batch
    rows = s // GRID_W
    nq = NA_ROWS_PER_BLOCK * GRID_W
    nk = NA_WIN_ROWS * GRID_W
    n_blocks = rows // NA_ROWS_PER_BLOCK
    lc = kc.shape[1] // batch
    kblk_per_batch = s // NA_KBLK
    heads_per_pair = LANES // NA_HEAD_DIM
    bias6 = bias.reshape(bias.shape[0], n_pairs, heads_per_pair, *bias.shape[2:])

    def win_start(b, i, sb_r):
        return (b * kblk_per_batch + sb_r[i]) * NA_KBLK

    grid_spec = pltpu.PrefetchScalarGridSpec(
        num_scalar_prefetch=2,
        grid=(n_pairs, n_blocks, batch),
        in_specs=[pl.BlockSpec((None, nq, LANES), lambda hp, i, b, vi_r, sb_r: (hp, b * n_blocks + i, 0)),
                  pl.BlockSpec((None, pl.Element(nk), pl.Element(LANES)),
                               lambda hp, i, b, vi_r, sb_r: (hp, win_start(b, i, sb_r), 0)),
                  pl.BlockSpec((pl.Element(LANES), pl.Element(nk)),
                               lambda hp, i, b, vi_r, sb_r: (hp * LANES, win_start(b, i, sb_r)))]
        + [pl.BlockSpec((None, lc, LANES), lambda hp, i, b, vi_r, sb_r: (hp, b, 0)),
           pl.BlockSpec((LANES, lc), lambda hp, i, b, vi_r, sb_r: (hp, b)),
           pl.BlockSpec((None, None, heads_per_pair, *bias.shape[2:]),
                        lambda hp, i, b, vi_r, sb_r: (vi_r[i], hp, 0, 0, 0, 0))],
        out_specs=pl.BlockSpec((None, nq, LANES), lambda hp, i, b, vi_r, sb_r: (hp, b * n_blocks + i, 0)),
    )
    return pl.pallas_call(
        functools.partial(_na_kernel, pair_lows=tuple(pair_lows)),
        grid_spec=grid_spec,
        out_shape=jax.ShapeDtypeStruct((n_pairs, t, LANES), BF16),
        compiler_params=_params(3),
        name="neighbourhood_attention",
    )(jnp.asarray(vi, jnp.int32), jnp.asarray(start_blk, jnp.int32),
      q, k, vt, kc, vct, bias6)


def _ctx_attn_kernel(q_ref, k_ref, vt_ref, o_ref):
    q = q_ref[...]
    k = k_ref[...]
    outs = []
    for hh, hm in enumerate(_pair_masks()):
        st = _dot_nt(k, q * hm)
        m = jnp.max(st, axis=0, keepdims=True)
        p = jnp.exp2(st - m)
        l = jnp.sum(p, axis=0, keepdims=True)
        outs.append(_dot(vt_ref[hh * NA_HEAD_DIM:(hh + 1) * NA_HEAD_DIM, :], p.astype(BF16)) / l)
    o_ref[...] = jnp.transpose(jnp.concatenate(outs, axis=0)).astype(BF16)


def _ctx_dense_attention(q, k, vt, batch):
    n_pairs, t, _ = q.shape
    lc = t // batch
    spec = pl.BlockSpec((None, lc, LANES), lambda b, hp: (hp, b, 0))
    return pl.pallas_call(
        _ctx_attn_kernel,
        grid=(batch, n_pairs),
        in_specs=[spec, spec, pl.BlockSpec((LANES, lc), lambda b, hp: (hp, b))],
        out_specs=spec,
        out_shape=jax.ShapeDtypeStruct((n_pairs, t, LANES), BF16),
        compiler_params=_params(2),
        name="ctx_dense_attention",
    )(q, k, vt)


MLA_QK_PAD = 256
MLA_LOOKAHEAD = 1


def _rope(x, cos, sin_next, sin_prev):
    quarter = MLA_ROPE // 4
    return (x * cos + pltpu.roll(x, LANES - quarter, 1) * sin_next + pltpu.roll(x, quarter, 1) * sin_prev)


def _proj_mla_body(x, mod_ref, in_refs, out_refs, *, rope, need_q):
    g_ref, win_ref, gq_ref, gkv_ref, wqn_ref, wqr_ref, wuk_ref, wuv_ref, *rope_refs = in_refs
    if rope:
        cos_ref, sa_ref, sb_ref = rope_refs
    if need_q:
        q_ref, k_ref, v_ref = out_refs
    else:
        k_ref, v_ref = out_refs
    hb = _norm_mod(x, g_ref[...], mod_ref, 3).astype(BF16)
    z = _dot(hb, win_ref[...])
    ckv = (_rms(z[:, MLA_Q_RANK:MLA_Q_RANK + MLA_KV_RANK]) * gkv_ref[...]).astype(BF16)
    kr = z[:, MLA_Q_RANK + MLA_KV_RANK:]
    if rope:
        cos, sa, sb = cos_ref[...], sa_ref[...], sb_ref[...]
        kr = _rope(kr, cos, sa, sb)
    kr = kr.astype(BF16)
    kn = _dot(ckv, wuk_ref[...])
    v_ref[...] = _dot_nt(wuv_ref[...], ckv).astype(BF16)
    for h in range(MLA_HEADS):
        k_ref[h, :, :LANES] = kn[:, h * LANES:(h + 1) * LANES].astype(BF16)
        k_ref[h, :, LANES:] = kr
    if need_q:
        cq = (_rms(z[:, :MLA_Q_RANK]) * gq_ref[...]).astype(BF16)
        scale = (MLA_NOPE + MLA_ROPE) ** -0.5 * LOG2_E
        qn = _dot(cq, wqn_ref[...])
        qr = _dot(cq, wqr_ref[...])
        for h in range(MLA_HEADS):
            qrh = qr[:, h * LANES:(h + 1) * LANES]
            if rope:
                qrh = _rope(qrh, cos, sa, sb)
            q_ref[h, :, :LANES] = (qn[:, h * LANES:(h + 1) * LANES] * scale).astype(BF16)
            q_ref[h, :, LANES:] = (qrh * scale).astype(BF16)
    return x


def _rope_tables(n_tok):
    pos = jnp.arange(n_tok)
    row = (pos // GRID_W).astype(F32)
    col = (pos % GRID_W).astype(F32)
    half = MLA_ROPE // 2
    inv = ROPE_BASE ** (-jnp.arange(0, half, 2, dtype=F32) / half)
    ang_r = row[:, None] * inv[None]
    ang_c = col[:, None] * inv[None]
    ang = jnp.concatenate([ang_r, ang_r, ang_c, ang_c], axis=-1)
    cos, sin = jnp.cos(ang), jnp.sin(ang)
    quarter = MLA_ROPE // 4
    even_q = ((jnp.arange(MLA_ROPE) // quarter) % 2 == 0)[None, :]
    pad = LANES - MLA_ROPE
    cos_p = jnp.pad(cos, ((0, 0), (0, pad)), constant_values=1.0)
    sin_next = jnp.pad(jnp.where(even_q, -sin, 0.0), ((0, 0), (0, pad)))
    sin_prev = jnp.pad(jnp.where(even_q, 0.0, sin), ((0, 0), (0, pad)))
    return cos_p, sin_next, sin_prev


def _sec_proj_mla(g, w, rope_tabs, need_q):
    rope = rope_tabs is not None
    inputs = [(g.reshape(1, -1), "res")] + [(a, "res") for a in w]
    if rope:
        inputs += [(tab, "rope") for tab in rope_tabs]
    qk = ((MLA_HEADS, MLA_QK_PAD), "head")
    outputs = ([qk] if need_q else []) + [qk, (MLA_HEADS * MLA_V, "col")]
    return _Section(functools.partial(_proj_mla_body, rope=rope, need_q=need_q), inputs, outputs)


def _mla_attn_kernel(q_ref, *rest, n_src, chunk):
    k_refs, vt_refs, o_ref = rest[:n_src], rest[n_src:2 * n_src], rest[2 * n_src]
    q = q_ref[...]
    chunks = [(k_ref, vt_ref, c0, min(k_ref.shape[0], c0 + chunk))
              for k_ref, vt_ref in zip(k_refs, vt_refs) for c0 in range(0, k_ref.shape[0], chunk)]

    def scores(j):
        k_ref, _, c0, c1 = chunks[j]
        return _dot_nt(k_ref[c0:c1, :], q)

    m = l = acc = None
    ahead =[scores(j) for j in range(min(MLA_LOOKAHEAD, len(chunks)))]
    for j, (_, vt_ref, c0, c1) in enumerate(chunks):
        if j + MLA_LOOKAHEAD < len(chunks):
            ahead.append(scores(j + MLA_LOOKAHEAD))
        st = ahead.pop(0)
        mj = jnp.max(st, axis=0, keepdims=True)
        if m is None:
            m_new = mj
        else:
            m_new = jnp.maximum(m, mj)
            alpha = jnp.exp2(m - m_new)
        p = jnp.exp2(st - m_new)
        lj = jnp.sum(p, axis=0, keepdims=True)
        oj = _dot(vt_ref[:, c0:c1], p.astype(BF16))
        if m is None:
            l, acc = lj, oj
        else:
            l, acc = alpha * l + lj, alpha * acc + oj
        m = m_new
    o_ref[...] = jnp.transpose(acc / l).astype(BF16)


def _mla_attention(q, sources, batch, tq=1024, chunk=1024):
    t = q.shape[1]
    s = t // batch
    tq = min(tq, s)
    nq = s // tq
    k_specs, v_specs = [], []
    for k, _ in sources:
        n = k.shape[1] // batch
        k_specs.append(pl.BlockSpec((None, n, MLA_QK_PAD), lambda b, h, i: (h, b, 0)))
        v_specs.append(pl.BlockSpec((MLA_V, n), lambda b, h, i: (h, b)))
    return pl.pallas_call(
        functools.partial(_mla_attn_kernel, n_src=len(sources), chunk=chunk),
        grid=(batch, MLA_HEADS, nq),
        in_specs=[pl.BlockSpec((None, tq, MLA_QK_PAD), lambda b, h, i: (h, b * nq + i, 0))] + k_specs + v_specs,
        out_specs=pl.BlockSpec((None, tq, MLA_V), lambda b, h, i: (h, b * nq + i, 0)),
        out_shape=jax.ShapeDtypeStruct((MLA_HEADS, t, MLA_V), BF16),
        compiler_params=_params(3),
        name="mla_attention",
    )(q, *[k for k, _ in sources], *[v for _, v in sources])


def _mla_weights(w_in, g_q, g_kv, w_uq, w_uk, w_uv):
    w_in_p = jnp.pad(w_in, ((0, 0), (0, 5 * LANES - w_in.shape[1]))).astype(BF16)
    wq = w_uq.reshape(MLA_Q_RANK, MLA_HEADS, MLA_NOPE + MLA_ROPE)
    wqn = wq[:, :, :MLA_NOPE].reshape(MLA_Q_RANK, MLA_HEADS * MLA_NOPE).astype(BF16)
    wqr = jnp.pad(wq[:, :, MLA_NOPE:], ((0, 0), (0, 0), (0, LANES - MLA_ROPE)))
    wqr = wqr.reshape(MLA_Q_RANK, MLA_HEADS * LANES).astype(BF16)
    return (w_in_p, g_q.reshape(1, -1), g_kv.reshape(1, -1), wqn, wqr, w_uk.astype(BF16), w_uv.T.astype(BF16))


def kernel(x, c, ctx, c_ctx, ada_w, ada_b, norm_g, ffn_w_gate, ffn_w_up, ffn_w_down, ab_w_in, ab_rpb, ab_w_out,
           mla_w_in, mla_g_q, mla_g_kv, mla_w_uq, mla_w_uk, mla_w_uv, mla_w_o, final_g):
    batch, seq, d = x.shape
    lc = ctx.shape[1]
    depth = ada_w.shape[0]
    assert seq % (GRID_W * NA_WIN_ROWS) == 0 and lc % LANES == 0
    t_lat, t_ctx = batch * seq, batch * lc

    n_rows = -(-(batch + 1) // 8) * 8
    cvec = jnp.concatenate([c, c_ctx[None, :], jnp.zeros((n_rows - batch - 1, d), F32)], axis=0)
    mod = _modulation(cvec, ada_w, ada_b)

    x_l = x.reshape(t_lat, d)
    x_c = ctx.reshape(t_ctx, d)
    lat = dict(rows_per_mod=seq, mod_off=0)
    cx = dict(rows_per_mod=t_ctx, mod_off=batch)

    wg, wu, wd = ffn_w_gate.astype(BF16), ffn_w_up.astype(BF16), ffn_w_down.astype(BF16)
    rope_tabs = None

    for layer in range(depth):
        last = layer == depth - 1
        m = mod[layer]
        g = norm_g[layer]
        ffn1 = _sec_half_ffn(0, g[0], wg, wu, wd, (layer, 0))
        ffn2 = _sec_half_ffn(6, g[2], wg, wu, wd, (layer, 1))
        i = layer // 2
        if layer % 2 == 0:
            proj = _sec_proj_ab(g[1], ab_w_in[i])
            w_out = ab_w_out[i].astype(BF16)
            x_l, uc_l, us_l, q_l, k_l, vt_l = _token_stage(x_l, m, sections=[ffn1, proj], name="ffn_proj_ab", **lat)
            x_c, uc_c, us_c, q_c, k_c, vt_c = _token_stage(x_c, m, sections=[ffn1, proj], name="ffn_proj_ab", **cx)
            a_l = _fnet_positions(uc_l, us_l, batch)
            vi, start_blk, variants = _na_geometry(seq // GRID_W)
            pair_lows = _na_pair_windows(variants)
            bias = _na_bias_tables(ab_rpb[i], variants, pair_lows)
            b_l = _neighbourhood_attention(q_l, k_l, vt_l, k_c, vt_c, bias, vi, start_blk, pair_lows, batch)
            ys_l = [a_l, b_l]
            if not last:
                a_c = _fnet_positions(uc_c, us_c, batch)
                b_c = _ctx_dense_attention(q_c, k_c, vt_c, batch)
                ys_c = [a_c, b_c]
        else:
            if rope_tabs is None:
                rope_tabs = _rope_tables(seq)
            w = _mla_weights(mla_w_in[i], mla_g_q[i], mla_g_kv[i], mla_w_uq[i], mla_w_uk[i], mla_w_uv[i])
            w_out = mla_w_o[i].astype(BF16)
            x_l, q_l, k_l, vt_l = _token_stage(
                x_l, m, sections=[ffn1, _sec_proj_mla(g[1], w, rope_tabs, True)], name="ffn_proj_mla", seq=seq, **lat)
            x_c, *q_c, k_c, vt_c = _token_stage(
                x_c, m, sections=[ffn1, _sec_proj_mla(g[1], w, None, not last)], name="ffn_proj_mla", **cx)
            ys_l = [_mla_attention(q_l, [(k_l, vt_l), (k_c, vt_c)], batch)]
            if not last:
                ys_c = [_mla_attention(q_c[0], [(k_c, vt_c)], batch)]
        tail = [_sec_final_norm(final_g)] if last else []
        (x_l,) = _token_stage(x_l, m, sections=[_sec_outproj(w_out, ys_l), ffn2] + tail, name="outproj_ffn", **lat)
        if not last:
            (x_c,) = _token_stage(x_c, m, sections=[_sec_outproj(w_out, ys_c), ffn2], name="outproj_ffn", **cx)
    return x_l.reshape(batch, seq, d)
```

```python
import functools
import math

import numpy as np
import jax
import jax.numpy as jnp
from jax import lax
from jax.experimental import pallas as pl
from jax.experimental.pallas import tpu as pltpu

F32 = jnp.float32
BF16 = jnp.bfloat16

GRID_W = 64
EPS = 1e-6
N_MOD = 9
FNET_GROUPS = 4
FNET_GROUP_DIM = 128
FNET_DIM = FNET_GROUPS * FNET_GROUP_DIM
NA_HEADS = 8
NA_HEAD_DIM = 64
NA_DIM = NA_HEADS * NA_HEAD_DIM
NA_KH = 8
NA_KW = 16
MLA_HEADS = 8
MLA_NOPE = 128
MLA_ROPE = 64
MLA_V = 128
MLA_Q_RANK = 384
MLA_KV_RANK = 128
ROPE_BASE = 10000.0

LANES = 128
VMEM_LIMIT_BYTES = 56 * 1024 * 1024
NEG_BIG = -1e30
LOG2_E = math.log2(math.e)

NA_ROWS_PER_BLOCK = 8
NA_WIN_ROWS = 16
NA_KBLK = 256
NA_PAIR_ROWS = NA_KH + 1


def _params(n_axes):
    return pltpu.CompilerParams(
        dimension_semantics=("arbitrary",) * n_axes,
        vmem_limit_bytes=VMEM_LIMIT_BYTES,
    )


def _resident(shape, lead=()):
    nd = len(shape) - len(lead)
    return pl.BlockSpec((None,) * len(lead) + tuple(shape[len(lead):]), lambda *_: tuple(lead) + (0,) * nd,
                        pipeline_mode=pl.Buffered(1))


def _dot(a, b):
    return jnp.dot(a, b, preferred_element_type=F32)


def _dot_nt(a, b):
    return lax.dot_general(a, b, (((1,), (1,)), ((), ())), preferred_element_type=F32)


def _rms(x):
    return x * lax.rsqrt(jnp.mean(x * x, axis=-1, keepdims=True) + EPS)


def _norm_mod(x, g, mod_ref, k0):
    shift = mod_ref[k0:k0 + 1, :]
    scale = mod_ref[k0 + 1:k0 + 2, :]
    return (_rms(x) * g) * (1.0 + scale) + shift


def _mod_kernel(c_ref, w_ref, b_ref, o_ref):
    c = c_ref[...]
    sc = c * jax.nn.sigmoid(c)
    w = w_ref[...]
    sc_hi, w_hi = sc.astype(BF16), w.astype(BF16)
    sc_lo = (sc - sc_hi.astype(F32)).astype(BF16)
    w_lo = (w - w_hi.astype(F32)).astype(BF16)
    r = sc.shape[0]
    a = _dot(jnp.concatenate([sc_hi, sc_lo], axis=0), w_hi)
    o_ref[...] = a[:r] + a[r:] + _dot(sc_hi, w_lo) + b_ref[...]


def _modulation(cvec, ada_w, ada_b):
    depth, d, n = ada_w.shape
    r = cvec.shape[0]
    tn = 1536
    assert n % tn == 0
    out = pl.pallas_call(
        _mod_kernel,
        grid=(depth, n // tn),
        in_specs=[
            pl.BlockSpec((r, d), lambda l, j: (0, 0)),
            pl.BlockSpec((None, d, tn), lambda l, j: (l, 0, j)),
            pl.BlockSpec((None, 1, tn), lambda l, j: (l, 0, j)),
        ],
        out_specs=pl.BlockSpec((None, r, tn), lambda l, j: (l, 0, j)),
        out_shape=jax.ShapeDtypeStruct((depth, r, n), F32),
        compiler_params=_params(2),
        name="adaln_modulation",
    )(cvec, ada_w, ada_b.reshape(depth, 1, n))
    return out.reshape(depth, r, N_MOD, d)


def _mod_spec(d, tm, rows_per_mod, mod_off):
    return pl.BlockSpec((None, N_MOD, d), lambda i: (mod_off + (i * tm) // rows_per_mod, 0, 0))


class _Section:
    def __init__(self, body, inputs=(), outputs=(), writes_x=False):
        self.body, self.inputs, self.outputs, self.writes_x = body, list(inputs), list(outputs), writes_x


def _stage_kernel(*refs, sections, writes_x):
    x_ref, mod_ref = refs[0], refs[1]
    n_in = sum(len(s.inputs) for s in sections)
    in_refs = refs[2:2 + n_in]
    out_refs = refs[2 + n_in:]
    x = x_ref[...]
    xo_ref = None
    if writes_x:
        xo_ref, out_refs = out_refs[0], out_refs[1:]
    i0 = o0 = 0
    for s in sections:
        x = s.body(x, mod_ref, in_refs[i0:i0 + len(s.inputs)], out_refs[o0:o0 + len(s.outputs)])
        i0 += len(s.inputs)
        o0 += len(s.outputs)
    if writes_x:
        xo_ref[...] = x


def _token_stage(x, mod, rows_per_mod, mod_off, sections, name, seq=None, tm=512):
    t, d = x.shape
    tm = min(tm, t)
    assert t % tm == 0 and rows_per_mod % tm == 0
    writes_x = any(s.writes_x for s in sections)
    in_specs = [pl.BlockSpec((tm, d), lambda i: (i, 0)), _mod_spec(d, tm, rows_per_mod, mod_off)]
    args = [x, mod]
    for s in sections:
        for arr, kind in s.inputs:
            if kind == "res":
                in_specs.append(_resident(arr.shape))
            elif isinstance(kind, tuple):
                in_specs.append(_resident(arr.shape, kind[1]))
            elif kind == "row":
                in_specs.append(pl.BlockSpec((tm, arr.shape[1]), lambda i: (i, 0)))
            elif kind == "head":
                in_specs.append(pl.BlockSpec((arr.shape[0], tm, arr.shape[2]), lambda i: (0, i, 0)))
            else:
                assert kind == "rope" and seq % tm == 0
                n_rt = seq // tm
                in_specs.append(pl.BlockSpec((tm, arr.shape[1]), lambda i: (i % n_rt, 0)))
            args.append(arr)
    out_specs, out_shape = [], []
    if writes_x:
        out_specs.append(pl.BlockSpec((tm, d), lambda i: (i, 0)))
        out_shape.append(jax.ShapeDtypeStruct((t, d), F32))
    for s in sections:
        for w, kind in s.outputs:
            if kind == "row":
                out_specs.append(pl.BlockSpec((tm, w), lambda i: (i, 0)))
                out_shape.append(jax.ShapeDtypeStruct((t, w), BF16))
            elif kind == "head":
                heads, hw = w
                out_specs.append(pl.BlockSpec((heads, tm, hw), lambda i: (0, i, 0)))
                out_shape.append(jax.ShapeDtypeStruct((heads, t, hw), BF16))
            else:
                out_specs.append(pl.BlockSpec((w, tm), lambda i: (0, i)))
                out_shape.append(jax.ShapeDtypeStruct((w, t), BF16))
    return pl.pallas_call(
        functools.partial(_stage_kernel, sections=sections, writes_x=writes_x),
        grid=(t // tm,),
        in_specs=in_specs,
        out_specs=out_specs,
        out_shape=out_shape,
        compiler_params=_params(1),
        name=name,
    )(*args)


FFN_CHUNKS = 2


def _sec_half_ffn(k0, g, wg, wu, wd, lead):
    def body(x, mod_ref, in_refs, out_refs):
        g_ref, wg_ref, wu_ref, wd_ref = in_refs
        hb = _norm_mod(x, g_ref[...], mod_ref, k0).astype(BF16)
        tf = wg_ref.shape[1] // FFN_CHUNKS
        y = None
        for j in range(FFN_CHUNKS):
            a = _dot(hb, wg_ref[:, j * tf:(j + 1) * tf])
            u = _dot(hb, wu_ref[:, j * tf:(j + 1) * tf])
            act = ((a * jax.nn.sigmoid(a)) * u).astype(BF16)
            yj = _dot(act, wd_ref[j * tf:(j + 1) * tf, :])
            y = yj if y is None else y + yj
        return x + (0.5 * mod_ref[k0 + 2:k0 + 3, :]) * y

    assert wg.shape[-1] % (FFN_CHUNKS * LANES) == 0
    at = ("res_at", tuple(lead))
    return _Section(body, [(g.reshape(1, -1), "res"), (wg, at), (wu, at), (wd, at)], writes_x=True)


def _sec_outproj(w, ys):
    def body(x, mod_ref, in_refs, out_refs):
        w_ref, *y_refs = in_refs
        acc, r0 = None, 0
        for y_ref in y_refs:
            if len(y_ref.shape) == 3:
                y = jnp.concatenate([y_ref[h] for h in range(y_ref.shape[0])], axis=1)
            else:
                y = y_ref[...]
            k = y.shape[1]
            part = _dot(y, w_ref[r0:r0 + k, :])
            acc = part if acc is None else acc + part
            r0 += k
        return x + mod_ref[5:6, :] * acc

    return _Section(body, [(w, "res")] + [(y, "head" if y.ndim == 3 else "row") for y in ys], writes_x=True)


def _sec_final_norm(g):
    def body(x, mod_ref, in_refs, out_refs):
        return _rms(x) * in_refs[0][...]

    return _Section(body, [(g.reshape(1, -1), "res")], writes_x=True)


def _proj_ab_body(x, mod_ref, in_refs, out_refs):
    g_ref, w_ref, wvt_ref, cc_ref, cs_ref = in_refs
    uc_ref, us_ref, q_ref, k_ref, vt_ref = out_refs
    hb = _norm_mod(x, g_ref[...], mod_ref, 3).astype(BF16)
    z = _dot(hb, w_ref[...])
    u = z[:, :FNET_DIM].astype(BF16)
    for g in range(FNET_GROUPS):
        sl = slice(g * FNET_GROUP_DIM, (g + 1) * FNET_GROUP_DIM)
        uc_ref[:, sl] = _dot(u[:, sl], cc_ref[...]).astype(BF16)
        us_ref[:, sl] = _dot(u[:, sl], cs_ref[...]).astype(BF16)
    for hp in range(NA_DIM // LANES):
        q0, k0 = FNET_DIM + hp * LANES, FNET_DIM + NA_DIM + hp * LANES
        q_ref[hp] = (z[:, q0:q0 + LANES] * (NA_HEAD_DIM ** -0.5 * LOG2_E)).astype(BF16)
        k_ref[hp] = z[:, k0:k0 + LANES].astype(BF16)
    vt_ref[...] = _dot_nt(wvt_ref[...], hb).astype(BF16)
    return x


def _dft_real_imag(n, scale):
    j = np.arange(n, dtype=np.int64)
    ang = 2.0 * np.pi * ((j[:, None] * j[None, :]) % n).astype(np.float64) / n
    return (np.cos(ang) * scale).astype(np.float32), (np.sin(ang) * scale).astype(np.float32)


def _sec_proj_ab(g, w_in):
    cc, cs = _dft_real_imag(FNET_GROUP_DIM, FNET_GROUP_DIM ** -0.5)
    n_row = FNET_DIM + 2 * NA_DIM
    w_row = w_in[:, :n_row].astype(BF16)
    w_vt = w_in[:, n_row:].T.astype(BF16)
    inputs = [(g.reshape(1, -1), "res"), (w_row, "res"), (w_vt, "res"),
              (jnp.asarray(cc).astype(BF16), "res"), (jnp.asarray(cs).astype(BF16), "res")]
    pairs = ((NA_DIM // LANES, LANES), "head")
    outputs = [(FNET_DIM, "row"), (FNET_DIM, "row"), pairs, pairs, (NA_DIM, "col")]
    return _Section(_proj_ab_body, inputs, outputs)


def _fnet_kernel(cn_ref, sn_ref, uc_ref, us_ref, o_ref):
    o = _dot(cn_ref[...], uc_ref[...]) - _dot(sn_ref[...], us_ref[...])
    o_ref[...] = o.astype(BF16)


def _position_dft_matrices(n):
    if n <= 512:
        c, s = _dft_real_imag(n, n ** -0.5)
        return jnp.asarray(c).astype(BF16), jnp.asarray(s).astype(BF16)
    n1 = GRID_W
    n2 = n // n1
    k = np.arange(n, dtype=np.int64)
    hi = np.arange(n2, dtype=np.int64)[:, None] * n1
    lo = np.arange(n1, dtype=np.int64)[:, None]
    ang_hi = 2.0 * np.pi * ((hi * k[None, :]) % n).astype(np.float64) / n
    ang_lo = 2.0 * np.pi * ((lo * k[None, :]) % n).astype(np.float64) / n
    ch, sh = jnp.asarray(np.cos(ang_hi), F32)[:, None, :], jnp.asarray(np.sin(ang_hi), F32)[:, None, :]
    cl, sl = jnp.asarray(np.cos(ang_lo), F32)[None, :, :], jnp.asarray(np.sin(ang_lo), F32)[None, :, :]
    scale = n ** -0.5
    c = ((ch * cl - sh * sl) * scale).reshape(n, n).astype(BF16)
    s = ((sh * cl + ch * sl) * scale).reshape(n, n).astype(BF16)
    return c, s


def _fnet_positions(uc, us, batch):
    t, w = uc.shape
    n = t // batch
    cn, sn = _position_dft_matrices(n)
    tn = min(512, n)
    nt = n // tn
    return pl.pallas_call(
        _fnet_kernel,
        grid=(nt, batch),
        in_specs=[
            pl.BlockSpec((tn, n), lambda i, b: (i, 0)),
            pl.BlockSpec((tn, n), lambda i, b: (i, 0)),
            pl.BlockSpec((n, w), lambda i, b: (b, 0)),
            pl.BlockSpec((n, w), lambda i, b: (b, 0)),
        ],
        out_specs=pl.BlockSpec((tn, w), lambda i, b: (b * nt + i, 0)),
        out_shape=jax.ShapeDtypeStruct((t, w), BF16),
        compiler_params=_params(2),
        name="fnet_positions",
    )(cn, sn, uc, us)


def _na_geometry(rows):
    n_blocks = rows // NA_ROWS_PER_BLOCK
    kh = min(NA_KH, rows)
    variants, vi, start_blk = [], [], []
    for blk in range(n_blocks):
        start = int(np.clip(NA_ROWS_PER_BLOCK * blk - kh // 2, 0, rows - NA_WIN_ROWS))
        table = []
        for a in range(NA_ROWS_PER_BLOCK):
            r = NA_ROWS_PER_BLOCK * blk + a
            r0 = int(np.clip(r - kh // 2, 0, rows - kh))
            table.append(tuple((start + i) - r + (NA_KH - 1) if 0 <= (start + i) - r0 < kh else -1
                               for i in range(NA_WIN_ROWS)))
            assert sum(e >= 0 for e in table[-1]) == kh
        table = tuple(table)
        if table not in variants:
            variants.append(table)
        vi.append(variants.index(table))
        start_blk.append(start * GRID_W // NA_KBLK)
    return vi, start_blk, variants


def _na_pair_windows(variants):
    lows = []
    for table in variants:
        for j in range(NA_ROWS_PER_BLOCK // 2):
            valid = [i for i in range(NA_WIN_ROWS) if table[2 * j][i] >= 0 or table[2 * j + 1][i] >= 0]
            lo = min(min(valid), NA_WIN_ROWS - NA_PAIR_ROWS)
            assert max(valid) < lo + NA_PAIR_ROWS
            lows.append(lo)
    return lows


def _na_bias_kernel(rpb_ref, o_ref, *, variants, pair_lows):
    h = pl.program_id(0)
    n_ro = 2 * NA_KH - 1
    n_co = 2 * NA_KW - 1
    kc = lax.broadcasted_iota(jnp.int32, (GRID_W, LANES), 0)
    lane = lax.broadcasted_iota(jnp.int32, (GRID_W, LANES), 1)
    c = lane & (GRID_W - 1)
    co = kc - c + (NA_KW - 1)
    c0 = jnp.clip(c - NA_KW // 2, 0, GRID_W - NA_KW)
    win = (kc >= c0) & (kc < c0 + NA_KW)
    neg = jnp.full((GRID_W, LANES), NEG_BIG, F32)
    tiles = []
    for ro in range(n_ro):
        t = neg
        for tt in range(n_co):
            t = jnp.where(win & (co == tt), rpb_ref[(h * n_ro + ro) * n_co + tt] * LOG2_E, t)
        tiles.append(t)
    left = lane < GRID_W
    for vv, table in enumerate(variants):
        for j in range(NA_ROWS_PER_BLOCK // 2):
            lo = pair_lows[vv * (NA_ROWS_PER_BLOCK // 2) + j]
            for ii in range(NA_PAIR_ROWS):
                r1, r2 = table[2 * j][lo + ii], table[2 * j + 1][lo + ii]
                t1 = tiles[r1] if r1 >= 0 else neg
                t2 = tiles[r2] if r2 >= 0 else neg
                o_ref[vv, j, ii * GRID_W:(ii + 1) * GRID_W, :] = jnp.where(left, t1, t2)


def _na_bias_tables(rpb, variants, pair_lows):
    heads = rpb.shape[0]
    n_pairs = NA_ROWS_PER_BLOCK // 2
    nk = NA_PAIR_ROWS * GRID_W
    return pl.pallas_call(
        functools.partial(_na_bias_kernel, variants=variants, pair_lows=pair_lows),
        grid=(heads,),
        in_specs=[pl.BlockSpec(memory_space=pltpu.SMEM)],
        out_specs=pl.BlockSpec((len(variants), None, n_pairs, nk, LANES), lambda h: (0, h, 0, 0, 0)),
        out_shape=jax.ShapeDtypeStruct((len(variants), heads, n_pairs, nk, LANES), F32),
        compiler_params=_params(1),
        name="na_bias_tables",
    )(rpb.reshape(-1))


def _pair_masks():
    lane = lax.broadcasted_iota(jnp.int32, (1, LANES), 1)
    return [(lane // NA_HEAD_DIM == hh).astype(BF16) for hh in range(LANES // NA_HEAD_DIM)]


def _na_body(q_ref, k_ref, vt_ref, kc_ref, vc_ref, bias_ref, o_ref, lows):
    span = NA_PAIR_ROWS * GRID_W
    nk = NA_WIN_ROWS * GRID_W
    q = q_ref[...]
    masks = _pair_masks()

    def scores(hh):
        qh = q * masks[hh]
        st = _dot_nt(k_ref[...], qh)
        return st, _dot_nt(kc_ref[...], qh)

    outs = []
    st, sc = scores(0)
    for hh in range(len(masks)):
        nxt = scores(hh + 1) if hh + 1 < len(masks) else None
        ps, pcs, ls = [], [], []
        for j, lo in enumerate(lows):
            lanes = slice(j * LANES, (j + 1) * LANES)
            r0 = lo * GRID_W
            sj = st[r0:r0 + span, lanes] + bias_ref[hh, j]
            scj = sc[:, lanes]
            m = jnp.maximum(jnp.max(sj, axis=0, keepdims=True), jnp.max(scj, axis=0, keepdims=True))
            pj = jnp.exp2(sj - m)
            pcj = jnp.exp2(scj - m)
            ls.append(jnp.sum(pj, axis=0, keepdims=True) + jnp.sum(pcj, axis=0, keepdims=True))
            pieces = [jnp.zeros((r0, LANES), BF16), pj.astype(BF16), jnp.zeros((nk - r0 - span, LANES), BF16)]
            ps.append(jnp.concatenate([x for x in pieces if x.shape[0]], axis=0))
            pcs.append(pcj.astype(BF16))
        p = jnp.concatenate(ps, axis=1)
        pc = jnp.concatenate(pcs, axis=1)
        rows = slice(hh * NA_HEAD_DIM, (hh + 1) * NA_HEAD_DIM)
        acc = _dot(vt_ref[rows, :], p) + _dot(vc_ref[rows, :], pc)
        outs.append(acc / jnp.concatenate(ls, axis=1))
        if nxt is not None:
            st, sc = nxt
    o_ref[...] = jnp.transpose(jnp.concatenate(outs, axis=0)).astype(BF16)


def _na_kernel(vi_ref, sb_ref, q_ref, k_ref, vt_ref, kc_ref, vc_ref, bias_ref, o_ref, *, pair_lows):
    del sb_ref
    variant = vi_ref[pl.program_id(1)]
    n_lp = NA_ROWS_PER_BLOCK // 2
    for vv in range(len(pair_lows) // n_lp):
        pl.when(variant == vv)(functools.partial(
            _na_body, q_ref, k_ref, vt_ref, kc_ref, vc_ref, bias_ref, o_ref, pair_lows[vv * n_lp:(vv + 1) * n_lp]))


def _neighbourhood_attention(q, k, vt, kc, vct, bias, vi, start_blk, pair_lows, batch):
    n_pairs, t, _ = q.shape
    s = t // batch
    rows = s // GRID_W
    nq = NA_ROWS_PER_BLOCK * GRID_W
    nk = NA_WIN_ROWS * GRID_W
    n_blocks = rows // NA_ROWS_PER_BLOCK
    lc = kc.shape[1] // batch
    kblk_per_batch = s // NA_KBLK
    heads_per_pair = LANES // NA_HEAD_DIM
    bias6 = bias.reshape(bias.shape[0], n_pairs, heads_per_pair, *bias.shape[2:])

    def win_start(b, i, sb_r):
        return (b * kblk_per_batch + sb_r[i]) * NA_KBLK

    grid_spec = pltpu.PrefetchScalarGridSpec(
        num_scalar_prefetch=2,
        grid=(n_pairs, n_blocks, batch),
        in_specs=[pl.BlockSpec((None, nq, LANES), lambda hp, i, b, vi_r, sb_r: (hp, b * n_blocks + i, 0)),
                  pl.BlockSpec((None, pl.Element(nk), pl.Element(LANES)),
                               lambda hp, i, b, vi_r, sb_r: (hp, win_start(b, i, sb_r), 0)),
                  pl.BlockSpec((pl.Element(LANES), pl.Element(nk)),
                               lambda hp, i, b, vi_r, sb_r: (hp * LANES, win_start(b, i, sb_r)))]
        + [pl.BlockSpec((None, lc, LANES), lambda hp, i, b, vi_r, sb_r: (hp, b, 0)),
           pl.BlockSpec((LANES, lc), lambda hp, i, b, vi_r, sb_r: (hp, b)),
           pl.BlockSpec((None, None, heads_per_pair, *bias.shape[2:]),
                        lambda hp, i, b, vi_r, sb_r: (vi_r[i], hp, 0, 0, 0, 0))],
        out_specs=pl.BlockSpec((None, nq, LANES), lambda hp, i, b, vi_r, sb_r: (hp, b * n_blocks + i, 0)),
    )
    return pl.pallas_call(
        functools.partial(_na_kernel, pair_lows=tuple(pair_lows)),
        grid_spec=grid_spec,
        out_shape=jax.ShapeDtypeStruct((n_pairs, t, LANES), BF16),
        compiler_params=_params(3),
        name="neighbourhood_attention",
    )(jnp.asarray(vi, jnp.int32), jnp.asarray(start_blk, jnp.int32),
      q, k, vt, kc, vct, bias6)


def _ctx_attn_kernel(q_ref, k_ref, vt_ref, o_ref):
    q = q_ref[...]
    k = k_ref[...]
    outs = []
    for hh, hm in enumerate(_pair_masks()):
        st = _dot_nt(k, q * hm)
        m = jnp.max(st, axis=0, keepdims=True)
        p = jnp.exp2(st - m)
        l = jnp.sum(p, axis=0, keepdims=True)
        outs.append(_dot(vt_ref[hh * NA_HEAD_DIM:(hh + 1) * NA_HEAD_DIM, :], p.astype(BF16)) / l)
    o_ref[...] = jnp.transpose(jnp.concatenate(outs, axis=0)).astype(BF16)


def _ctx_dense_attention(q, k, vt, batch):
    n_pairs, t, _ = q.shape
    lc = t // batch
    spec = pl.BlockSpec((None, lc, LANES), lambda b, hp: (hp, b, 0))
    return pl.pallas_call(
        _ctx_attn_kernel,
        grid=(batch, n_pairs),
        in_specs=[spec, spec, pl.BlockSpec((LANES, lc), lambda b, hp: (hp, b))],
        out_specs=spec,
        out_shape=jax.ShapeDtypeStruct((n_pairs, t, LANES), BF16),
        compiler_params=_params(2),
        name="ctx_dense_attention",
    )(q, k, vt)


MLA_QK_PAD = 256
MLA_LOOKAHEAD = 1


def _rope(x, cos, sin_next, sin_prev):
    quarter = MLA_ROPE // 4
    return (x * cos + pltpu.roll(x, LANES - quarter, 1) * sin_next + pltpu.roll(x, quarter, 1) * sin_prev)


def _proj_mla_body(x, mod_ref, in_refs, out_refs, *, rope, need_q):
    g_ref, win_ref, gq_ref, gkv_ref, wqn_ref, wqr_ref, wuk_ref, wuv_ref, *rope_refs = in_refs
    if rope:
        cos_ref, sa_ref, sb_ref = rope_refs
    if need_q:
        q_ref, k_ref, v_ref = out_refs
    else:
        k_ref, v_ref = out_refs
    hb = _norm_mod(x, g_ref[...], mod_ref, 3).astype(BF16)
    z = _dot(hb, win_ref[...])
    ckv = (_rms(z[:, MLA_Q_RANK:MLA_Q_RANK + MLA_KV_RANK]) * gkv_ref[...]).astype(BF16)
    kr = z[:, MLA_Q_RANK + MLA_KV_RANK:]
    if rope:
        cos, sa, sb = cos_ref[...], sa_ref[...], sb_ref[...]
        kr = _rope(kr, cos, sa, sb)
    kr = kr.astype(BF16)
    kn = _dot(ckv, wuk_ref[...])
    v_ref[...] = _dot_nt(wuv_ref[...], ckv).astype(BF16)
    for h in range(MLA_HEADS):
        k_ref[h, :, :LANES] = kn[:, h * LANES:(h + 1) * LANES].astype(BF16)
        k_ref[h, :, LANES:] = kr
    if need_q:
        cq = (_rms(z[:, :MLA_Q_RANK]) * gq_ref[...]).astype(BF16)
        scale = (MLA_NOPE + MLA_ROPE) ** -0.5 * LOG2_E
        qn = _dot(cq, wqn_ref[...])
        qr = _dot(cq, wqr_ref[...])
        for h in range(MLA_HEADS):
            qrh = qr[:, h * LANES:(h + 1) * LANES]
            if rope:
                qrh = _rope(qrh, cos, sa, sb)
            q_ref[h, :, :LANES] = (qn[:, h * LANES:(h + 1) * LANES] * scale).astype(BF16)
            q_ref[h, :, LANES:] = (qrh * scale).astype(BF16)
    return x


def _rope_tables(n_tok):
    pos = jnp.arange(n_tok)
    row = (pos // GRID_W).astype(F32)
    col = (pos % GRID_W).astype(F32)
    half = MLA_ROPE // 2
    inv = ROPE_BASE ** (-jnp.arange(0, half, 2, dtype=F32) / half)
    ang_r = row[:, None] * inv[None]
    ang_c = col[:, None] * inv[None]
    ang = jnp.concatenate([ang_r, ang_r, ang_c, ang_c], axis=-1)
    cos, sin = jnp.cos(ang), jnp.sin(ang)
    quarter = MLA_ROPE // 4
    even_q = ((jnp.arange(MLA_ROPE) // quarter) % 2 == 0)[None, :]
    pad = LANES - MLA_ROPE
    cos_p = jnp.pad(cos, ((0, 0), (0, pad)), constant_values=1.0)
    sin_next = jnp.pad(jnp.where(even_q, -sin, 0.0), ((0, 0), (0, pad)))
    sin_prev = jnp.pad(jnp.where(even_q, 0.0, sin), ((0, 0), (0, pad)))
    return cos_p, sin_next, sin_prev


def _sec_proj_mla(g, w, rope_tabs, need_q):
    rope = rope_tabs is not None
    inputs = [(g.reshape(1, -1), "res")] + [(a, "res") for a in w]
    if rope:
        inputs += [(tab, "rope") for tab in rope_tabs]
    qk = ((MLA_HEADS, MLA_QK_PAD), "head")
    outputs = ([qk] if need_q else []) + [qk, (MLA_HEADS * MLA_V, "col")]
    return _Section(functools.partial(_proj_mla_body, rope=rope, need_q=need_q), inputs, outputs)


def _mla_attn_kernel(q_ref, *rest, n_src, chunk):
    k_refs, vt_refs, o_ref = rest[:n_src], rest[n_src:2 * n_src], rest[2 * n_src]
    q = q_ref[...]
    chunks = [(k_ref, vt_ref, c0, min(k_ref.shape[0], c0 + chunk))
              for k_ref, vt_ref in zip(k_refs, vt_refs) for c0 in range(0, k_ref.shape[0], chunk)]

    def scores(j):
        k_ref, _, c0, c1 = chunks[j]
        return _dot_nt(k_ref[c0:c1, :], q)

    m = l = acc = None
    ahead =[scores(j) for j in range(min(MLA_LOOKAHEAD, len(chunks)))]
    for j, (_, vt_ref, c0, c1) in enumerate(chunks):
        if j + MLA_LOOKAHEAD < len(chunks):
            ahead.append(scores(j + MLA_LOOKAHEAD))
        st = ahead.pop(0)
        mj = jnp.max(st, axis=0, keepdims=True)
        if m is None:
            m_new = mj
        else:
            m_new = jnp.maximum(m, mj)
            alpha = jnp.exp2(m - m_new)
        p = jnp.exp2(st - m_new)
        lj = jnp.sum(p, axis=0, keepdims=True)
        oj = _dot(vt_ref[:, c0:c1], p.astype(BF16))
        if m is None:
            l, acc = lj, oj
        else:
            l, acc = alpha * l + lj, alpha * acc + oj
        m = m_new
    o_ref[...] = jnp.transpose(acc / l).astype(BF16)


def _mla_attention(q, sources, batch, tq=1024, chunk=1024):
    t = q.shape[1]
    s = t // batch
    tq = min(tq, s)
    nq = s // tq
    k_specs, v_specs = [], []
    for k, _ in sources:
        n = k.shape[1] // batch
        k_specs.append(pl.BlockSpec((None, n, MLA_QK_PAD), lambda b, h, i: (h, b, 0)))
        v_specs.append(pl.BlockSpec((MLA_V, n), lambda b, h, i: (h, b)))
    return pl.pallas_call(
        functools.partial(_mla_attn_kernel, n_src=len(sources), chunk=chunk),
        grid=(batch, MLA_HEADS, nq),
        in_specs=[pl.BlockSpec((None, tq, MLA_QK_PAD), lambda b, h, i: (h, b * nq + i, 0))] + k_specs + v_specs,
        out_specs=pl.BlockSpec((None, tq, MLA_V), lambda b, h, i: (h, b * nq + i, 0)),
        out_shape=jax.ShapeDtypeStruct((MLA_HEADS, t, MLA_V), BF16),
        compiler_params=_params(3),
        name="mla_attention",
    )(q, *[k for k, _ in sources], *[v for _, v in sources])


def _mla_weights(w_in, g_q, g_kv, w_uq, w_uk, w_uv):
    w_in_p = jnp.pad(w_in, ((0, 0), (0, 5 * LANES - w_in.shape[1]))).astype(BF16)
    wq = w_uq.reshape(MLA_Q_RANK, MLA_HEADS, MLA_NOPE + MLA_ROPE)
    wqn = wq[:, :, :MLA_NOPE].reshape(MLA_Q_RANK, MLA_HEADS * MLA_NOPE).astype(BF16)
    wqr = jnp.pad(wq[:, :, MLA_NOPE:], ((0, 0), (0, 0), (0, LANES - MLA_ROPE)))
    wqr = wqr.reshape(MLA_Q_RANK, MLA_HEADS * LANES).astype(BF16)
    return (w_in_p, g_q.reshape(1, -1), g_kv.reshape(1, -1), wqn, wqr, w_uk.astype(BF16), w_uv.T.astype(BF16))


def kernel(x, c, ctx, c_ctx, ada_w, ada_b, norm_g, ffn_w_gate, ffn_w_up, ffn_w_down, ab_w_in, ab_rpb, ab_w_out,
           mla_w_in, mla_g_q, mla_g_kv, mla_w_uq, mla_w_uk, mla_w_uv, mla_w_o, final_g):
    batch, seq, d = x.shape
    lc = ctx.shape[1]
    depth = ada_w.shape[0]
    assert seq % (GRID_W * NA_WIN_ROWS) == 0 and lc % LANES == 0
    t_lat, t_ctx = batch * seq, batch * lc

    n_rows = -(-(batch + 1) // 8) * 8
    cvec = jnp.concatenate([c, c_ctx[None, :], jnp.zeros((n_rows - batch - 1, d), F32)], axis=0)
    mod = _modulation(cvec, ada_w, ada_b)

    x_l = x.reshape(t_lat, d)
    x_c = ctx.reshape(t_ctx, d)
    lat = dict(rows_per_mod=seq, mod_off=0)
    cx = dict(rows_per_mod=t_ctx, mod_off=batch)

    wg, wu, wd = ffn_w_gate.astype(BF16), ffn_w_up.astype(BF16), ffn_w_down.astype(BF16)
    rope_tabs = None

    for layer in range(depth):
        last = layer == depth - 1
        m = mod[layer]
        g = norm_g[layer]
        ffn1 = _sec_half_ffn(0, g[0], wg, wu, wd, (layer, 0))
        ffn2 = _sec_half_ffn(6, g[2], wg, wu, wd, (layer, 1))
        i = layer // 2
        if layer % 2 == 0:
            proj = _sec_proj_ab(g[1], ab_w_in[i])
            w_out = ab_w_out[i].astype(BF16)
            x_l, uc_l, us_l, q_l, k_l, vt_l = _token_stage(x_l, m, sections=[ffn1, proj], name="ffn_proj_ab", **lat)
            x_c, uc_c, us_c, q_c, k_c, vt_c = _token_stage(x_c, m, sections=[ffn1, proj], name="ffn_proj_ab", **cx)
            a_l = _fnet_positions(uc_l, us_l, batch)
            vi, start_blk, variants = _na_geometry(seq // GRID_W)
            pair_lows = _na_pair_windows(variants)
            bias = _na_bias_tables(ab_rpb[i], variants, pair_lows)
            b_l = _neighbourhood_attention(q_l, k_l, vt_l, k_c, vt_c, bias, vi, start_blk, pair_lows, batch)
            ys_l = [a_l, b_l]
            if not last:
                a_c = _fnet_positions(uc_c, us_c, batch)
                b_c = _ctx_dense_attention(q_c, k_c, vt_c, batch)
                ys_c = [a_c, b_c]
        else:
            if rope_tabs is None:
                rope_tabs = _rope_tables(seq)
            w = _mla_weights(mla_w_in[i], mla_g_q[i], mla_g_kv[i], mla_w_uq[i], mla_w_uk[i], mla_w_uv[i])
            w_out = mla_w_o[i].astype(BF16)
            x_l, q_l, k_l, vt_l = _token_stage(
                x_l, m, sections=[ffn1, _sec_proj_mla(g[1], w, rope_tabs, True)], name="ffn_proj_mla", seq=seq, **lat)
            x_c, *q_c, k_c, vt_c = _token_stage(
                x_c, m, sections=[ffn1, _sec_proj_mla(g[1], w, None, not last)], name="ffn_proj_mla", **cx)
            ys_l = [_mla_attention(q_l, [(k_l, vt_l), (k_c, vt_c)], batch)]
            if not last:
                ys_c = [_mla_attention(q_c[0], [(k_c, vt_c)], batch)]
        tail = [_sec_final_norm(final_g)] if last else []
        (x_l,) = _token_stage(x_l, m, sections=[_sec_outproj(w_out, ys_l), ffn2] + tail, name="outproj_ffn", **lat)
        if not last:
            (x_c,) = _token_stage(x_c, m, sections=[_sec_outproj(w_out, ys_c), ffn2], name="outproj_ffn", **cx)
    return x_l.reshape(batch, seq, d)
```

```python
import functools
import math

import numpy as np
import jax
import jax.numpy as jnp
from jax import lax
from jax.experimental import pallas as pl
from jax.experimental.pallas import tpu as pltpu

F32 = jnp.float32
BF16 = jnp.bfloat16

GRID_W = 64
EPS = 1e-6
N_MOD = 9
FNET_GROUPS = 4
FNET_GROUP_DIM = 128
FNET_DIM = FNET_GROUPS * FNET_GROUP_DIM
NA_HEADS = 8
NA_HEAD_DIM = 64
NA_DIM = NA_HEADS * NA_HEAD_DIM
NA_KH = 8
NA_KW = 16
MLA_HEADS = 8
MLA_NOPE = 128
MLA_ROPE = 64
MLA_V = 128
MLA_Q_RANK = 384
MLA_KV_RANK = 128
ROPE_BASE = 10000.0

LANES = 128
VMEM_LIMIT_BYTES = 56 * 1024 * 1024
NEG_BIG = -1e30
LOG2_E = math.log2(math.e)

NA_ROWS_PER_BLOCK = 8
NA_WIN_ROWS = 16
NA_KBLK = 256
NA_PAIR_ROWS = NA_KH + 1


def _params(n_axes):
    return pltpu.CompilerParams(
        dimension_semantics=("arbitrary",) * n_axes,
        vmem_limit_bytes=VMEM_LIMIT_BYTES,
    )


def _resident(shape, lead=()):
    nd = len(shape) - len(lead)
    return pl.BlockSpec((None,) * len(lead) + tuple(shape[len(lead):]), lambda *_: tuple(lead) + (0,) * nd,
                        pipeline_mode=pl.Buffered(1))


def _dot(a, b):
    return jnp.dot(a, b, preferred_element_type=F32)


def _dot_nt(a, b):
    return lax.dot_general(a, b, (((1,), (1,)), ((), ())), preferred_element_type=F32)


def _rms(x):
    return x * lax.rsqrt(jnp.mean(x * x, axis=-1, keepdims=True) + EPS)


def _norm_mod(x, g, mod_ref, k0):
    shift = mod_ref[k0:k0 + 1, :]
    scale = mod_ref[k0 + 1:k0 + 2, :]
    return (_rms(x) * g) * (1.0 + scale) + shift


def _mod_kernel(c_ref, w_ref, b_ref, o_ref):
    c = c_ref[...]
    sc = c * jax.nn.sigmoid(c)
    w = w_ref[...]
    sc_hi, w_hi = sc.astype(BF16), w.astype(BF16)
    sc_lo = (sc - sc_hi.astype(F32)).astype(BF16)
    w_lo = (w - w_hi.astype(F32)).astype(BF16)
    r = sc.shape[0]
    a = _dot(jnp.concatenate([sc_hi, sc_lo], axis=0), w_hi)
    o_ref[...] = a[:r] + a[r:] + _dot(sc_hi, w_lo) + b_ref[...]


def _modulation(cvec, ada_w, ada_b):
    depth, d, n = ada_w.shape
    r = cvec.shape[0]
    tn = 1536
    assert n % tn == 0
    out = pl.pallas_call(
        _mod_kernel,
        grid=(depth, n // tn),
        in_specs=[
            pl.BlockSpec((r, d), lambda l, j: (0, 0)),
            pl.BlockSpec((None, d, tn), lambda l, j: (l, 0, j)),
            pl.BlockSpec((None, 1, tn), lambda l, j: (l, 0, j)),
        ],
        out_specs=pl.BlockSpec((None, r, tn), lambda l, j: (l, 0, j)),
        out_shape=jax.ShapeDtypeStruct((depth, r, n), F32),
        compiler_params=_params(2),
        name="adaln_modulation",
    )(cvec, ada_w, ada_b.reshape(depth, 1, n))
    return out.reshape(depth, r, N_MOD, d)


def _mod_spec(d, tm, rows_per_mod, mod_off):
    return pl.BlockSpec((None, N_MOD, d), lambda i: (mod_off + (i * tm) // rows_per_mod, 0, 0))


class _Section:
    def __init__(self, body, inputs=(), outputs=(), writes_x=False):
        self.body, self.inputs, self.outputs, self.writes_x = body, list(inputs), list(outputs), writes_x


def _stage_kernel(*refs, sections, writes_x):
    x_ref, mod_ref = refs[0], refs[1]
    n_in = sum(len(s.inputs) for s in sections)
    in_refs = refs[2:2 + n_in]
    out_refs = refs[2 + n_in:]
    x = x_ref[...]
    xo_ref = None
    if writes_x:
        xo_ref, out_refs = out_refs[0], out_refs[1:]
    i0 = o0 = 0
    for s in sections:
        x = s.body(x, mod_ref, in_refs[i0:i0 + len(s.inputs)], out_refs[o0:o0 + len(s.outputs)])
        i0 += len(s.inputs)
        o0 += len(s.outputs)
    if writes_x:
        xo_ref[...] = x


def _token_stage(x, mod, rows_per_mod, mod_off, sections, name, seq=None, tm=512):
    t, d = x.shape
    tm = min(tm, t)
    assert t % tm == 0 and rows_per_mod % tm == 0
    writes_x = any(s.writes_x for s in sections)
    in_specs = [pl.BlockSpec((tm, d), lambda i: (i, 0)), _mod_spec(d, tm, rows_per_mod, mod_off)]
    args = [x, mod]
    for s in sections:
        for arr, kind in s.inputs:
            if kind == "res":
                in_specs.append(_resident(arr.shape))
            elif isinstance(kind, tuple):
                in_specs.append(_resident(arr.shape, kind[1]))
            elif kind == "row":
                in_specs.append(pl.BlockSpec((tm, arr.shape[1]), lambda i: (i, 0)))
            elif kind == "head":
                in_specs.append(pl.BlockSpec((arr.shape[0], tm, arr.shape[2]), lambda i: (0, i, 0)))
            else:
                assert kind == "rope" and seq % tm == 0
                n_rt = seq // tm
                in_specs.append(pl.BlockSpec((tm, arr.shape[1]), lambda i: (i % n_rt, 0)))
            args.append(arr)
    out_specs, out_shape = [], []
    if writes_x:
        out_specs.append(pl.BlockSpec((tm, d), lambda i: (i, 0)))
        out_shape.append(jax.ShapeDtypeStruct((t, d), F32))
    for s in sections:
        for w, kind in s.outputs:
            if kind == "row":
                out_specs.append(pl.BlockSpec((tm, w), lambda i: (i, 0)))
                out_shape.append(jax.ShapeDtypeStruct((t, w), BF16))
            elif kind == "head":
                heads, hw = w
                out_specs.append(pl.BlockSpec((heads, tm, hw), lambda i: (0, i, 0)))
                out_shape.append(jax.ShapeDtypeStruct((heads, t, hw), BF16))
            else:
                out_specs.append(pl.BlockSpec((w, tm), lambda i: (0, i)))
                out_shape.append(jax.ShapeDtypeStruct((w, t), BF16))
    return pl.pallas_call(
        functools.partial(_stage_kernel, sections=sections, writes_x=writes_x),
        grid=(t // tm,),
        in_specs=in_specs,
        out_specs=out_specs,
        out_shape=out_shape,
        compiler_params=_params(1),
        name=name,
    )(*args)


FFN_CHUNKS = 2


def _sec_half_ffn(k0, g, wg, wu, wd, lead):
    def body(x, mod_ref, in_refs, out_refs):
        g_ref, wg_ref, wu_ref, wd_ref = in_refs
        hb = _norm_mod(x, g_ref[...], mod_ref, k0).astype(BF16)
        tf = wg_ref.shape[1] // FFN_CHUNKS
        y = None
        for j in range(FFN_CHUNKS):
            a = _dot(hb, wg_ref[:, j * tf:(j + 1) * tf])
            u = _dot(hb, wu_ref[:, j * tf:(j + 1) * tf])
            act = ((a * jax.nn.sigmoid(a)) * u).astype(BF16)
            yj = _dot(act, wd_ref[j * tf:(j + 1) * tf, :])
            y = yj if y is None else y + yj
        return x + (0.5 * mod_ref[k0 + 2:k0 + 3, :]) * y

    assert wg.shape[-1] % (FFN_CHUNKS * LANES) == 0
    at = ("res_at", tuple(lead))
    return _Section(body, [(g.reshape(1, -1), "res"), (wg, at), (wu, at), (wd, at)], writes_x=True)


def _sec_outproj(w, ys):
    def body(x, mod_ref, in_refs, out_refs):
        w_ref, *y_refs = in_refs
        acc, r0 = None, 0
        for y_ref in y_refs:
            if len(y_ref.shape) == 3:
                y = jnp.concatenate([y_ref[h] for h in range(y_ref.shape[0])], axis=1)
            else:
                y = y_ref[...]
            k = y.shape[1]
            part = _dot(y, w_ref[r0:r0 + k, :])
            acc = part if acc is None else acc + part
            r0 += k
        return x + mod_ref[5:6, :] * acc

    return _Section(body, [(w, "res")] + [(y, "head" if y.ndim == 3 else "row") for y in ys], writes_x=True)


def _sec_final_norm(g):
    def body(x, mod_ref, in_refs, out_refs):
        return _rms(x) * in_refs[0][...]

    return _Section(body, [(g.reshape(1, -1), "res")], writes_x=True)


def _proj_ab_body(x, mod_ref, in_refs, out_refs):
    g_ref, w_ref, wvt_ref, cc_ref, cs_ref = in_refs
    uc_ref, us_ref, q_ref, k_ref, vt_ref = out_refs
    hb = _norm_mod(x, g_ref[...], mod_ref, 3).astype(BF16)
    z = _dot(hb, w_ref[...])
    u = z[:, :FNET_DIM].astype(BF16)
    for g in range(FNET_GROUPS):
        sl = slice(g * FNET_GROUP_DIM, (g + 1) * FNET_GROUP_DIM)
        uc_ref[:, sl] = _dot(u[:, sl], cc_ref[...]).astype(BF16)
        us_ref[:, sl] = _dot(u[:, sl], cs_ref[...]).astype(BF16)
    for hp in range(NA_DIM // LANES):
        q0, k0 = FNET_DIM + hp * LANES, FNET_DIM + NA_DIM + hp * LANES
        q_ref[hp] = (z[:, q0:q0 + LANES] * (NA_HEAD_DIM ** -0.5 * LOG2_E)).astype(BF16)
        k_ref[hp] = z[:, k0:k0 + LANES].astype(BF16)
    vt_ref[...] = _dot_nt(wvt_ref[...], hb).astype(BF16)
    return x


def _dft_real_imag(n, scale):
    j = np.arange(n, dtype=np.int64)
    ang = 2.0 * np.pi * ((j[:, None] * j[None, :]) % n).astype(np.float64) / n
    return (np.cos(ang) * scale).astype(np.float32), (np.sin(ang) * scale).astype(np.float32)


def _sec_proj_ab(g, w_in):
    cc, cs = _dft_real_imag(FNET_GROUP_DIM, FNET_GROUP_DIM ** -0.5)
    n_row = FNET_DIM + 2 * NA_DIM
    w_row = w_in[:, :n_row].astype(BF16)
    w_vt = w_in[:, n_row:].T.astype(BF16)
    inputs = [(g.reshape(1, -1), "res"), (w_row, "res"), (w_vt, "res"),
              (jnp.asarray(cc).astype(BF16), "res"), (jnp.asarray(cs).astype(BF16), "res")]
    pairs = ((NA_DIM // LANES, LANES), "head")
    outputs = [(FNET_DIM, "row"), (FNET_DIM, "row"), pairs, pairs, (NA_DIM, "col")]
    return _Section(_proj_ab_body, inputs, outputs)


def _fnet_kernel(cn_ref, sn_ref, uc_ref, us_ref, o_ref):
    o = _dot(cn_ref[...], uc_ref[...]) - _dot(sn_ref[...], us_ref[...])
    o_ref[...] = o.astype(BF16)


def _position_dft_matrices(n):
    if n <= 512:
        c, s = _dft_real_imag(n, n ** -0.5)
        return jnp.asarray(c).astype(BF16), jnp.asarray(s).astype(BF16)
    n1 = GRID_W
    n2 = n // n1
    k = np.arange(n, dtype=np.int64)
    hi = np.arange(n2, dtype=np.int64)[:, None] * n1
    lo = np.arange(n1, dtype=np.int64)[:, None]
    ang_hi = 2.0 * np.pi * ((hi * k[None, :]) % n).astype(np.float64) / n
    ang_lo = 2.0 * np.pi * ((lo * k[None, :]) % n).astype(np.float64) / n
    ch, sh = jnp.asarray(np.cos(ang_hi), F32)[:, None, :], jnp.asarray(np.sin(ang_hi), F32)[:, None, :]
    cl, sl = jnp.asarray(np.cos(ang_lo), F32)[None, :, :], jnp.asarray(np.sin(ang_lo), F32)[None, :, :]
    scale = n ** -0.5
    c = ((ch * cl - sh * sl) * scale).reshape(n, n).astype(BF16)
    s = ((sh * cl + ch * sl) * scale).reshape(n, n).astype(BF16)
    return c, s


def _fnet_positions(uc, us, batch):
    t, w = uc.shape
    n = t // batch
    cn, sn = _position_dft_matrices(n)
    tn = min(512, n)
    nt = n // tn
    return pl.pallas_call(
        _fnet_kernel,
        grid=(nt, batch),
        in_specs=[
            pl.BlockSpec((tn, n), lambda i, b: (i, 0)),
            pl.BlockSpec((tn, n), lambda i, b: (i, 0)),
            pl.BlockSpec((n, w), lambda i, b: (b, 0)),
            pl.BlockSpec((n, w), lambda i, b: (b, 0)),
        ],
        out_specs=pl.BlockSpec((tn, w), lambda i, b: (b * nt + i, 0)),
        out_shape=jax.ShapeDtypeStruct((t, w), BF16),
        compiler_params=_params(2),
        name="fnet_positions",
    )(cn, sn, uc, us)


def _na_geometry(rows):
    n_blocks = rows // NA_ROWS_PER_BLOCK
    kh = min(NA_KH, rows)
    variants, vi, start_blk = [], [], []
    for blk in range(n_blocks):
        start = int(np.clip(NA_ROWS_PER_BLOCK * blk - kh // 2, 0, rows - NA_WIN_ROWS))
        table = []
        for a in range(NA_ROWS_PER_BLOCK):
            r = NA_ROWS_PER_BLOCK * blk + a
            r0 = int(np.clip(r - kh // 2, 0, rows - kh))
            table.append(tuple((start + i) - r + (NA_KH - 1) if 0 <= (start + i) - r0 < kh else -1
                               for i in range(NA_WIN_ROWS)))
            assert sum(e >= 0 for e in table[-1]) == kh
        table = tuple(table)
        if table not in variants:
            variants.append(table)
        vi.append(variants.index(table))
        start_blk.append(start * GRID_W // NA_KBLK)
    return vi, start_blk, variants


def _na_pair_windows(variants):
    lows = []
    for table in variants:
        for j in range(NA_ROWS_PER_BLOCK // 2):
            valid = [i for i in range(NA_WIN_ROWS) if table[2 * j][i] >= 0 or table[2 * j + 1][i] >= 0]
            lo = min(min(valid), NA_WIN_ROWS - NA_PAIR_ROWS)
            assert max(valid) < lo + NA_PAIR_ROWS
            lows.append(lo)
    return lows


def _na_bias_kernel(rpb_ref, o_ref, *, variants, pair_lows):
    h = pl.program_id(0)
    n_ro = 2 * NA_KH - 1
    n_co = 2 * NA_KW - 1
    kc = lax.broadcasted_iota(jnp.int32, (GRID_W, LANES), 0)
    lane = lax.broadcasted_iota(jnp.int32, (GRID_W, LANES), 1)
    c = lane & (GRID_W - 1)
    co = kc - c + (NA_KW - 1)
    c0 = jnp.clip(c - NA_KW // 2, 0, GRID_W - NA_KW)
    win = (kc >= c0) & (kc < c0 + NA_KW)
    neg = jnp.full((GRID_W, LANES), NEG_BIG, F32)
    tiles = []
    for ro in range(n_ro):
        t = neg
        for tt in range(n_co):
            t = jnp.where(win & (co == tt), rpb_ref[(h * n_ro + ro) * n_co + tt] * LOG2_E, t)
        tiles.append(t)
    left = lane < GRID_W
    for vv, table in enumerate(variants):
        for j in range(NA_ROWS_PER_BLOCK // 2):
            lo = pair_lows[vv * (NA_ROWS_PER_BLOCK // 2) + j]
            for ii in range(NA_PAIR_ROWS):
                r1, r2 = table[2 * j][lo + ii], table[2 * j + 1][lo + ii]
                t1 = tiles[r1] if r1 >= 0 else neg
                t2 = tiles[r2] if r2 >= 0 else neg
                o_ref[vv, j, ii * GRID_W:(ii + 1) * GRID_W, :] = jnp.where(left, t1, t2)


def _na_bias_tables(rpb, variants, pair_lows):
    heads = rpb.shape[0]
    n_pairs = NA_ROWS_PER_BLOCK // 2
    nk = NA_PAIR_ROWS * GRID_W
    return pl.pallas_call(
        functools.partial(_na_bias_kernel, variants=variants, pair_lows=pair_lows),
        grid=(heads,),
        in_specs=[pl.BlockSpec(memory_space=pltpu.SMEM)],
        out_specs=pl.BlockSpec((len(variants), None, n_pairs, nk, LANES), lambda h: (0, h, 0, 0, 0)),
        out_shape=jax.ShapeDtypeStruct((len(variants), heads, n_pairs, nk, LANES), F32),
        compiler_params=_params(1),
        name="na_bias_tables",
    )(rpb.reshape(-1))


def _pair_masks():
    lane = lax.broadcasted_iota(jnp.int32, (1, LANES), 1)
    return [(lane // NA_HEAD_DIM == hh).astype(BF16) for hh in range(LANES // NA_HEAD_DIM)]


def _na_body(q_ref, k_ref, vt_ref, kc_ref, vc_ref, bias_ref, o_ref, lows):
    span = NA_PAIR_ROWS * GRID_W
    nk = NA_WIN_ROWS * GRID_W
    q = q_ref[...]
    masks = _pair_masks()

    def scores(hh):
        qh = q * masks[hh]
        st = _dot_nt(k_ref[...], qh)
        return st, _dot_nt(kc_ref[...], qh)

    outs = []
    st, sc = scores(0)
    for hh in range(len(masks)):
        nxt = scores(hh + 1) if hh + 1 < len(masks) else None
        ps, pcs, ls = [], [], []
        for j, lo in enumerate(lows):
            lanes = slice(j * LANES, (j + 1) * LANES)
            r0 = lo * GRID_W
            sj = st[r0:r0 + span, lanes] + bias_ref[hh, j]
            scj = sc[:, lanes]
            m = jnp.maximum(jnp.max(sj, axis=0, keepdims=True), jnp.max(scj, axis=0, keepdims=True))
            pj = jnp.exp2(sj - m)
            pcj = jnp.exp2(scj - m)
            ls.append(jnp.sum(pj, axis=0, keepdims=True) + jnp.sum(pcj, axis=0, keepdims=True))
            pieces = [jnp.zeros((r0, LANES), BF16), pj.astype(BF16), jnp.zeros((nk - r0 - span, LANES), BF16)]
            ps.append(jnp.concatenate([x for x in pieces if x.shape[0]], axis=0))
            pcs.append(pcj.astype(BF16))
        p = jnp.concatenate(ps, axis=1)
        pc = jnp.concatenate(pcs, axis=1)
        rows = slice(hh * NA_HEAD_DIM, (hh + 1) * NA_HEAD_DIM)
        acc = _dot(vt_ref[rows, :], p) + _dot(vc_ref[rows, :], pc)
        outs.append(acc / jnp.concatenate(ls, axis=1))
        if nxt is not None:
            st, sc = nxt
    o_ref[...] = jnp.transpose(jnp.concatenate(outs, axis=0)).astype(BF16)


def _na_kernel(vi_ref, sb_ref, q_ref, k_ref, vt_ref, kc_ref, vc_ref, bias_ref, o_ref, *, pair_lows):
    del sb_ref
    variant = vi_ref[pl.program_id(1)]
    n_lp = NA_ROWS_PER_BLOCK // 2
    for vv in range(len(pair_lows) // n_lp):
        pl.when(variant == vv)(functools.partial(
            _na_body, q_ref, k_ref, vt_ref, kc_ref, vc_ref, bias_ref, o_ref, pair_lows[vv * n_lp:(vv + 1) * n_lp]))


def _neighbourhood_attention(q, k, vt, kc, vct, bias, vi, start_blk, pair_lows, batch):
    n_pairs, t, _ = q.shape
    s = t // batch
    rows = s // GRID_W
    nq = NA_ROWS_PER_BLOCK * GRID_W
    nk = NA_WIN_ROWS * GRID_W
    n_blocks = rows // NA_ROWS_PER_BLOCK
    lc = kc.shape[1] // batch
    kblk_per_batch = s // NA_KBLK
    heads_per_pair = LANES // NA_HEAD_DIM
    bias6 = bias.reshape(bias.shape[0], n_pairs, heads_per_pair, *bias.shape[2:])

    def win_start(b, i, sb_r):
        return (b * kblk_per_batch + sb_r[i]) * NA_KBLK

    grid_spec = pltpu.PrefetchScalarGridSpec(
        num_scalar_prefetch=2,
        grid=(n_pairs, n_blocks, batch),
        in_specs=[pl.BlockSpec((None, nq, LANES), lambda hp, i, b, vi_r, sb_r: (hp, b * n_blocks + i, 0)),
                  pl.BlockSpec((None, pl.Element(nk), pl.Element(LANES)),
                               lambda hp, i, b, vi_r, sb_r: (hp, win_start(b, i, sb_r), 0)),
                  pl.BlockSpec((pl.Element(LANES), pl.Element(nk)),
                               lambda hp, i, b, vi_r, sb_r: (hp * LANES, win_start(b, i, sb_r)))]
        + [pl.BlockSpec((None, lc, LANES), lambda hp, i, b, vi_r, sb_r: (hp, b, 0)),
           pl.BlockSpec((LANES, lc), lambda hp, i, b, vi_r, sb_r: (hp, b)),
           pl.BlockSpec((None, None, heads_per_pair, *bias.shape[2:]),
                        lambda hp, i, b, vi_r, sb_r: (vi_r[i], hp, 0, 0, 0, 0))],
        out_specs=pl.BlockSpec((None, nq, LANES), lambda hp, i, b, vi_r, sb_r: (hp, b * n_blocks + i, 0)),
    )
    return pl.pallas_call(
        functools.partial(_na_kernel, pair_lows=tuple(pair_lows)),
        grid_spec=grid_spec,
        out_shape=jax.ShapeDtypeStruct((n_pairs, t, LANES), BF16),
        compiler_params=_params(3),
        name="neighbourhood_attention",
    )(jnp.asarray(vi, jnp.int32), jnp.asarray(start_blk, jnp.int32),
      q, k, vt, kc, vct, bias6)


def _ctx_attn_kernel(q_ref, k_ref, vt_ref, o_ref):
    q = q_ref[...]
    k = k_ref[...]
    outs = []
    for hh, hm in enumerate(_pair_masks()):
        st = _dot_nt(k, q * hm)
        m = jnp.max(st, axis=0, keepdims=True)
        p = jnp.exp2(st - m)
        l = jnp.sum(p, axis=0, keepdims=True)
        outs.append(_dot(vt_ref[hh * NA_HEAD_DIM:(hh + 1) * NA_HEAD_DIM, :], p.astype(BF16)) / l)
    o_ref[...] = jnp.transpose(jnp.concatenate(outs, axis=0)).astype(BF16)


def _ctx_dense_attention(q, k, vt, batch):
    n_pairs, t, _ = q.shape
    lc = t // batch
    spec = pl.BlockSpec((None, lc, LANES), lambda b, hp: (hp, b, 0))
    return pl.pallas_call(
        _ctx_attn_kernel,
        grid=(batch, n_pairs),
        in_specs=[spec, spec, pl.BlockSpec((LANES, lc), lambda b, hp: (hp, b))],
        out_specs=spec,
        out_shape=jax.ShapeDtypeStruct((n_pairs, t, LANES), BF16),
        compiler_params=_params(2),
        name="ctx_dense_attention",
    )(q, k, vt)


MLA_QK_PAD = 256
MLA_LOOKAHEAD = 1


def _rope(x, cos, sin_next, sin_prev):
    quarter = MLA_ROPE // 4
    return (x * cos + pltpu.roll(x, LANES - quarter, 1) * sin_next + pltpu.roll(x, quarter, 1) * sin_prev)


def _proj_mla_body(x, mod_ref, in_refs, out_refs, *, rope, need_q):
    g_ref, win_ref, gq_ref, gkv_ref, wqn_ref, wqr_ref, wuk_ref, wuv_ref, *rope_refs = in_refs
    if rope:
        cos_ref, sa_ref, sb_ref = rope_refs
    if need_q:
        q_ref, k_ref, v_ref = out_refs
    else:
        k_ref, v_ref = out_refs
    hb = _norm_mod(x, g_ref[...], mod_ref, 3).astype(BF16)
    z = _dot(hb, win_ref[...])
    ckv = (_rms(z[:, MLA_Q_RANK:MLA_Q_RANK + MLA_KV_RANK]) * gkv_ref[...]).astype(BF16)
    kr = z[:, MLA_Q_RANK + MLA_KV_RANK:]
    if rope:
        cos, sa, sb = cos_ref[...], sa_ref[...], sb_ref[...]
        kr = _rope(kr, cos, sa, sb)
    kr = kr.astype(BF16)
    kn = _dot(ckv, wuk_ref[...])
    v_ref[...] = _dot_nt(wuv_ref[...], ckv).astype(BF16)
    for h in range(MLA_HEADS):
        k_ref[h, :, :LANES] = kn[:, h * LANES:(h + 1) * LANES].astype(BF16)
        k_ref[h, :, LANES:] = kr
    if need_q:
        cq = (_rms(z[:, :MLA_Q_RANK]) * gq_ref[...]).astype(BF16)
        scale = (MLA_NOPE + MLA_ROPE) ** -0.5 * LOG2_E
        qn = _dot(cq, wqn_ref[...])
        qr = _dot(cq, wqr_ref[...])
        for h in range(MLA_HEADS):
            qrh = qr[:, h * LANES:(h + 1) * LANES]
            if rope:
                qrh = _rope(qrh, cos, sa, sb)
            q_ref[h, :, :LANES] = (qn[:, h * LANES:(h + 1) * LANES] * scale).astype(BF16)
            q_ref[h, :, LANES:] = (qrh * scale).astype(BF16)
    return x


def _rope_tables(n_tok):
    pos = jnp.arange(n_tok)
    row = (pos // GRID_W).astype(F32)
    col = (pos % GRID_W).astype(F32)
    half = MLA_ROPE // 2
    inv = ROPE_BASE ** (-jnp.arange(0, half, 2, dtype=F32) / half)
    ang_r = row[:, None] * inv[None]
    ang_c = col[:, None] * inv[None]
    ang = jnp.concatenate([ang_r, ang_r, ang_c, ang_c], axis=-1)
    cos, sin = jnp.cos(ang), jnp.sin(ang)
    quarter = MLA_ROPE // 4
    even_q = ((jnp.arange(MLA_ROPE) // quarter) % 2 == 0)[None, :]
    pad = LANES - MLA_ROPE
    cos_p = jnp.pad(cos, ((0, 0), (0, pad)), constant_values=1.0)
    sin_next = jnp.pad(jnp.where(even_q, -sin, 0.0), ((0, 0), (0, pad)))
    sin_prev = jnp.pad(jnp.where(even_q, 0.0, sin), ((0, 0), (0, pad)))
    return cos_p, sin_next, sin_prev


def _sec_proj_mla(g, w, rope_tabs, need_q):
    rope = rope_tabs is not None
    inputs = [(g.reshape(1, -1), "res")] + [(a, "res") for a in w]
    if rope:
        inputs += [(tab, "rope") for tab in rope_tabs]
    qk = ((MLA_HEADS, MLA_QK_PAD), "head")
    outputs = ([qk] if need_q else []) + [qk, (MLA_HEADS * MLA_V, "col")]
    return _Section(functools.partial(_proj_mla_body, rope=rope, need_q=need_q), inputs, outputs)


def _mla_attn_kernel(q_ref, *rest, n_src, chunk):
    k_refs, vt_refs, o_ref = rest[:n_src], rest[n_src:2 * n_src], rest[2 * n_src]
    q = q_ref[...]
    chunks = [(k_ref, vt_ref, c0, min(k_ref.shape[0], c0 + chunk))
              for k_ref, vt_ref in zip(k_refs, vt_refs) for c0 in range(0, k_ref.shape[0], chunk)]

    def scores(j):
        k_ref, _, c0, c1 = chunks[j]
        return _dot_nt(k_ref[c0:c1, :], q)

    m = l = acc = None
    ahead =[scores(j) for j in range(min(MLA_LOOKAHEAD, len(chunks)))]
    for j, (_, vt_ref, c0, c1) in enumerate(chunks):
        if j + MLA_LOOKAHEAD < len(chunks):
            ahead.append(scores(j + MLA_LOOKAHEAD))
        st = ahead.pop(0)
        mj = jnp.max(st, axis=0, keepdims=True)
        if m is None:
            m_new = mj
        else:
            m_new = jnp.maximum(m, mj)
            alpha = jnp.exp2(m - m_new)
        p = jnp.exp2(st - m_new)
        lj = jnp.sum(p, axis=0, keepdims=True)
        oj = _dot(vt_ref[:, c0:c1], p.astype(BF16))
        if m is None:
            l, acc = lj, oj
        else:
            l, acc = alpha * l + lj, alpha * acc + oj
        m = m_new
    o_ref[...] = jnp.transpose(acc / l).astype(BF16)


def _mla_attention(q, sources, batch, tq=512, chunk=1024):
    t = q.shape[1]
    s = t // batch
    tq = min(tq, s)
    nq = s // tq
    k_specs, v_specs = [], []
    for k, _ in sources:
        n = k.shape[1] // batch
        k_specs.append(pl.BlockSpec((None, n, MLA_QK_PAD), lambda b, h, i: (h, b, 0)))
        v_specs.append(pl.BlockSpec((MLA_V, n), lambda b, h, i: (h, b)))
    return pl.pallas_call(
        functools.partial(_mla_attn_kernel, n_src=len(sources), chunk=chunk),
        grid=(batch, MLA_HEADS, nq),
        in_specs=[pl.BlockSpec((None, tq, MLA_QK_PAD), lambda b, h, i: (h, b * nq + i, 0))] + k_specs + v_specs,
        out_specs=pl.BlockSpec((None, tq, MLA_V), lambda b, h, i: (h, b * nq + i, 0)),
        out_shape=jax.ShapeDtypeStruct((MLA_HEADS, t, MLA_V), BF16),
        compiler_params=_params(3),
        name="mla_attention",
    )(q, *[k for k, _ in sources], *[v for _, v in sources])


def _mla_weights(w_in, g_q, g_kv, w_uq, w_uk, w_uv):
    w_in_p = jnp.pad(w_in, ((0, 0), (0, 5 * LANES - w_in.shape[1]))).astype(BF16)
    wq = w_uq.reshape(MLA_Q_RANK, MLA_HEADS, MLA_NOPE + MLA_ROPE)
    wqn = wq[:, :, :MLA_NOPE].reshape(MLA_Q_RANK, MLA_HEADS * MLA_NOPE).astype(BF16)
    wqr = jnp.pad(wq[:, :, MLA_NOPE:], ((0, 0), (0, 0), (0, LANES - MLA_ROPE)))
    wqr = wqr.reshape(MLA_Q_RANK, MLA_HEADS * LANES).astype(BF16)
    return (w_in_p, g_q.reshape(1, -1), g_kv.reshape(1, -1), wqn, wqr, w_uk.astype(BF16), w_uv.T.astype(BF16))


def kernel(x, c, ctx, c_ctx, ada_w, ada_b, norm_g, ffn_w_gate, ffn_w_up, ffn_w_down, ab_w_in, ab_rpb, ab_w_out,
           mla_w_in, mla_g_q, mla_g_kv, mla_w_uq, mla_w_uk, mla_w_uv, mla_w_o, final_g):
    batch, seq, d = x.shape
    lc = ctx.shape[1]
    depth = ada_w.shape[0]
    assert seq % (GRID_W * NA_WIN_ROWS) == 0 and lc % LANES == 0
    t_lat, t_ctx = batch * seq, batch * lc

    n_rows = -(-(batch + 1) // 8) * 8
    cvec = jnp.concatenate([c, c_ctx[None, :], jnp.zeros((n_rows - batch - 1, d), F32)], axis=0)
    mod = _modulation(cvec, ada_w, ada_b)

    x_l = x.reshape(t_lat, d)
    x_c = ctx.reshape(t_ctx, d)
    lat = dict(rows_per_mod=seq, mod_off=0)
    cx = dict(rows_per_mod=t_ctx, mod_off=batch)

    wg, wu, wd = ffn_w_gate.astype(BF16), ffn_w_up.astype(BF16), ffn_w_down.astype(BF16)
    rope_tabs = None

    for layer in range(depth):
        last = layer == depth - 1
        m = mod[layer]
        g = norm_g[layer]
        ffn1 = _sec_half_ffn(0, g[0], wg, wu, wd, (layer, 0))
        ffn2 = _sec_half_ffn(6, g[2], wg, wu, wd, (layer, 1))
        i = layer // 2
        if layer % 2 == 0:
            proj = _sec_proj_ab(g[1], ab_w_in[i])
            w_out = ab_w_out[i].astype(BF16)
            x_l, uc_l, us_l, q_l, k_l, vt_l = _token_stage(x_l, m, sections=[ffn1, proj], name="ffn_proj_ab", **lat)
            x_c, uc_c, us_c, q_c, k_c, vt_c = _token_stage(x_c, m, sections=[ffn1, proj], name="ffn_proj_ab", **cx)
            a_l = _fnet_positions(uc_l, us_l, batch)
            vi, start_blk, variants = _na_geometry(seq // GRID_W)
            pair_lows = _na_pair_windows(variants)
            bias = _na_bias_tables(ab_rpb[i], variants, pair_lows)
            b_l = _neighbourhood_attention(q_l, k_l, vt_l, k_c, vt_c, bias, vi, start_blk, pair_lows, batch)
            ys_l = [a_l, b_l]
            if not last:
                a_c = _fnet_positions(uc_c, us_c, batch)
                b_c = _ctx_dense_attention(q_c, k_c, vt_c, batch)
                ys_c = [a_c, b_c]
        else:
            if rope_tabs is None:
                rope_tabs = _rope_tables(seq)
            w = _mla_weights(mla_w_in[i], mla_g_q[i], mla_g_kv[i], mla_w_uq[i], mla_w_uk[i], mla_w_uv[i])
            w_out = mla_w_o[i].astype(BF16)
            x_l, q_l, k_l, vt_l = _token_stage(
                x_l, m, sections=[ffn1, _sec_proj_mla(g[1], w, rope_tabs, True)], name="ffn_proj_mla", seq=seq, **lat)
            x_c, *q_c, k_c, vt_c = _token_stage(
                x_c, m, sections=[ffn1, _sec_proj_mla(g[1], w, None, not last)], name="ffn_proj_mla", **cx)
            ys_l = [_mla_attention(q_l, [(k_l, vt_l), (k_c, vt_c)], batch)]
            if not last:
                ys_c = [_mla_attention(q_c[0], [(k_c, vt_c)], batch)]
        tail = [_sec_final_norm(final_g)] if last else []
        (x_l,) = _token_stage(x_l, m, sections=[_sec_outproj(w_out, ys_l), ffn2] + tail, name="outproj_ffn", **lat)
        if not last:
            (x_c,) = _token_stage(x_c, m, sections=[_sec_outproj(w_out, ys_c), ffn2], name="outproj_ffn", **cx)
    return x_l.reshape(batch, seq, d)
```

```python
import functools
import math

import numpy as np
import jax
import jax.numpy as jnp
from jax import lax
from jax.experimental import pallas as pl
from jax.experimental.pallas import tpu as pltpu

F32 = jnp.float32
BF16 = jnp.bfloat16

GRID_W = 64
EPS = 1e-6
N_MOD = 9
FNET_GROUPS = 4
FNET_GROUP_DIM = 128
FNET_DIM = FNET_GROUPS * FNET_GROUP_DIM
NA_HEADS = 8
NA_HEAD_DIM = 64
NA_DIM = NA_HEADS * NA_HEAD_DIM
NA_KH = 8
NA_KW = 16
MLA_HEADS = 8
MLA_NOPE = 128
MLA_ROPE = 64
MLA_V = 128
MLA_Q_RANK = 384
MLA_KV_RANK = 128
ROPE_BASE = 10000.0

LANES = 128
VMEM_LIMIT_BYTES = 56 * 1024 * 1024
NEG_BIG = -1e30
LOG2_E = math.log2(math.e)

NA_ROWS_PER_BLOCK = 8
NA_WIN_ROWS = 16
NA_KBLK = 256
NA_PAIR_ROWS = NA_KH + 1


def _params(n_axes):
    return pltpu.CompilerParams(
        dimension_semantics=("arbitrary",) * n_axes,
        vmem_limit_bytes=VMEM_LIMIT_BYTES,
    )


def _resident(shape, lead=()):
    nd = len(shape) - len(lead)
    return pl.BlockSpec((None,) * len(lead) + tuple(shape[len(lead):]), lambda *_: tuple(lead) + (0,) * nd,
                        pipeline_mode=pl.Buffered(1))


def _dot(a, b):
    return jnp.dot(a, b, preferred_element_type=F32)


def _dot_nt(a, b):
    return lax.dot_general(a, b, (((1,), (1,)), ((), ())), preferred_element_type=F32)


def _rms(x):
    return x * lax.rsqrt(jnp.mean(x * x, axis=-1, keepdims=True) + EPS)


def _norm_mod(x, g, mod_ref, k0):
    shift = mod_ref[k0:k0 + 1, :]
    scale = mod_ref[k0 + 1:k0 + 2, :]
    return (_rms(x) * g) * (1.0 + scale) + shift


def _mod_kernel(c_ref, w_ref, b_ref, o_ref):
    c = c_ref[...]
    sc = c * jax.nn.sigmoid(c)
    w = w_ref[...]
    sc_hi, w_hi = sc.astype(BF16), w.astype(BF16)
    sc_lo = (sc - sc_hi.astype(F32)).astype(BF16)
    w_lo = (w - w_hi.astype(F32)).astype(BF16)
    r = sc.shape[0]
    a = _dot(jnp.concatenate([sc_hi, sc_lo], axis=0), w_hi)
    o_ref[...] = a[:r] + a[r:] + _dot(sc_hi, w_lo) + b_ref[...]


def _modulation(cvec, ada_w, ada_b):
    depth, d, n = ada_w.shape
    r = cvec.shape[0]
    tn = 1536
    assert n % tn == 0
    out = pl.pallas_call(
        _mod_kernel,
        grid=(depth, n // tn),
        in_specs=[
            pl.BlockSpec((r, d), lambda l, j: (0, 0)),
            pl.BlockSpec((None, d, tn), lambda l, j: (l, 0, j)),
            pl.BlockSpec((None, 1, tn), lambda l, j: (l, 0, j)),
        ],
        out_specs=pl.BlockSpec((None, r, tn), lambda l, j: (l, 0, j)),
        out_shape=jax.ShapeDtypeStruct((depth, r, n), F32),
        compiler_params=_params(2),
        name="adaln_modulation",
    )(cvec, ada_w, ada_b.reshape(depth, 1, n))
    return out.reshape(depth, r, N_MOD, d)


def _mod_spec(d, tm, rows_per_mod, mod_off):
    return pl.BlockSpec((None, N_MOD, d), lambda i: (mod_off + (i * tm) // rows_per_mod, 0, 0))


class _Section:
    def __init__(self, body, inputs=(), outputs=(), writes_x=False):
        self.body, self.inputs, self.outputs, self.writes_x = body, list(inputs), list(outputs), writes_x


def _stage_kernel(*refs, sections, writes_x):
    x_ref, mod_ref = refs[0], refs[1]
    n_in = sum(len(s.inputs) for s in sections)
    in_refs = refs[2:2 + n_in]
    out_refs = refs[2 + n_in:]
    x = x_ref[...]
    xo_ref = None
    if writes_x:
        xo_ref, out_refs = out_refs[0], out_refs[1:]
    i0 = o0 = 0
    for s in sections:
        x = s.body(x, mod_ref, in_refs[i0:i0 + len(s.inputs)], out_refs[o0:o0 + len(s.outputs)])
        i0 += len(s.inputs)
        o0 += len(s.outputs)
    if writes_x:
        xo_ref[...] = x


def _token_stage(x, mod, rows_per_mod, mod_off, sections, name, seq=None, tm=512):
    t, d = x.shape
    tm = min(tm, t)
    assert t % tm == 0 and rows_per_mod % tm == 0
    writes_x = any(s.writes_x for s in sections)
    in_specs = [pl.BlockSpec((tm, d), lambda i: (i, 0)), _mod_spec(d, tm, rows_per_mod, mod_off)]
    args = [x, mod]
    for s in sections:
        for arr, kind in s.inputs:
            if kind == "res":
                in_specs.append(_resident(arr.shape))
            elif isinstance(kind, tuple):
                in_specs.append(_resident(arr.shape, kind[1]))
            elif kind == "row":
                in_specs.append(pl.BlockSpec((tm, arr.shape[1]), lambda i: (i, 0)))
            elif kind == "head":
                in_specs.append(pl.BlockSpec((arr.shape[0], tm, arr.shape[2]), lambda i: (0, i, 0)))
            else:
                assert kind == "rope" and seq % tm == 0
                n_rt = seq // tm
                in_specs.append(pl.BlockSpec((tm, arr.shape[1]), lambda i: (i % n_rt, 0)))
            args.append(arr)
    out_specs, out_shape = [], []
    if writes_x:
        out_specs.append(pl.BlockSpec((tm, d), lambda i: (i, 0)))
        out_shape.append(jax.ShapeDtypeStruct((t, d), F32))
    for s in sections:
        for w, kind in s.outputs:
            if kind == "row":
                out_specs.append(pl.BlockSpec((tm, w), lambda i: (i, 0)))
                out_shape.append(jax.ShapeDtypeStruct((t, w), BF16))
            elif kind == "head":
                heads, hw = w
                out_specs.append(pl.BlockSpec((heads, tm, hw), lambda i: (0, i, 0)))
                out_shape.append(jax.ShapeDtypeStruct((heads, t, hw), BF16))
            else:
                out_specs.append(pl.BlockSpec((w, tm), lambda i: (0, i)))
                out_shape.append(jax.ShapeDtypeStruct((w, t), BF16))
    return pl.pallas_call(
        functools.partial(_stage_kernel, sections=sections, writes_x=writes_x),
        grid=(t // tm,),
        in_specs=in_specs,
        out_specs=out_specs,
        out_shape=out_shape,
        compiler_params=_params(1),
        name=name,
    )(*args)


FFN_CHUNKS = 1


def _sec_half_ffn(k0, g, wg, wu, wd, lead):
    def body(x, mod_ref, in_refs, out_refs):
        g_ref, wg_ref, wu_ref, wd_ref = in_refs
        hb = _norm_mod(x, g_ref[...], mod_ref, k0).astype(BF16)
        tf = wg_ref.shape[1] // FFN_CHUNKS
        y = None
        for j in range(FFN_CHUNKS):
            a = _dot(hb, wg_ref[:, j * tf:(j + 1) * tf])
            u = _dot(hb, wu_ref[:, j * tf:(j + 1) * tf])
            act = ((a * jax.nn.sigmoid(a)) * u).astype(BF16)
            yj = _dot(act, wd_ref[j * tf:(j + 1) * tf, :])
            y = yj if y is None else y + yj
        return x + (0.5 * mod_ref[k0 + 2:k0 + 3, :]) * y

    assert wg.shape[-1] % (FFN_CHUNKS * LANES) == 0
    at = ("res_at", tuple(lead))
    return _Section(body, [(g.reshape(1, -1), "res"), (wg, at), (wu, at), (wd, at)], writes_x=True)


def _sec_outproj(w, ys):
    def body(x, mod_ref, in_refs, out_refs):
        w_ref, *y_refs = in_refs
        acc, r0 = None, 0
        for y_ref in y_refs:
            if len(y_ref.shape) == 3:
                y = jnp.concatenate([y_ref[h] for h in range(y_ref.shape[0])], axis=1)
            else:
                y = y_ref[...]
            k = y.shape[1]
            part = _dot(y, w_ref[r0:r0 + k, :])
            acc = part if acc is None else acc + part
            r0 += k
        return x + mod_ref[5:6, :] * acc

    return _Section(body, [(w, "res")] + [(y, "head" if y.ndim == 3 else "row") for y in ys], writes_x=True)


def _sec_final_norm(g):
    def body(x, mod_ref, in_refs, out_refs):
        return _rms(x) * in_refs[0][...]

    return _Section(body, [(g.reshape(1, -1), "res")], writes_x=True)


def _proj_ab_body(x, mod_ref, in_refs, out_refs):
    g_ref, w_ref, wvt_ref, cc_ref, cs_ref = in_refs
    uc_ref, us_ref, q_ref, k_ref, vt_ref = out_refs
    hb = _norm_mod(x, g_ref[...], mod_ref, 3).astype(BF16)
    z = _dot(hb, w_ref[...])
    u = z[:, :FNET_DIM].astype(BF16)
    for g in range(FNET_GROUPS):
        sl = slice(g * FNET_GROUP_DIM, (g + 1) * FNET_GROUP_DIM)
        uc_ref[:, sl] = _dot(u[:, sl], cc_ref[...]).astype(BF16)
        us_ref[:, sl] = _dot(u[:, sl], cs_ref[...]).astype(BF16)
    for hp in range(NA_DIM // LANES):
        q0, k0 = FNET_DIM + hp * LANES, FNET_DIM + NA_DIM + hp * LANES
        q_ref[hp] = (z[:, q0:q0 + LANES] * (NA_HEAD_DIM ** -0.5 * LOG2_E)).astype(BF16)
        k_ref[hp] = z[:, k0:k0 + LANES].astype(BF16)
    vt_ref[...] = _dot_nt(wvt_ref[...], hb).astype(BF16)
    return x


def _dft_real_imag(n, scale):
    j = np.arange(n, dtype=np.int64)
    ang = 2.0 * np.pi * ((j[:, None] * j[None, :]) % n).astype(np.float64) / n
    return (np.cos(ang) * scale).astype(np.float32), (np.sin(ang) * scale).astype(np.float32)


def _sec_proj_ab(g, w_in):
    cc, cs = _dft_real_imag(FNET_GROUP_DIM, FNET_GROUP_DIM ** -0.5)
    n_row = FNET_DIM + 2 * NA_DIM
    w_row = w_in[:, :n_row].astype(BF16)
    w_vt = w_in[:, n_row:].T.astype(BF16)
    inputs = [(g.reshape(1, -1), "res"), (w_row, "res"), (w_vt, "res"),
              (jnp.asarray(cc).astype(BF16), "res"), (jnp.asarray(cs).astype(BF16), "res")]
    pairs = ((NA_DIM // LANES, LANES), "head")
    outputs = [(FNET_DIM, "row"), (FNET_DIM, "row"), pairs, pairs, (NA_DIM, "col")]
    return _Section(_proj_ab_body, inputs, outputs)


def _fnet_kernel(cn_ref, sn_ref, uc_ref, us_ref, o_ref):
    o = _dot(cn_ref[...], uc_ref[...]) - _dot(sn_ref[...], us_ref[...])
    o_ref[...] = o.astype(BF16)


def _position_dft_matrices(n):
    if n <= 512:
        c, s = _dft_real_imag(n, n ** -0.5)
        return jnp.asarray(c).astype(BF16), jnp.asarray(s).astype(BF16)
    n1 = GRID_W
    n2 = n // n1
    k = np.arange(n, dtype=np.int64)
    hi = np.arange(n2, dtype=np.int64)[:, None] * n1
    lo = np.arange(n1, dtype=np.int64)[:, None]
    ang_hi = 2.0 * np.pi * ((hi * k[None, :]) % n).astype(np.float64) / n
    ang_lo = 2.0 * np.pi * ((lo * k[None, :]) % n).astype(np.float64) / n
    ch, sh = jnp.asarray(np.cos(ang_hi), F32)[:, None, :], jnp.asarray(np.sin(ang_hi), F32)[:, None, :]
    cl, sl = jnp.asarray(np.cos(ang_lo), F32)[None, :, :], jnp.asarray(np.sin(ang_lo), F32)[None, :, :]
    scale = n ** -0.5
    c = ((ch * cl - sh * sl) * scale).reshape(n, n).astype(BF16)
    s = ((sh * cl + ch * sl) * scale).reshape(n, n).astype(BF16)
    return c, s


def _fnet_positions(uc, us, batch):
    t, w = uc.shape
    n = t // batch
    cn, sn = _position_dft_matrices(n)
    tn = min(512, n)
    nt = n // tn
    return pl.pallas_call(
        _fnet_kernel,
        grid=(nt, batch),
        in_specs=[
            pl.BlockSpec((tn, n), lambda i, b: (i, 0)),
            pl.BlockSpec((tn, n), lambda i, b: (i, 0)),
            pl.BlockSpec((n, w), lambda i, b: (b, 0)),
            pl.BlockSpec((n, w), lambda i, b: (b, 0)),
        ],
        out_specs=pl.BlockSpec((tn, w), lambda i, b: (b * nt + i, 0)),
        out_shape=jax.ShapeDtypeStruct((t, w), BF16),
        compiler_params=_params(2),
        name="fnet_positions",
    )(cn, sn, uc, us)


def _na_geometry(rows):
    n_blocks = rows // NA_ROWS_PER_BLOCK
    kh = min(NA_KH, rows)
    variants, vi, start_blk = [], [], []
    for blk in range(n_blocks):
        start = int(np.clip(NA_ROWS_PER_BLOCK * blk - kh // 2, 0, rows - NA_WIN_ROWS))
        table = []
        for a in range(NA_ROWS_PER_BLOCK):
            r = NA_ROWS_PER_BLOCK * blk + a
            r0 = int(np.clip(r - kh // 2, 0, rows - kh))
            table.append(tuple((start + i) - r + (NA_KH - 1) if 0 <= (start + i) - r0 < kh else -1
                               for i in range(NA_WIN_ROWS)))
            assert sum(e >= 0 for e in table[-1]) == kh
        table = tuple(table)
        if table not in variants:
            variants.append(table)
        vi.append(variants.index(table))
        start_blk.append(start * GRID_W // NA_KBLK)
    return vi, start_blk, variants


def _na_pair_windows(variants):
    lows = []
    for table in variants:
        for j in range(NA_ROWS_PER_BLOCK // 2):
            valid = [i for i in range(NA_WIN_ROWS) if table[2 * j][i] >= 0 or table[2 * j + 1][i] >= 0]
            lo = min(min(valid), NA_WIN_ROWS - NA_PAIR_ROWS)
            assert max(valid) < lo + NA_PAIR_ROWS
            lows.append(lo)
    return lows


def _na_bias_kernel(rpb_ref, o_ref, *, variants, pair_lows):
    h = pl.program_id(0)
    n_ro = 2 * NA_KH - 1
    n_co = 2 * NA_KW - 1
    kc = lax.broadcasted_iota(jnp.int32, (GRID_W, LANES), 0)
    lane = lax.broadcasted_iota(jnp.int32, (GRID_W, LANES), 1)
    c = lane & (GRID_W - 1)
    co = kc - c + (NA_KW - 1)
    c0 = jnp.clip(c - NA_KW // 2, 0, GRID_W - NA_KW)
    win = (kc >= c0) & (kc < c0 + NA_KW)
    neg = jnp.full((GRID_W, LANES), NEG_BIG, F32)
    tiles = []
    for ro in range(n_ro):
        t = neg
        for tt in range(n_co):
            t = jnp.where(win & (co == tt), rpb_ref[(h * n_ro + ro) * n_co + tt] * LOG2_E, t)
        tiles.append(t)
    left = lane < GRID_W
    for vv, table in enumerate(variants):
        for j in range(NA_ROWS_PER_BLOCK // 2):
            lo = pair_lows[vv * (NA_ROWS_PER_BLOCK // 2) + j]
            for ii in range(NA_PAIR_ROWS):
                r1, r2 = table[2 * j][lo + ii], table[2 * j + 1][lo + ii]
                t1 = tiles[r1] if r1 >= 0 else neg
                t2 = tiles[r2] if r2 >= 0 else neg
                o_ref[vv, j, ii * GRID_W:(ii + 1) * GRID_W, :] = jnp.where(left, t1, t2)


def _na_bias_tables(rpb, variants, pair_lows):
    heads = rpb.shape[0]
    n_pairs = NA_ROWS_PER_BLOCK // 2
    nk = NA_PAIR_ROWS * GRID_W
    return pl.pallas_call(
        functools.partial(_na_bias_kernel, variants=variants, pair_lows=pair_lows),
        grid=(heads,),
        in_specs=[pl.BlockSpec(memory_space=pltpu.SMEM)],
        out_specs=pl.BlockSpec((len(variants), None, n_pairs, nk, LANES), lambda h: (0, h, 0, 0, 0)),
        out_shape=jax.ShapeDtypeStruct((len(variants), heads, n_pairs, nk, LANES), F32),
        compiler_params=_params(1),
        name="na_bias_tables",
    )(rpb.reshape(-1))


def _pair_masks():
    lane = lax.broadcasted_iota(jnp.int32, (1, LANES), 1)
    return [(lane // NA_HEAD_DIM == hh).astype(BF16) for hh in range(LANES // NA_HEAD_DIM)]


def _na_body(q_ref, k_ref, vt_ref, kc_ref, vc_ref, bias_ref, o_ref, lows):
    span = NA_PAIR_ROWS * GRID_W
    nk = NA_WIN_ROWS * GRID_W
    q = q_ref[...]
    masks = _pair_masks()

    def scores(hh):
        qh = q * masks[hh]
        st = _dot_nt(k_ref[...], qh)
        return st, _dot_nt(kc_ref[...], qh)

    outs = []
    st, sc = scores(0)
    for hh in range(len(masks)):
        nxt = scores(hh + 1) if hh + 1 < len(masks) else None
        ps, pcs, ls = [], [], []
        for j, lo in enumerate(lows):
            lanes = slice(j * LANES, (j + 1) * LANES)
            r0 = lo * GRID_W
            sj = st[r0:r0 + span, lanes] + bias_ref[hh, j]
            scj = sc[:, lanes]
            m = jnp.maximum(jnp.max(sj, axis=0, keepdims=True), jnp.max(scj, axis=0, keepdims=True))
            pj = jnp.exp2(sj - m)
            pcj = jnp.exp2(scj - m)
            ls.append(jnp.sum(pj, axis=0, keepdims=True) + jnp.sum(pcj, axis=0, keepdims=True))
            pieces = [jnp.zeros((r0, LANES), BF16), pj.astype(BF16), jnp.zeros((nk - r0 - span, LANES), BF16)]
            ps.append(jnp.concatenate([x for x in pieces if x.shape[0]], axis=0))
            pcs.append(pcj.astype(BF16))
        p = jnp.concatenate(ps, axis=1)
        pc = jnp.concatenate(pcs, axis=1)
        rows = slice(hh * NA_HEAD_DIM, (hh + 1) * NA_HEAD_DIM)
        acc = _dot(vt_ref[rows, :], p) + _dot(vc_ref[rows, :], pc)
        outs.append(acc / jnp.concatenate(ls, axis=1))
        if nxt is not None:
            st, sc = nxt
    o_ref[...] = jnp.transpose(jnp.concatenate(outs, axis=0)).astype(BF16)


def _na_kernel(vi_ref, sb_ref, q_ref, k_ref, vt_ref, kc_ref, vc_ref, bias_ref, o_ref, *, pair_lows):
    del sb_ref
    variant = vi_ref[pl.program_id(1)]
    n_lp = NA_ROWS_PER_BLOCK // 2
    for vv in range(len(pair_lows) // n_lp):
        pl.when(variant == vv)(functools.partial(
            _na_body, q_ref, k_ref, vt_ref, kc_ref, vc_ref, bias_ref, o_ref, pair_lows[vv * n_lp:(vv + 1) * n_lp]))


def _neighbourhood_attention(q, k, vt, kc, vct, bias, vi, start_blk, pair_lows, batch):
    n_pairs, t, _ = q.shape
    s = t // batch
    rows = s // GRID_W
    nq = NA_ROWS_PER_BLOCK * GRID_W
    nk = NA_WIN_ROWS * GRID_W
    n_blocks = rows // NA_ROWS_PER_BLOCK
    lc = kc.shape[1] // batch
    kblk_per_batch = s // NA_KBLK
    heads_per_pair = LANES // NA_HEAD_DIM
    bias6 = bias.reshape(bias.shape[0], n_pairs, heads_per_pair, *bias.shape[2:])

    def win_start(b, i, sb_r):
        return (b * kblk_per_batch + sb_r[i]) * NA_KBLK

    grid_spec = pltpu.PrefetchScalarGridSpec(
        num_scalar_prefetch=2,
        grid=(n_pairs, n_blocks, batch),
        in_specs=[pl.BlockSpec((None, nq, LANES), lambda hp, i, b, vi_r, sb_r: (hp, b * n_blocks + i, 0)),
                  pl.BlockSpec((None, pl.Element(nk), pl.Element(LANES)),
                               lambda hp, i, b, vi_r, sb_r: (hp, win_start(b, i, sb_r), 0)),
                  pl.BlockSpec((pl.Element(LANES), pl.Element(nk)),
                               lambda hp, i, b, vi_r, sb_r: (hp * LANES, win_start(b, i, sb_r)))]
        + [pl.BlockSpec((None, lc, LANES), lambda hp, i, b, vi_r, sb_r: (hp, b, 0)),
           pl.BlockSpec((LANES, lc), lambda hp, i, b, vi_r, sb_r: (hp, b)),
           pl.BlockSpec((None, None, heads_per_pair, *bias.shape[2:]),
                        lambda hp, i, b, vi_r, sb_r: (vi_r[i], hp, 0, 0, 0, 0))],
        out_specs=pl.BlockSpec((None, nq, LANES), lambda hp, i, b, vi_r, sb_r: (hp, b * n_blocks + i, 0)),
    )
    return pl.pallas_call(
        functools.partial(_na_kernel, pair_lows=tuple(pair_lows)),
        grid_spec=grid_spec,
        out_shape=jax.ShapeDtypeStruct((n_pairs, t, LANES), BF16),
        compiler_params=_params(3),
        name="neighbourhood_attention",
    )(jnp.asarray(vi, jnp.int32), jnp.asarray(start_blk, jnp.int32),
      q, k, vt, kc, vct, bias6)


def _ctx_attn_kernel(q_ref, k_ref, vt_ref, o_ref):
    q = q_ref[...]
    k = k_ref[...]
    outs = []
    for hh, hm in enumerate(_pair_masks()):
        st = _dot_nt(k, q * hm)
        m = jnp.max(st, axis=0, keepdims=True)
        p = jnp.exp2(st - m)
        l = jnp.sum(p, axis=0, keepdims=True)
        outs.append(_dot(vt_ref[hh * NA_HEAD_DIM:(hh + 1) * NA_HEAD_DIM, :], p.astype(BF16)) / l)
    o_ref[...] = jnp.transpose(jnp.concatenate(outs, axis=0)).astype(BF16)


def _ctx_dense_attention(q, k, vt, batch):
    n_pairs, t, _ = q.shape
    lc = t // batch
    spec = pl.BlockSpec((None, lc, LANES), lambda b, hp: (hp, b, 0))
    return pl.pallas_call(
        _ctx_attn_kernel,
        grid=(batch, n_pairs),
        in_specs=[spec, spec, pl.BlockSpec((LANES, lc), lambda b, hp: (hp, b))],
        out_specs=spec,
        out_shape=jax.ShapeDtypeStruct((n_pairs, t, LANES), BF16),
        compiler_params=_params(2),
        name="ctx_dense_attention",
    )(q, k, vt)


MLA_QK_PAD = 256
MLA_LOOKAHEAD = 1


def _rope(x, cos, sin_next, sin_prev):
    quarter = MLA_ROPE // 4
    return (x * cos + pltpu.roll(x, LANES - quarter, 1) * sin_next + pltpu.roll(x, quarter, 1) * sin_prev)


def _proj_mla_body(x, mod_ref, in_refs, out_refs, *, rope, need_q):
    g_ref, win_ref, gq_ref, gkv_ref, wqn_ref, wqr_ref, wuk_ref, wuv_ref, *rope_refs = in_refs
    if rope:
        cos_ref, sa_ref, sb_ref = rope_refs
    if need_q:
        q_ref, k_ref, v_ref = out_refs
    else:
        k_ref, v_ref = out_refs
    hb = _norm_mod(x, g_ref[...], mod_ref, 3).astype(BF16)
    z = _dot(hb, win_ref[...])
    ckv = (_rms(z[:, MLA_Q_RANK:MLA_Q_RANK + MLA_KV_RANK]) * gkv_ref[...]).astype(BF16)
    kr = z[:, MLA_Q_RANK + MLA_KV_RANK:]
    if rope:
        cos, sa, sb = cos_ref[...], sa_ref[...], sb_ref[...]
        kr = _rope(kr, cos, sa, sb)
    kr = kr.astype(BF16)
    kn = _dot(ckv, wuk_ref[...])
    v_ref[...] = _dot_nt(wuv_ref[...], ckv).astype(BF16)
    for h in range(MLA_HEADS):
        k_ref[h, :, :LANES] = kn[:, h * LANES:(h + 1) * LANES].astype(BF16)
        k_ref[h, :, LANES:] = kr
    if need_q:
        cq = (_rms(z[:, :MLA_Q_RANK]) * gq_ref[...]).astype(BF16)
        scale = (MLA_NOPE + MLA_ROPE) ** -0.5 * LOG2_E
        qn = _dot(cq, wqn_ref[...])
        qr = _dot(cq, wqr_ref[...])
        for h in range(MLA_HEADS):
            qrh = qr[:, h * LANES:(h + 1) * LANES]
            if rope:
                qrh = _rope(qrh, cos, sa, sb)
            q_ref[h, :, :LANES] = (qn[:, h * LANES:(h + 1) * LANES] * scale).astype(BF16)
            q_ref[h, :, LANES:] = (qrh * scale).astype(BF16)
    return x


def _rope_tables(n_tok):
    pos = jnp.arange(n_tok)
    row = (pos // GRID_W).astype(F32)
    col = (pos % GRID_W).astype(F32)
    half = MLA_ROPE // 2
    inv = ROPE_BASE ** (-jnp.arange(0, half, 2, dtype=F32) / half)
    ang_r = row[:, None] * inv[None]
    ang_c = col[:, None] * inv[None]
    ang = jnp.concatenate([ang_r, ang_r, ang_c, ang_c], axis=-1)
    cos, sin = jnp.cos(ang), jnp.sin(ang)
    quarter = MLA_ROPE // 4
    even_q = ((jnp.arange(MLA_ROPE) // quarter) % 2 == 0)[None, :]
    pad = LANES - MLA_ROPE
    cos_p = jnp.pad(cos, ((0, 0), (0, pad)), constant_values=1.0)
    sin_next = jnp.pad(jnp.where(even_q, -sin, 0.0), ((0, 0), (0, pad)))
    sin_prev = jnp.pad(jnp.where(even_q, 0.0, sin), ((0, 0), (0, pad)))
    return cos_p, sin_next, sin_prev


def _sec_proj_mla(g, w, rope_tabs, need_q):
    rope = rope_tabs is not None
    inputs = [(g.reshape(1, -1), "res")] + [(a, "res") for a in w]
    if rope:
        inputs += [(tab, "rope") for tab in rope_tabs]
    qk = ((MLA_HEADS, MLA_QK_PAD), "head")
    outputs = ([qk] if need_q else []) + [qk, (MLA_HEADS * MLA_V, "col")]
    return _Section(functools.partial(_proj_mla_body, rope=rope, need_q=need_q), inputs, outputs)


def _mla_attn_kernel(q_ref, *rest, n_src, chunk):
    k_refs, vt_refs, o_ref = rest[:n_src], rest[n_src:2 * n_src], rest[2 * n_src]
    q = q_ref[...]
    chunks = [(k_ref, vt_ref, c0, min(k_ref.shape[0], c0 + chunk))
              for k_ref, vt_ref in zip(k_refs, vt_refs) for c0 in range(0, k_ref.shape[0], chunk)]

    def scores(j):
        k_ref, _, c0, c1 = chunks[j]
        return _dot_nt(k_ref[c0:c1, :], q)

    m = l = acc = None
    ahead =[scores(j) for j in range(min(MLA_LOOKAHEAD, len(chunks)))]
    for j, (_, vt_ref, c0, c1) in enumerate(chunks):
        if j + MLA_LOOKAHEAD < len(chunks):
            ahead.append(scores(j + MLA_LOOKAHEAD))
        st = ahead.pop(0)
        mj = jnp.max(st, axis=0, keepdims=True)
        if m is None:
            m_new = mj
        else:
            m_new = jnp.maximum(m, mj)
            alpha = jnp.exp2(m - m_new)
        p = jnp.exp2(st - m_new)
        lj = jnp.sum(p, axis=0, keepdims=True)
        oj = _dot(vt_ref[:, c0:c1], p.astype(BF16))
        if m is None:
            l, acc = lj, oj
        else:
            l, acc = alpha * l + lj, alpha * acc + oj
        m = m_new
    o_ref[...] = jnp.transpose(acc / l).astype(BF16)


def _mla_attention(q, sources, batch, tq=1024, chunk=1024):
    t = q.shape[1]
    s = t // batch
    tq = min(tq, s)
    nq = s // tq
    k_specs, v_specs = [], []
    for k, _ in sources:
        n = k.shape[1] // batch
        k_specs.append(pl.BlockSpec((None, n, MLA_QK_PAD), lambda b, h, i: (h, b, 0)))
        v_specs.append(pl.BlockSpec((MLA_V, n), lambda b, h, i: (h, b)))
    return pl.pallas_call(
        functools.partial(_mla_attn_kernel, n_src=len(sources), chunk=chunk),
        grid=(batch, MLA_HEADS, nq),
        in_specs=[pl.BlockSpec((None, tq, MLA_QK_PAD), lambda b, h, i: (h, b * nq + i, 0))] + k_specs + v_specs,
        out_specs=pl.BlockSpec((None, tq, MLA_V), lambda b, h, i: (h, b * nq + i, 0)),
        out_shape=jax.ShapeDtypeStruct((MLA_HEADS, t, MLA_V), BF16),
        compiler_params=_params(3),
        name="mla_attention",
    )(q, *[k for k, _ in sources], *[v for _, v in sources])


def _mla_weights(w_in, g_q, g_kv, w_uq, w_uk, w_uv):
    w_in_p = jnp.pad(w_in, ((0, 0), (0, 5 * LANES - w_in.shape[1]))).astype(BF16)
    wq = w_uq.reshape(MLA_Q_RANK, MLA_HEADS, MLA_NOPE + MLA_ROPE)
    wqn = wq[:, :, :MLA_NOPE].reshape(MLA_Q_RANK, MLA_HEADS * MLA_NOPE).astype(BF16)
    wqr = jnp.pad(wq[:, :, MLA_NOPE:], ((0, 0), (0, 0), (0, LANES - MLA_ROPE)))
    wqr = wqr.reshape(MLA_Q_RANK, MLA_HEADS * LANES).astype(BF16)
    return (w_in_p, g_q.reshape(1, -1), g_kv.reshape(1, -1), wqn, wqr, w_uk.astype(BF16), w_uv.T.astype(BF16))


def kernel(x, c, ctx, c_ctx, ada_w, ada_b, norm_g, ffn_w_gate, ffn_w_up, ffn_w_down, ab_w_in, ab_rpb, ab_w_out,
           mla_w_in, mla_g_q, mla_g_kv, mla_w_uq, mla_w_uk, mla_w_uv, mla_w_o, final_g):
    batch, seq, d = x.shape
    lc = ctx.shape[1]
    depth = ada_w.shape[0]
    assert seq % (GRID_W * NA_WIN_ROWS) == 0 and lc % LANES == 0
    t_lat, t_ctx = batch * seq, batch * lc

    n_rows = -(-(batch + 1) // 8) * 8
    cvec = jnp.concatenate([c, c_ctx[None, :], jnp.zeros((n_rows - batch - 1, d), F32)], axis=0)
    mod = _modulation(cvec, ada_w, ada_b)

    x_l = x.reshape(t_lat, d)
    x_c = ctx.reshape(t_ctx, d)
    lat = dict(rows_per_mod=seq, mod_off=0)
    cx = dict(rows_per_mod=t_ctx, mod_off=batch)

    wg, wu, wd = ffn_w_gate.astype(BF16), ffn_w_up.astype(BF16), ffn_w_down.astype(BF16)
    rope_tabs = None

    for layer in range(depth):
        last = layer == depth - 1
        m = mod[layer]
        g = norm_g[layer]
        ffn1 = _sec_half_ffn(0, g[0], wg, wu, wd, (layer, 0))
        ffn2 = _sec_half_ffn(6, g[2], wg, wu, wd, (layer, 1))
        i = layer // 2
        if layer % 2 == 0:
            proj = _sec_proj_ab(g[1], ab_w_in[i])
            w_out = ab_w_out[i].astype(BF16)
            x_l, uc_l, us_l, q_l, k_l, vt_l = _token_stage(x_l, m, sections=[ffn1, proj], name="ffn_proj_ab", **lat)
            x_c, uc_c, us_c, q_c, k_c, vt_c = _token_stage(x_c, m, sections=[ffn1, proj], name="ffn_proj_ab", **cx)
            a_l = _fnet_positions(uc_l, us_l, batch)
            vi, start_blk, variants = _na_geometry(seq // GRID_W)
            pair_lows = _na_pair_windows(variants)
            bias = _na_bias_tables(ab_rpb[i], variants, pair_lows)
            b_l = _neighbourhood_attention(q_l, k_l, vt_l, k_c, vt_c, bias, vi, start_blk, pair_lows, batch)
            ys_l = [a_l, b_l]
            if not last:
                a_c = _fnet_positions(uc_c, us_c, batch)
                b_c = _ctx_dense_attention(q_c, k_c, vt_c, batch)
                ys_c = [a_c, b_c]
        else:
            if rope_tabs is None:
                rope_tabs = _rope_tables(seq)
            w = _mla_weights(mla_w_in[i], mla_g_q[i], mla_g_kv[i], mla_w_uq[i], mla_w_uk[i], mla_w_uv[i])
            w_out = mla_w_o[i].astype(BF16)
            x_l, q_l, k_l, vt_l = _token_stage(
                x_l, m, sections=[ffn1, _sec_proj_mla(g[1], w, rope_tabs, True)], name="ffn_proj_mla", seq=seq, **lat)
            x_c, *q_c, k_c, vt_c = _token_stage(
                x_c, m, sections=[ffn1, _sec_proj_mla(g[1], w, None, not last)], name="ffn_proj_mla", **cx)
            ys_l = [_mla_attention(q_l, [(k_l, vt_l), (k_c, vt_c)], batch)]
            if not last:
                ys_c = [_mla_attention(q_c[0], [(k_c, vt_c)], batch)]
        tail = [_sec_final_norm(final_g)] if last else []
        (x_l,) = _token_stage(x_l, m, sections=[_sec_outproj(w_out, ys_l), ffn2] + tail, name="outproj_ffn", **lat)
        if not last:
            (x_c,) = _token_stage(x_c, m, sections=[_sec_outproj(w_out, ys_c), ffn2], name="outproj_ffn", **cx)
    return x_l.reshape(batch, seq, d)
```

```python
import functools
import math

import numpy as np
import jax
import jax.numpy as jnp
from jax import lax
from jax.experimental import pallas as pl
from jax.experimental.pallas import tpu as pltpu

F32 = jnp.float32
BF16 = jnp.bfloat16

GRID_W = 64
EPS = 1e-6
N_MOD = 9
FNET_GROUPS = 4
FNET_GROUP_DIM = 128
FNET_DIM = FNET_GROUPS * FNET_GROUP_DIM
NA_HEADS = 8
NA_HEAD_DIM = 64
NA_DIM = NA_HEADS * NA_HEAD_DIM
NA_KH = 8
NA_KW = 16
MLA_HEADS = 8
MLA_NOPE = 128
MLA_ROPE = 64
MLA_V = 128
MLA_Q_RANK = 384
MLA_KV_RANK = 128
ROPE_BASE = 10000.0

LANES = 128
VMEM_LIMIT_BYTES = 56 * 1024 * 1024
NEG_BIG = -1e30
LOG2_E = math.log2(math.e)

NA_ROWS_PER_BLOCK = 8
NA_WIN_ROWS = 16
NA_KBLK = 256
NA_PAIR_ROWS = NA_KH + 1


def _params(n_axes):
    return pltpu.CompilerParams(
        dimension_semantics=("arbitrary",) * n_axes,
        vmem_limit_bytes=VMEM_LIMIT_BYTES,
    )


def _resident(shape, lead=()):
    nd = len(shape) - len(lead)
    return pl.BlockSpec((None,) * len(lead) + tuple(shape[len(lead):]), lambda *_: tuple(lead) + (0,) * nd,
                        pipeline_mode=pl.Buffered(1))


def _dot(a, b):
    return jnp.dot(a, b, preferred_element_type=F32)


def _dot_nt(a, b):
    return lax.dot_general(a, b, (((1,), (1,)), ((), ())), preferred_element_type=F32)


def _rms(x):
    return x * lax.rsqrt(jnp.mean(x * x, axis=-1, keepdims=True) + EPS)


def _norm_mod(x, g, mod_ref, k0):
    shift = mod_ref[k0:k0 + 1, :]
    scale = mod_ref[k0 + 1:k0 + 2, :]
    return (_rms(x) * g) * (1.0 + scale) + shift


def _mod_kernel(c_ref, w_ref, b_ref, o_ref):
    c = c_ref[...]
    sc = c * jax.nn.sigmoid(c)
    w = w_ref[...]
    sc_hi, w_hi = sc.astype(BF16), w.astype(BF16)
    sc_lo = (sc - sc_hi.astype(F32)).astype(BF16)
    w_lo = (w - w_hi.astype(F32)).astype(BF16)
    r = sc.shape[0]
    a = _dot(jnp.concatenate([sc_hi, sc_lo], axis=0), w_hi)
    o_ref[...] = a[:r] + a[r:] + _dot(sc_hi, w_lo) + b_ref[...]


def _modulation(cvec, ada_w, ada_b):
    depth, d, n = ada_w.shape
    r = cvec.shape[0]
    tn = 1536
    assert n % tn == 0
    out = pl.pallas_call(
        _mod_kernel,
        grid=(depth, n // tn),
        in_specs=[
            pl.BlockSpec((r, d), lambda l, j: (0, 0)),
            pl.BlockSpec((None, d, tn), lambda l, j: (l, 0, j)),
            pl.BlockSpec((None, 1, tn), lambda l, j: (l, 0, j)),
        ],
        out_specs=pl.BlockSpec((None, r, tn), lambda l, j: (l, 0, j)),
        out_shape=jax.ShapeDtypeStruct((depth, r, n), F32),
        compiler_params=_params(2),
        name="adaln_modulation",
    )(cvec, ada_w, ada_b.reshape(depth, 1, n))
    return out.reshape(depth, r, N_MOD, d)


def _mod_spec(d, tm, rows_per_mod, mod_off):
    return pl.BlockSpec((None, N_MOD, d), lambda i: (mod_off + (i * tm) // rows_per_mod, 0, 0))


class _Section:
    def __init__(self, body, inputs=(), outputs=(), writes_x=False):
        self.body, self.inputs, self.outputs, self.writes_x = body, list(inputs), list(outputs), writes_x


def _stage_kernel(*refs, sections, writes_x):
    x_ref, mod_ref = refs[0], refs[1]
    n_in = sum(len(s.inputs) for s in sections)
    in_refs = refs[2:2 + n_in]
    out_refs = refs[2 + n_in:]
    x = x_ref[...]
    xo_ref = None
    if writes_x:
        xo_ref, out_refs = out_refs[0], out_refs[1:]
    i0 = o0 = 0
    for s in sections:
        x = s.body(x, mod_ref, in_refs[i0:i0 + len(s.inputs)], out_refs[o0:o0 + len(s.outputs)])
        i0 += len(s.inputs)
        o0 += len(s.outputs)
    if writes_x:
        xo_ref[...] = x


def _token_stage(x, mod, rows_per_mod, mod_off, sections, name, seq=None, tm=512):
    t, d = x.shape
    tm = min(tm, t)
    assert t % tm == 0 and rows_per_mod % tm == 0
    writes_x = any(s.writes_x for s in sections)
    in_specs = [pl.BlockSpec((tm, d), lambda i: (i, 0)), _mod_spec(d, tm, rows_per_mod, mod_off)]
    args = [x, mod]
    for s in sections:
        for arr, kind in s.inputs:
            if kind == "res":
                in_specs.append(_resident(arr.shape))
            elif isinstance(kind, tuple):
                in_specs.append(_resident(arr.shape, kind[1]))
            elif kind == "row":
                in_specs.append(pl.BlockSpec((tm, arr.shape[1]), lambda i: (i, 0)))
            elif kind == "head":
                in_specs.append(pl.BlockSpec((arr.shape[0], tm, arr.shape[2]), lambda i: (0, i, 0)))
            else:
                assert kind == "rope" and seq % tm == 0
                n_rt = seq // tm
                in_specs.append(pl.BlockSpec((tm, arr.shape[1]), lambda i: (i % n_rt, 0)))
            args.append(arr)
    out_specs, out_shape = [], []
    if writes_x:
        out_specs.append(pl.BlockSpec((tm, d), lambda i: (i, 0)))
        out_shape.append(jax.ShapeDtypeStruct((t, d), F32))
    for s in sections:
        for w, kind in s.outputs:
            if kind == "row":
                out_specs.append(pl.BlockSpec((tm, w), lambda i: (i, 0)))
                out_shape.append(jax.ShapeDtypeStruct((t, w), BF16))
            elif kind == "head":
                heads, hw = w
                out_specs.append(pl.BlockSpec((heads, tm, hw), lambda i: (0, i, 0)))
                out_shape.append(jax.ShapeDtypeStruct((heads, t, hw), BF16))
            else:
                out_specs.append(pl.BlockSpec((w, tm), lambda i: (0, i)))
                out_shape.append(jax.ShapeDtypeStruct((w, t), BF16))
    return pl.pallas_call(
        functools.partial(_stage_kernel, sections=sections, writes_x=writes_x),
        grid=(t // tm,),
        in_specs=in_specs,
        out_specs=out_specs,
        out_shape=out_shape,
        compiler_params=_params(1),
        name=name,
    )(*args)


FFN_CHUNKS = 1


def _sec_half_ffn(k0, g, wg, wu, wd, lead):
    def body(x, mod_ref, in_refs, out_refs):
        g_ref, wg_ref, wu_ref, wd_ref = in_refs
        hb = _norm_mod(x, g_ref[...], mod_ref, k0).astype(BF16)
        tf = wg_ref.shape[1] // FFN_CHUNKS
        y = None
        for j in range(FFN_CHUNKS):
            a = _dot(hb, wg_ref[:, j * tf:(j + 1) * tf])
            u = _dot(hb, wu_ref[:, j * tf:(j + 1) * tf])
            act = ((a * jax.nn.sigmoid(a)) * u).astype(BF16)
            yj = _dot(act, wd_ref[j * tf:(j + 1) * tf, :])
            y = yj if y is None else y + yj
        return x + (0.5 * mod_ref[k0 + 2:k0 + 3, :]) * y

    assert wg.shape[-1] % (FFN_CHUNKS * LANES) == 0
    at = ("res_at", tuple(lead))
    return _Section(body, [(g.reshape(1, -1), "res"), (wg, at), (wu, at), (wd, at)], writes_x=True)


def _sec_outproj(w, ys):
    def body(x, mod_ref, in_refs, out_refs):
        w_ref, *y_refs = in_refs
        pieces = []
        for y_ref in y_refs:
            if len(y_ref.shape) == 3:
                pieces += [y_ref[h] for h in range(y_ref.shape[0])]
            else:
                pieces.append(y_ref[...])
        return x + mod_ref[5:6, :] * _dot(jnp.concatenate(pieces, axis=1), w_ref[...])

    return _Section(body, [(w, "res")] + [(y, "head" if y.ndim == 3 else "row") for y in ys], writes_x=True)


def _sec_final_norm(g):
    def body(x, mod_ref, in_refs, out_refs):
        return _rms(x) * in_refs[0][...]

    return _Section(body, [(g.reshape(1, -1), "res")], writes_x=True)


def _proj_ab_body(x, mod_ref, in_refs, out_refs):
    g_ref, w_ref, wvt_ref, cc_ref, cs_ref = in_refs
    uc_ref, us_ref, q_ref, k_ref, vt_ref = out_refs
    hb = _norm_mod(x, g_ref[...], mod_ref, 3).astype(BF16)
    z = _dot(hb, w_ref[...])
    u = z[:, :FNET_DIM].astype(BF16)
    for g in range(FNET_GROUPS):
        sl = slice(g * FNET_GROUP_DIM, (g + 1) * FNET_GROUP_DIM)
        uc_ref[:, sl] = _dot(u[:, sl], cc_ref[...]).astype(BF16)
        us_ref[:, sl] = _dot(u[:, sl], cs_ref[...]).astype(BF16)
    for hp in range(NA_DIM // LANES):
        q0, k0 = FNET_DIM + hp * LANES, FNET_DIM + NA_DIM + hp * LANES
        q_ref[hp] = (z[:, q0:q0 + LANES] * (NA_HEAD_DIM ** -0.5 * LOG2_E)).astype(BF16)
        k_ref[hp] = z[:, k0:k0 + LANES].astype(BF16)
    vt_ref[...] = _dot_nt(wvt_ref[...], hb).astype(BF16)
    return x


def _dft_real_imag(n, scale):
    j = np.arange(n, dtype=np.int64)
    ang = 2.0 * np.pi * ((j[:, None] * j[None, :]) % n).astype(np.float64) / n
    return (np.cos(ang) * scale).astype(np.float32), (np.sin(ang) * scale).astype(np.float32)


def _sec_proj_ab(g, w_in):
    cc, cs = _dft_real_imag(FNET_GROUP_DIM, FNET_GROUP_DIM ** -0.5)
    n_row = FNET_DIM + 2 * NA_DIM
    w_row = w_in[:, :n_row].astype(BF16)
    w_vt = w_in[:, n_row:].T.astype(BF16)
    inputs = [(g.reshape(1, -1), "res"), (w_row, "res"), (w_vt, "res"),
              (jnp.asarray(cc).astype(BF16), "res"), (jnp.asarray(cs).astype(BF16), "res")]
    pairs = ((NA_DIM // LANES, LANES), "head")
    outputs = [(FNET_DIM, "row"), (FNET_DIM, "row"), pairs, pairs, (NA_DIM, "col")]
    return _Section(_proj_ab_body, inputs, outputs)


def _fnet_kernel(cn_ref, sn_ref, uc_ref, us_ref, o_ref):
    o = _dot(cn_ref[...], uc_ref[...]) - _dot(sn_ref[...], us_ref[...])
    o_ref[...] = o.astype(BF16)


def _position_dft_matrices(n):
    if n <= 512:
        c, s = _dft_real_imag(n, n ** -0.5)
        return jnp.asarray(c).astype(BF16), jnp.asarray(s).astype(BF16)
    n1 = GRID_W
    n2 = n // n1
    k = np.arange(n, dtype=np.int64)
    hi = np.arange(n2, dtype=np.int64)[:, None] * n1
    lo = np.arange(n1, dtype=np.int64)[:, None]
    ang_hi = 2.0 * np.pi * ((hi * k[None, :]) % n).astype(np.float64) / n
    ang_lo = 2.0 * np.pi * ((lo * k[None, :]) % n).astype(np.float64) / n
    ch, sh = jnp.asarray(np.cos(ang_hi), F32)[:, None, :], jnp.asarray(np.sin(ang_hi), F32)[:, None, :]
    cl, sl = jnp.asarray(np.cos(ang_lo), F32)[None, :, :], jnp.asarray(np.sin(ang_lo), F32)[None, :, :]
    scale = n ** -0.5
    c = ((ch * cl - sh * sl) * scale).reshape(n, n).astype(BF16)
    s = ((sh * cl + ch * sl) * scale).reshape(n, n).astype(BF16)
    return c, s


def _fnet_positions(uc, us, batch):
    t, w = uc.shape
    n = t // batch
    cn, sn = _position_dft_matrices(n)
    tn = min(512, n)
    nt = n // tn
    return pl.pallas_call(
        _fnet_kernel,
        grid=(nt, batch),
        in_specs=[
            pl.BlockSpec((tn, n), lambda i, b: (i, 0)),
            pl.BlockSpec((tn, n), lambda i, b: (i, 0)),
            pl.BlockSpec((n, w), lambda i, b: (b, 0)),
            pl.BlockSpec((n, w), lambda i, b: (b, 0)),
        ],
        out_specs=pl.BlockSpec((tn, w), lambda i, b: (b * nt + i, 0)),
        out_shape=jax.ShapeDtypeStruct((t, w), BF16),
        compiler_params=_params(2),
        name="fnet_positions",
    )(cn, sn, uc, us)


def _na_geometry(rows):
    n_blocks = rows // NA_ROWS_PER_BLOCK
    kh = min(NA_KH, rows)
    variants, vi, start_blk = [], [], []
    for blk in range(n_blocks):
        start = int(np.clip(NA_ROWS_PER_BLOCK * blk - kh // 2, 0, rows - NA_WIN_ROWS))
        table = []
        for a in range(NA_ROWS_PER_BLOCK):
            r = NA_ROWS_PER_BLOCK * blk + a
            r0 = int(np.clip(r - kh // 2, 0, rows - kh))
            table.append(tuple((start + i) - r + (NA_KH - 1) if 0 <= (start + i) - r0 < kh else -1
                               for i in range(NA_WIN_ROWS)))
            assert sum(e >= 0 for e in table[-1]) == kh
        table = tuple(table)
        if table not in variants:
            variants.append(table)
        vi.append(variants.index(table))
        start_blk.append(start * GRID_W // NA_KBLK)
    return vi, start_blk, variants


def _na_pair_windows(variants):
    lows = []
    for table in variants:
        for j in range(NA_ROWS_PER_BLOCK // 2):
            valid = [i for i in range(NA_WIN_ROWS) if table[2 * j][i] >= 0 or table[2 * j + 1][i] >= 0]
            lo = min(min(valid), NA_WIN_ROWS - NA_PAIR_ROWS)
            assert max(valid) < lo + NA_PAIR_ROWS
            lows.append(lo)
    return lows


def _na_bias_kernel(rpb_ref, o_ref, *, variants, pair_lows):
    h = pl.program_id(0)
    n_ro = 2 * NA_KH - 1
    n_co = 2 * NA_KW - 1
    kc = lax.broadcasted_iota(jnp.int32, (GRID_W, LANES), 0)
    lane = lax.broadcasted_iota(jnp.int32, (GRID_W, LANES), 1)
    c = lane & (GRID_W - 1)
    co = kc - c + (NA_KW - 1)
    c0 = jnp.clip(c - NA_KW // 2, 0, GRID_W - NA_KW)
    win = (kc >= c0) & (kc < c0 + NA_KW)
    neg = jnp.full((GRID_W, LANES), NEG_BIG, F32)
    tiles = []
    for ro in range(n_ro):
        t = neg
        for tt in range(n_co):
            t = jnp.where(win & (co == tt), rpb_ref[(h * n_ro + ro) * n_co + tt] * LOG2_E, t)
        tiles.append(t)
    left = lane < GRID_W
    for vv, table in enumerate(variants):
        for j in range(NA_ROWS_PER_BLOCK // 2):
            lo = pair_lows[vv * (NA_ROWS_PER_BLOCK // 2) + j]
            for ii in range(NA_PAIR_ROWS):
                r1, r2 = table[2 * j][lo + ii], table[2 * j + 1][lo + ii]
                t1 = tiles[r1] if r1 >= 0 else neg
                t2 = tiles[r2] if r2 >= 0 else neg
                o_ref[vv, j, ii * GRID_W:(ii + 1) * GRID_W, :] = jnp.where(left, t1, t2)


def _na_bias_tables(rpb, variants, pair_lows):
    heads = rpb.shape[0]
    n_pairs = NA_ROWS_PER_BLOCK // 2
    nk = NA_PAIR_ROWS * GRID_W
    return pl.pallas_call(
        functools.partial(_na_bias_kernel, variants=variants, pair_lows=pair_lows),
        grid=(heads,),
        in_specs=[pl.BlockSpec(memory_space=pltpu.SMEM)],
        out_specs=pl.BlockSpec((len(variants), None, n_pairs, nk, LANES), lambda h: (0, h, 0, 0, 0)),
        out_shape=jax.ShapeDtypeStruct((len(variants), heads, n_pairs, nk, LANES), F32),
        compiler_params=_params(1),
        name="na_bias_tables",
    )(rpb.reshape(-1))


def _pair_masks():
    lane = lax.broadcasted_iota(jnp.int32, (1, LANES), 1)
    return [(lane // NA_HEAD_DIM == hh).astype(BF16) for hh in range(LANES // NA_HEAD_DIM)]


def _na_body(q_ref, k_ref, vt_ref, kc_ref, vc_ref, bias_ref, o_ref, lows):
    span = NA_PAIR_ROWS * GRID_W
    nk = NA_WIN_ROWS * GRID_W
    q = q_ref[...]
    masks = _pair_masks()

    def scores(hh):
        qh = q * masks[hh]
        st = _dot_nt(k_ref[...], qh)
        return st, _dot_nt(kc_ref[...], qh)

    outs = []
    st, sc = scores(0)
    for hh in range(len(masks)):
        nxt = scores(hh + 1) if hh + 1 < len(masks) else None
        ps, pcs, ls = [], [], []
        for j, lo in enumerate(lows):
            lanes = slice(j * LANES, (j + 1) * LANES)
            r0 = lo * GRID_W
            sj = st[r0:r0 + span, lanes] + bias_ref[hh, j]
            scj = sc[:, lanes]
            m = jnp.maximum(jnp.max(sj, axis=0, keepdims=True), jnp.max(scj, axis=0, keepdims=True))
            pj = jnp.exp2(sj - m)
            pcj = jnp.exp2(scj - m)
            ls.append(jnp.sum(pj, axis=0, keepdims=True) + jnp.sum(pcj, axis=0, keepdims=True))
            pieces = [jnp.zeros((r0, LANES), BF16), pj.astype(BF16), jnp.zeros((nk - r0 - span, LANES), BF16)]
            ps.append(jnp.concatenate([x for x in pieces if x.shape[0]], axis=0))
            pcs.append(pcj.astype(BF16))
        p = jnp.concatenate(ps, axis=1)
        pc = jnp.concatenate(pcs, axis=1)
        rows = slice(hh * NA_HEAD_DIM, (hh + 1) * NA_HEAD_DIM)
        acc = _dot(jnp.concatenate([vt_ref[rows, :], vc_ref[rows, :]], axis=1),
                   jnp.concatenate([p, pc], axis=0))
        outs.append(acc / jnp.concatenate(ls, axis=1))
        if nxt is not None:
            st, sc = nxt
    o_ref[...] = jnp.transpose(jnp.concatenate(outs, axis=0)).astype(BF16)


def _na_kernel(vi_ref, sb_ref, q_ref, k_ref, vt_ref, kc_ref, vc_ref, bias_ref, o_ref, *, pair_lows):
    del sb_ref
    variant = vi_ref[pl.program_id(1)]
    n_lp = NA_ROWS_PER_BLOCK // 2
    for vv in range(len(pair_lows) // n_lp):
        pl.when(variant == vv)(functools.partial(
            _na_body, q_ref, k_ref, vt_ref, kc_ref, vc_ref, bias_ref, o_ref, pair_lows[vv * n_lp:(vv + 1) * n_lp]))


def _neighbourhood_attention(q, k, vt, kc, vct, bias, vi, start_blk, pair_lows, batch):
    n_pairs, t, _ = q.shape
    s = t // batch
    rows = s // GRID_W
    nq = NA_ROWS_PER_BLOCK * GRID_W
    nk = NA_WIN_ROWS * GRID_W
    n_blocks = rows // NA_ROWS_PER_BLOCK
    lc = kc.shape[1] // batch
    kblk_per_batch = s // NA_KBLK
    heads_per_pair = LANES // NA_HEAD_DIM
    bias6 = bias.reshape(bias.shape[0], n_pairs, heads_per_pair, *bias.shape[2:])

    def win_start(b, i, sb_r):
        return (b * kblk_per_batch + sb_r[i]) * NA_KBLK

    grid_spec = pltpu.PrefetchScalarGridSpec(
        num_scalar_prefetch=2,
        grid=(n_pairs, n_blocks, batch),
        in_specs=[pl.BlockSpec((None, nq, LANES), lambda hp, i, b, vi_r, sb_r: (hp, b * n_blocks + i, 0)),
                  pl.BlockSpec((None, pl.Element(nk), pl.Element(LANES)),
                               lambda hp, i, b, vi_r, sb_r: (hp, win_start(b, i, sb_r), 0)),
                  pl.BlockSpec((pl.Element(LANES), pl.Element(nk)),
                               lambda hp, i, b, vi_r, sb_r: (hp * LANES, win_start(b, i, sb_r)))]
        + [pl.BlockSpec((None, lc, LANES), lambda hp, i, b, vi_r, sb_r: (hp, b, 0)),
           pl.BlockSpec((LANES, lc), lambda hp, i, b, vi_r, sb_r: (hp, b)),
           pl.BlockSpec((None, None, heads_per_pair, *bias.shape[2:]),
                        lambda hp, i, b, vi_r, sb_r: (vi_r[i], hp, 0, 0, 0, 0))],
        out_specs=pl.BlockSpec((None, nq, LANES), lambda hp, i, b, vi_r, sb_r: (hp, b * n_blocks + i, 0)),
    )
    return pl.pallas_call(
        functools.partial(_na_kernel, pair_lows=tuple(pair_lows)),
        grid_spec=grid_spec,
        out_shape=jax.ShapeDtypeStruct((n_pairs, t, LANES), BF16),
        compiler_params=_params(3),
        name="neighbourhood_attention",
    )(jnp.asarray(vi, jnp.int32), jnp.asarray(start_blk, jnp.int32),
      q, k, vt, kc, vct, bias6)


def _ctx_attn_kernel(q_ref, k_ref, vt_ref, o_ref):
    q = q_ref[...]
    k = k_ref[...]
    outs = []
    for hh, hm in enumerate(_pair_masks()):
        st = _dot_nt(k, q * hm)
        m = jnp.max(st, axis=0, keepdims=True)
        p = jnp.exp2(st - m)
        l = jnp.sum(p, axis=0, keepdims=True)
        outs.append(_dot(vt_ref[hh * NA_HEAD_DIM:(hh + 1) * NA_HEAD_DIM, :], p.astype(BF16)) / l)
    o_ref[...] = jnp.transpose(jnp.concatenate(outs, axis=0)).astype(BF16)


def _ctx_dense_attention(q, k, vt, batch):
    n_pairs, t, _ = q.shape
    lc = t // batch
    spec = pl.BlockSpec((None, lc, LANES), lambda b, hp: (hp, b, 0))
    return pl.pallas_call(
        _ctx_attn_kernel,
        grid=(batch, n_pairs),
        in_specs=[spec, spec, pl.BlockSpec((LANES, lc), lambda b, hp: (hp, b))],
        out_specs=spec,
        out_shape=jax.ShapeDtypeStruct((n_pairs, t, LANES), BF16),
        compiler_params=_params(2),
        name="ctx_dense_attention",
    )(q, k, vt)


MLA_QK_PAD = 256
MLA_LOOKAHEAD = 1


def _rope(x, cos, sin_next, sin_prev):
    quarter = MLA_ROPE // 4
    return (x * cos + pltpu.roll(x, LANES - quarter, 1) * sin_next + pltpu.roll(x, quarter, 1) * sin_prev)


def _proj_mla_body(x, mod_ref, in_refs, out_refs, *, rope, need_q):
    g_ref, win_ref, gq_ref, gkv_ref, wqn_ref, wqr_ref, wuk_ref, wuv_ref, *rope_refs = in_refs
    if rope:
        cos_ref, sa_ref, sb_ref = rope_refs
    if need_q:
        q_ref, k_ref, v_ref = out_refs
    else:
        k_ref, v_ref = out_refs
    hb = _norm_mod(x, g_ref[...], mod_ref, 3).astype(BF16)
    z = _dot(hb, win_ref[...])
    ckv = (_rms(z[:, MLA_Q_RANK:MLA_Q_RANK + MLA_KV_RANK]) * gkv_ref[...]).astype(BF16)
    kr = z[:, MLA_Q_RANK + MLA_KV_RANK:]
    if rope:
        cos, sa, sb = cos_ref[...], sa_ref[...], sb_ref[...]
        kr = _rope(kr, cos, sa, sb)
    kr = kr.astype(BF16)
    kn = _dot(ckv, wuk_ref[...])
    v_ref[...] = _dot_nt(wuv_ref[...], ckv).astype(BF16)
    for h in range(MLA_HEADS):
        k_ref[h, :, :LANES] = kn[:, h * LANES:(h + 1) * LANES].astype(BF16)
        k_ref[h, :, LANES:] = kr
    if need_q:
        cq = (_rms(z[:, :MLA_Q_RANK]) * gq_ref[...]).astype(BF16)
        scale = (MLA_NOPE + MLA_ROPE) ** -0.5 * LOG2_E
        qn = _dot(cq, wqn_ref[...])
        qr = _dot(cq, wqr_ref[...])
        for h in range(MLA_HEADS):
            qrh = qr[:, h * LANES:(h + 1) * LANES]
            if rope:
                qrh = _rope(qrh, cos, sa, sb)
            q_ref[h, :, :LANES] = (qn[:, h * LANES:(h + 1) * LANES] * scale).astype(BF16)
            q_ref[h, :, LANES:] = (qrh * scale).astype(BF16)
    return x


def _rope_tables(n_tok):
    pos = jnp.arange(n_tok)
    row = (pos // GRID_W).astype(F32)
    col = (pos % GRID_W).astype(F32)
    half = MLA_ROPE // 2
    inv = ROPE_BASE ** (-jnp.arange(0, half, 2, dtype=F32) / half)
    ang_r = row[:, None] * inv[None]
    ang_c = col[:, None] * inv[None]
    ang = jnp.concatenate([ang_r, ang_r, ang_c, ang_c], axis=-1)
    cos, sin = jnp.cos(ang), jnp.sin(ang)
    quarter = MLA_ROPE // 4
    even_q = ((jnp.arange(MLA_ROPE) // quarter) % 2 == 0)[None, :]
    pad = LANES - MLA_ROPE
    cos_p = jnp.pad(cos, ((0, 0), (0, pad)), constant_values=1.0)
    sin_next = jnp.pad(jnp.where(even_q, -sin, 0.0), ((0, 0), (0, pad)))
    sin_prev = jnp.pad(jnp.where(even_q, 0.0, sin), ((0, 0), (0, pad)))
    return cos_p, sin_next, sin_prev


def _sec_proj_mla(g, w, rope_tabs, need_q):
    rope = rope_tabs is not None
    inputs = [(g.reshape(1, -1), "res")] + [(a, "res") for a in w]
    if rope:
        inputs += [(tab, "rope") for tab in rope_tabs]
    qk = ((MLA_HEADS, MLA_QK_PAD), "head")
    outputs = ([qk] if need_q else []) + [qk, (MLA_HEADS * MLA_V, "col")]
    return _Section(functools.partial(_proj_mla_body, rope=rope, need_q=need_q), inputs, outputs)


def _mla_attn_kernel(q_ref, *rest, n_src, chunk):
    k_refs, vt_refs, o_ref = rest[:n_src], rest[n_src:2 * n_src], rest[2 * n_src]
    q = q_ref[...]
    chunks = [(k_ref, vt_ref, c0, min(k_ref.shape[0], c0 + chunk))
              for k_ref, vt_ref in zip(k_refs, vt_refs) for c0 in range(0, k_ref.shape[0], chunk)]

    def scores(j):
        k_ref, _, c0, c1 = chunks[j]
        return _dot_nt(k_ref[c0:c1, :], q)

    m = l = acc = None
    ahead =[scores(j) for j in range(min(MLA_LOOKAHEAD, len(chunks)))]
    for j, (_, vt_ref, c0, c1) in enumerate(chunks):
        if j + MLA_LOOKAHEAD < len(chunks):
            ahead.append(scores(j + MLA_LOOKAHEAD))
        st = ahead.pop(0)
        mj = jnp.max(st, axis=0, keepdims=True)
        if m is None:
            m_new = mj
        else:
            m_new = jnp.maximum(m, mj)
            alpha = jnp.exp2(m - m_new)
        p = jnp.exp2(st - m_new)
        lj = jnp.sum(p, axis=0, keepdims=True)
        oj = _dot(vt_ref[:, c0:c1], p.astype(BF16))
        if m is None:
            l, acc = lj, oj
        else:
            l, acc = alpha * l + lj, alpha * acc + oj
        m = m_new
    o_ref[...] = jnp.transpose(acc / l).astype(BF16)


def _mla_attention(q, sources, batch, tq=1024, chunk=1024):
    t = q.shape[1]
    s = t // batch
    tq = min(tq, s)
    nq = s // tq
    k_specs, v_specs = [], []
    for k, _ in sources:
        n = k.shape[1] // batch
        k_specs.append(pl.BlockSpec((None, n, MLA_QK_PAD), lambda b, h, i: (h, b, 0)))
        v_specs.append(pl.BlockSpec((MLA_V, n), lambda b, h, i: (h, b)))
    return pl.pallas_call(
        functools.partial(_mla_attn_kernel, n_src=len(sources), chunk=chunk),
        grid=(batch, MLA_HEADS, nq),
        in_specs=[pl.BlockSpec((None, tq, MLA_QK_PAD), lambda b, h, i: (h, b * nq + i, 0))] + k_specs + v_specs,
        out_specs=pl.BlockSpec((None, tq, MLA_V), lambda b, h, i: (h, b * nq + i, 0)),
        out_shape=jax.ShapeDtypeStruct((MLA_HEADS, t, MLA_V), BF16),
        compiler_params=_params(3),
        name="mla_attention",
    )(q, *[k for k, _ in sources], *[v for _, v in sources])


def _mla_weights(w_in, g_q, g_kv, w_uq, w_uk, w_uv):
    w_in_p = jnp.pad(w_in, ((0, 0), (0, 5 * LANES - w_in.shape[1]))).astype(BF16)
    wq = w_uq.reshape(MLA_Q_RANK, MLA_HEADS, MLA_NOPE + MLA_ROPE)
    wqn = wq[:, :, :MLA_NOPE].reshape(MLA_Q_RANK, MLA_HEADS * MLA_NOPE).astype(BF16)
    wqr = jnp.pad(wq[:, :, MLA_NOPE:], ((0, 0), (0, 0), (0, LANES - MLA_ROPE)))
    wqr = wqr.reshape(MLA_Q_RANK, MLA_HEADS * LANES).astype(BF16)
    return (w_in_p, g_q.reshape(1, -1), g_kv.reshape(1, -1), wqn, wqr, w_uk.astype(BF16), w_uv.T.astype(BF16))


def kernel(x, c, ctx, c_ctx, ada_w, ada_b, norm_g, ffn_w_gate, ffn_w_up, ffn_w_down, ab_w_in, ab_rpb, ab_w_out,
           mla_w_in, mla_g_q, mla_g_kv, mla_w_uq, mla_w_uk, mla_w_uv, mla_w_o, final_g):
    batch, seq, d = x.shape
    lc = ctx.shape[1]
    depth = ada_w.shape[0]
    assert seq % (GRID_W * NA_WIN_ROWS) == 0 and lc % LANES == 0
    t_lat, t_ctx = batch * seq, batch * lc

    n_rows = -(-(batch + 1) // 8) * 8
    cvec = jnp.concatenate([c, c_ctx[None, :], jnp.zeros((n_rows - batch - 1, d), F32)], axis=0)
    mod = _modulation(cvec, ada_w, ada_b)

    x_l = x.reshape(t_lat, d)
    x_c = ctx.reshape(t_ctx, d)
    lat = dict(rows_per_mod=seq, mod_off=0)
    cx = dict(rows_per_mod=t_ctx, mod_off=batch)

    wg, wu, wd = ffn_w_gate.astype(BF16), ffn_w_up.astype(BF16), ffn_w_down.astype(BF16)
    rope_tabs = None

    for layer in range(depth):
        last = layer == depth - 1
        m = mod[layer]
        g = norm_g[layer]
        ffn1 = _sec_half_ffn(0, g[0], wg, wu, wd, (layer, 0))
        ffn2 = _sec_half_ffn(6, g[2], wg, wu, wd, (layer, 1))
        i = layer // 2
        if layer % 2 == 0:
            proj = _sec_proj_ab(g[1], ab_w_in[i])
            w_out = ab_w_out[i].astype(BF16)
            x_l, uc_l, us_l, q_l, k_l, vt_l = _token_stage(x_l, m, sections=[ffn1, proj], name="ffn_proj_ab", **lat)
            x_c, uc_c, us_c, q_c, k_c, vt_c = _token_stage(x_c, m, sections=[ffn1, proj], name="ffn_proj_ab", **cx)
            a_l = _fnet_positions(uc_l, us_l, batch)
            vi, start_blk, variants = _na_geometry(seq // GRID_W)
            pair_lows = _na_pair_windows(variants)
            bias = _na_bias_tables(ab_rpb[i], variants, pair_lows)
            b_l = _neighbourhood_attention(q_l, k_l, vt_l, k_c, vt_c, bias, vi, start_blk, pair_lows, batch)
            ys_l = [a_l, b_l]
            if not last:
                a_c = _fnet_positions(uc_c, us_c, batch)
                b_c = _ctx_dense_attention(q_c, k_c, vt_c, batch)
                ys_c = [a_c, b_c]
        else:
            if rope_tabs is None:
                rope_tabs = _rope_tables(seq)
            w = _mla_weights(mla_w_in[i], mla_g_q[i], mla_g_kv[i], mla_w_uq[i], mla_w_uk[i], mla_w_uv[i])
            w_out = mla_w_o[i].astype(BF16)
            x_l, q_l, k_l, vt_l = _token_stage(
                x_l, m, sections=[ffn1, _sec_proj_mla(g[1], w, rope_tabs, True)], name="ffn_proj_mla", seq=seq, **lat)
            x_c, *q_c, k_c, vt_c = _token_stage(
                x_c, m, sections=[ffn1, _sec_proj_mla(g[1], w, None, not last)], name="ffn_proj_mla", **cx)
            ys_l = [_mla_attention(q_l, [(k_l, vt_l), (k_c, vt_c)], batch)]
            if not last:
                ys_c = [_mla_attention(q_c[0], [(k_c, vt_c)], batch)]
        tail = [_sec_final_norm(final_g)] if last else []
        (x_l,) = _token_stage(x_l, m, sections=[_sec_outproj(w_out, ys_l), ffn2] + tail, name="outproj_ffn", **lat)
        if not last:
            (x_c,) = _token_stage(x_c, m, sections=[_sec_outproj(w_out, ys_c), ffn2], name="outproj_ffn", **cx)
    return x_l.reshape(batch, seq, d)
```

```python
import functools
import math

import numpy as np
import jax
import jax.numpy as jnp
from jax import lax
from jax.experimental import pallas as pl
from jax.experimental.pallas import tpu as pltpu

F32 = jnp.float32
BF16 = jnp.bfloat16

GRID_W = 64
EPS = 1e-6
N_MOD = 9
FNET_GROUPS = 4
FNET_GROUP_DIM = 128
FNET_DIM = FNET_GROUPS * FNET_GROUP_DIM
NA_HEADS = 8
NA_HEAD_DIM = 64
NA_DIM = NA_HEADS * NA_HEAD_DIM
NA_KH = 8
NA_KW = 16
MLA_HEADS = 8
MLA_NOPE = 128
MLA_ROPE = 64
MLA_V = 128
MLA_Q_RANK = 384
MLA_KV_RANK = 128
ROPE_BASE = 10000.0

LANES = 128
VMEM_LIMIT_BYTES = 56 * 1024 * 1024
NEG_BIG = -1e30
LOG2_E = math.log2(math.e)

NA_ROWS_PER_BLOCK = 8
NA_WIN_ROWS = 16
NA_KBLK = 256
NA_PAIR_ROWS = NA_KH + 1


def _params(n_axes):
    return pltpu.CompilerParams(
        dimension_semantics=("arbitrary",) * n_axes,
        vmem_limit_bytes=VMEM_LIMIT_BYTES,
    )


def _resident(shape, lead=()):
    nd = len(shape) - len(lead)
    return pl.BlockSpec((None,) * len(lead) + tuple(shape[len(lead):]), lambda *_: tuple(lead) + (0,) * nd,
                        pipeline_mode=pl.Buffered(1))


def _dot(a, b):
    return jnp.dot(a, b, preferred_element_type=F32)


def _dot_nt(a, b):
    return lax.dot_general(a, b, (((1,), (1,)), ((), ())), preferred_element_type=F32)


def _rms(x):
    return x * lax.rsqrt(jnp.mean(x * x, axis=-1, keepdims=True) + EPS)


def _norm_mod(x, g, mod_ref, k0):
    shift = mod_ref[k0:k0 + 1, :]
    scale = mod_ref[k0 + 1:k0 + 2, :]
    return (_rms(x) * g) * (1.0 + scale) + shift


def _mod_kernel(c_ref, w_ref, b_ref, o_ref):
    c = c_ref[...]
    sc = c * jax.nn.sigmoid(c)
    w = w_ref[...]
    sc_hi, w_hi = sc.astype(BF16), w.astype(BF16)
    sc_lo = (sc - sc_hi.astype(F32)).astype(BF16)
    w_lo = (w - w_hi.astype(F32)).astype(BF16)
    r = sc.shape[0]
    a = _dot(jnp.concatenate([sc_hi, sc_lo], axis=0), w_hi)
    o_ref[...] = a[:r] + a[r:] + _dot(sc_hi, w_lo) + b_ref[...]


def _modulation(cvec, ada_w, ada_b):
    depth, d, n = ada_w.shape
    r = cvec.shape[0]
    tn = 1536
    assert n % tn == 0
    out = pl.pallas_call(
        _mod_kernel,
        grid=(depth, n // tn),
        in_specs=[
            pl.BlockSpec((r, d), lambda l, j: (0, 0)),
            pl.BlockSpec((None, d, tn), lambda l, j: (l, 0, j)),
            pl.BlockSpec((None, 1, tn), lambda l, j: (l, 0, j)),
        ],
        out_specs=pl.BlockSpec((None, r, tn), lambda l, j: (l, 0, j)),
        out_shape=jax.ShapeDtypeStruct((depth, r, n), F32),
        compiler_params=_params(2),
        name="adaln_modulation",
    )(cvec, ada_w, ada_b.reshape(depth, 1, n))
    return out.reshape(depth, r, N_MOD, d)


def _mod_spec(d, tm, rows_per_mod, mod_off):
    return pl.BlockSpec((None, N_MOD, d), lambda i: (mod_off + (i * tm) // rows_per_mod, 0, 0))


class _Section:
    def __init__(self, body, inputs=(), outputs=(), writes_x=False):
        self.body, self.inputs, self.outputs, self.writes_x = body, list(inputs), list(outputs), writes_x


def _stage_kernel(*refs, sections, writes_x):
    x_ref, mod_ref = refs[0], refs[1]
    n_in = sum(len(s.inputs) for s in sections)
    in_refs = refs[2:2 + n_in]
    out_refs = refs[2 + n_in:]
    x = x_ref[...]
    xo_ref = None
    if writes_x:
        xo_ref, out_refs = out_refs[0], out_refs[1:]
    i0 = o0 = 0
    for s in sections:
        x = s.body(x, mod_ref, in_refs[i0:i0 + len(s.inputs)], out_refs[o0:o0 + len(s.outputs)])
        i0 += len(s.inputs)
        o0 += len(s.outputs)
    if writes_x:
        xo_ref[...] = x


def _token_stage(x, mod, rows_per_mod, mod_off, sections, name, seq=None, tm=512):
    t, d = x.shape
    tm = min(tm, t)
    assert t % tm == 0 and rows_per_mod % tm == 0
    writes_x = any(s.writes_x for s in sections)
    in_specs = [pl.BlockSpec((tm, d), lambda i: (i, 0)), _mod_spec(d, tm, rows_per_mod, mod_off)]
    args = [x, mod]
    for s in sections:
        for arr, kind in s.inputs:
            if kind == "res":
                in_specs.append(_resident(arr.shape))
            elif isinstance(kind, tuple):
                in_specs.append(_resident(arr.shape, kind[1]))
            elif kind == "row":
                in_specs.append(pl.BlockSpec((tm, arr.shape[1]), lambda i: (i, 0)))
            elif kind == "head":
                in_specs.append(pl.BlockSpec((arr.shape[0], tm, arr.shape[2]), lambda i: (0, i, 0)))
            else:
                assert kind == "rope" and seq % tm == 0
                n_rt = seq // tm
                in_specs.append(pl.BlockSpec((tm, arr.shape[1]), lambda i: (i % n_rt, 0)))
            args.append(arr)
    out_specs, out_shape = [], []
    if writes_x:
        out_specs.append(pl.BlockSpec((tm, d), lambda i: (i, 0)))
        out_shape.append(jax.ShapeDtypeStruct((t, d), F32))
    for s in sections:
        for w, kind in s.outputs:
            if kind == "row":
                out_specs.append(pl.BlockSpec((tm, w), lambda i: (i, 0)))
                out_shape.append(jax.ShapeDtypeStruct((t, w), BF16))
            elif kind == "head":
                heads, hw = w
                out_specs.append(pl.BlockSpec((heads, tm, hw), lambda i: (0, i, 0)))
                out_shape.append(jax.ShapeDtypeStruct((heads, t, hw), BF16))
            else:
                out_specs.append(pl.BlockSpec((w, tm), lambda i: (0, i)))
                out_shape.append(jax.ShapeDtypeStruct((w, t), BF16))
    return pl.pallas_call(
        functools.partial(_stage_kernel, sections=sections, writes_x=writes_x),
        grid=(t // tm,),
        in_specs=in_specs,
        out_specs=out_specs,
        out_shape=out_shape,
        compiler_params=_params(1),
        name=name,
    )(*args)


FFN_CHUNKS = 1


def _sec_half_ffn(k0, g, wg, wu, wd, lead):
    def body(x, mod_ref, in_refs, out_refs):
        g_ref, wg_ref, wu_ref, wd_ref = in_refs
        hb = _norm_mod(x, g_ref[...], mod_ref, k0).astype(BF16)
        tf = wg_ref.shape[1] // FFN_CHUNKS
        y = None
        for j in range(FFN_CHUNKS):
            a = _dot(hb, wg_ref[:, j * tf:(j + 1) * tf])
            u = _dot(hb, wu_ref[:, j * tf:(j + 1) * tf])
            act = ((a * jax.nn.sigmoid(a)) * u).astype(BF16)
            yj = _dot(act, wd_ref[j * tf:(j + 1) * tf, :])
            y = yj if y is None else y + yj
        return x + (0.5 * mod_ref[k0 + 2:k0 + 3, :]) * y

    assert wg.shape[-1] % (FFN_CHUNKS * LANES) == 0
    at = ("res_at", tuple(lead))
    return _Section(body, [(g.reshape(1, -1), "res"), (wg, at), (wu, at), (wd, at)], writes_x=True)


def _sec_outproj(w, ys):
    def body(x, mod_ref, in_refs, out_refs):
        w_ref, *y_refs = in_refs
        pieces = []
        for y_ref in y_refs:
            if len(y_ref.shape) == 3:
                pieces += [y_ref[h] for h in range(y_ref.shape[0])]
            else:
                pieces.append(y_ref[...])
        return x + mod_ref[5:6, :] * _dot(jnp.concatenate(pieces, axis=1), w_ref[...])

    return _Section(body, [(w, "res")] + [(y, "head" if y.ndim == 3 else "row") for y in ys], writes_x=True)


def _sec_final_norm(g):
    def body(x, mod_ref, in_refs, out_refs):
        return _rms(x) * in_refs[0][...]

    return _Section(body, [(g.reshape(1, -1), "res")], writes_x=True)


def _proj_ab_body(x, mod_ref, in_refs, out_refs):
    g_ref, w_ref, wvt_ref, cc_ref, cs_ref = in_refs
    uc_ref, us_ref, q_ref, k_ref, vt_ref = out_refs
    hb = _norm_mod(x, g_ref[...], mod_ref, 3).astype(BF16)
    z = _dot(hb, w_ref[...])
    u = z[:, :FNET_DIM].astype(BF16)
    for g in range(FNET_GROUPS):
        sl = slice(g * FNET_GROUP_DIM, (g + 1) * FNET_GROUP_DIM)
        uc_ref[:, sl] = _dot(u[:, sl], cc_ref[...]).astype(BF16)
        us_ref[:, sl] = _dot(u[:, sl], cs_ref[...]).astype(BF16)
    for hp in range(NA_DIM // LANES):
        q0, k0 = FNET_DIM + hp * LANES, FNET_DIM + NA_DIM + hp * LANES
        q_ref[hp] = (z[:, q0:q0 + LANES] * (NA_HEAD_DIM ** -0.5 * LOG2_E)).astype(BF16)
        k_ref[hp] = z[:, k0:k0 + LANES].astype(BF16)
    vt_ref[...] = _dot_nt(wvt_ref[...], hb).astype(BF16)
    return x


def _dft_real_imag(n, scale):
    j = np.arange(n, dtype=np.int64)
    ang = 2.0 * np.pi * ((j[:, None] * j[None, :]) % n).astype(np.float64) / n
    return (np.cos(ang) * scale).astype(np.float32), (np.sin(ang) * scale).astype(np.float32)


def _sec_proj_ab(g, w_in):
    cc, cs = _dft_real_imag(FNET_GROUP_DIM, FNET_GROUP_DIM ** -0.5)
    n_row = FNET_DIM + 2 * NA_DIM
    w_row = w_in[:, :n_row].astype(BF16)
    w_vt = w_in[:, n_row:].T.astype(BF16)
    inputs = [(g.reshape(1, -1), "res"), (w_row, "res"), (w_vt, "res"),
              (jnp.asarray(cc).astype(BF16), "res"), (jnp.asarray(cs).astype(BF16), "res")]
    pairs = ((NA_DIM // LANES, LANES), "head")
    outputs = [(FNET_DIM, "row"), (FNET_DIM, "row"), pairs, pairs, (NA_DIM, "col")]
    return _Section(_proj_ab_body, inputs, outputs)


def _fnet_kernel(cn_ref, sn_ref, uc_ref, us_ref, o_ref):
    o = _dot(cn_ref[...], uc_ref[...]) - _dot(sn_ref[...], us_ref[...])
    o_ref[...] = o.astype(BF16)


def _position_dft_matrices(n):
    if n <= 512:
        c, s = _dft_real_imag(n, n ** -0.5)
        return jnp.asarray(c).astype(BF16), jnp.asarray(s).astype(BF16)
    n1 = GRID_W
    n2 = n // n1
    k = np.arange(n, dtype=np.int64)
    hi = np.arange(n2, dtype=np.int64)[:, None] * n1
    lo = np.arange(n1, dtype=np.int64)[:, None]
    ang_hi = 2.0 * np.pi * ((hi * k[None, :]) % n).astype(np.float64) / n
    ang_lo = 2.0 * np.pi * ((lo * k[None, :]) % n).astype(np.float64) / n
    ch, sh = jnp.asarray(np.cos(ang_hi), F32)[:, None, :], jnp.asarray(np.sin(ang_hi), F32)[:, None, :]
    cl, sl = jnp.asarray(np.cos(ang_lo), F32)[None, :, :], jnp.asarray(np.sin(ang_lo), F32)[None, :, :]
    scale = n ** -0.5
    c = ((ch * cl - sh * sl) * scale).reshape(n, n).astype(BF16)
    s = ((sh * cl + ch * sl) * scale).reshape(n, n).astype(BF16)
    return c, s


def _fnet_positions(uc, us, batch):
    t, w = uc.shape
    n = t // batch
    cn, sn = _position_dft_matrices(n)
    tn = min(512, n)
    nt = n // tn
    return pl.pallas_call(
        _fnet_kernel,
        grid=(nt, batch),
        in_specs=[
            pl.BlockSpec((tn, n), lambda i, b: (i, 0)),
            pl.BlockSpec((tn, n), lambda i, b: (i, 0)),
            pl.BlockSpec((n, w), lambda i, b: (b, 0)),
            pl.BlockSpec((n, w), lambda i, b: (b, 0)),
        ],
        out_specs=pl.BlockSpec((tn, w), lambda i, b: (b * nt + i, 0)),
        out_shape=jax.ShapeDtypeStruct((t, w), BF16),
        compiler_params=_params(2),
        name="fnet_positions",
    )(cn, sn, uc, us)


def _na_geometry(rows):
    n_blocks = rows // NA_ROWS_PER_BLOCK
    kh = min(NA_KH, rows)
    variants, vi, start_blk = [], [], []
    for blk in range(n_blocks):
        start = int(np.clip(NA_ROWS_PER_BLOCK * blk - kh // 2, 0, rows - NA_WIN_ROWS))
        table = []
        for a in range(NA_ROWS_PER_BLOCK):
            r = NA_ROWS_PER_BLOCK * blk + a
            r0 = int(np.clip(r - kh // 2, 0, rows - kh))
            table.append(tuple((start + i) - r + (NA_KH - 1) if 0 <= (start + i) - r0 < kh else -1
                               for i in range(NA_WIN_ROWS)))
            assert sum(e >= 0 for e in table[-1]) == kh
        table = tuple(table)
        if table not in variants:
            variants.append(table)
        vi.append(variants.index(table))
        start_blk.append(start * GRID_W // NA_KBLK)
    return vi, start_blk, variants


def _na_pair_windows(variants):
    lows = []
    for table in variants:
        for j in range(NA_ROWS_PER_BLOCK // 2):
            valid = [i for i in range(NA_WIN_ROWS) if table[2 * j][i] >= 0 or table[2 * j + 1][i] >= 0]
            lo = min(min(valid), NA_WIN_ROWS - NA_PAIR_ROWS)
            assert max(valid) < lo + NA_PAIR_ROWS
            lows.append(lo)
    return lows


def _na_bias_kernel(rpb_ref, o_ref, *, variants, pair_lows):
    h = pl.program_id(0)
    n_ro = 2 * NA_KH - 1
    n_co = 2 * NA_KW - 1
    kc = lax.broadcasted_iota(jnp.int32, (GRID_W, LANES), 0)
    lane = lax.broadcasted_iota(jnp.int32, (GRID_W, LANES), 1)
    c = lane & (GRID_W - 1)
    co = kc - c + (NA_KW - 1)
    c0 = jnp.clip(c - NA_KW // 2, 0, GRID_W - NA_KW)
    win = (kc >= c0) & (kc < c0 + NA_KW)
    neg = jnp.full((GRID_W, LANES), NEG_BIG, F32)
    tiles = []
    for ro in range(n_ro):
        t = neg
        for tt in range(n_co):
            t = jnp.where(win & (co == tt), rpb_ref[(h * n_ro + ro) * n_co + tt] * LOG2_E, t)
        tiles.append(t)
    left = lane < GRID_W
    for vv, table in enumerate(variants):
        for j in range(NA_ROWS_PER_BLOCK // 2):
            lo = pair_lows[vv * (NA_ROWS_PER_BLOCK // 2) + j]
            for ii in range(NA_PAIR_ROWS):
                r1, r2 = table[2 * j][lo + ii], table[2 * j + 1][lo + ii]
                t1 = tiles[r1] if r1 >= 0 else neg
                t2 = tiles[r2] if r2 >= 0 else neg
                o_ref[vv, j, ii * GRID_W:(ii + 1) * GRID_W, :] = jnp.where(left, t1, t2)


def _na_bias_tables(rpb, variants, pair_lows):
    heads = rpb.shape[0]
    n_pairs = NA_ROWS_PER_BLOCK // 2
    nk = NA_PAIR_ROWS * GRID_W
    return pl.pallas_call(
        functools.partial(_na_bias_kernel, variants=variants, pair_lows=pair_lows),
        grid=(heads,),
        in_specs=[pl.BlockSpec(memory_space=pltpu.SMEM)],
        out_specs=pl.BlockSpec((len(variants), None, n_pairs, nk, LANES), lambda h: (0, h, 0, 0, 0)),
        out_shape=jax.ShapeDtypeStruct((len(variants), heads, n_pairs, nk, LANES), F32),
        compiler_params=_params(1),
        name="na_bias_tables",
    )(rpb.reshape(-1))


def _pair_masks():
    lane = lax.broadcasted_iota(jnp.int32, (1, LANES), 1)
    return [(lane // NA_HEAD_DIM == hh).astype(BF16) for hh in range(LANES // NA_HEAD_DIM)]


def _na_body(q_ref, k_ref, vt_ref, kc_ref, vc_ref, bias_ref, o_ref, lows):
    span = NA_PAIR_ROWS * GRID_W
    nk = NA_WIN_ROWS * GRID_W
    q = q_ref[...]
    masks = _pair_masks()

    def scores(hh):
        qh = q * masks[hh]
        st = _dot_nt(k_ref[...], qh)
        return st, _dot_nt(kc_ref[...], qh)

    outs = []
    st, sc = scores(0)
    for hh in range(len(masks)):
        nxt = scores(hh + 1) if hh + 1 < len(masks) else None
        ps, pcs, ls = [], [], []
        for j, lo in enumerate(lows):
            lanes = slice(j * LANES, (j + 1) * LANES)
            r0 = lo * GRID_W
            sj = st[r0:r0 + span, lanes] + bias_ref[hh, j]
            scj = sc[:, lanes]
            m = jnp.maximum(jnp.max(sj, axis=0, keepdims=True), jnp.max(scj, axis=0, keepdims=True))
            pj = jnp.exp2(sj - m)
            pcj = jnp.exp2(scj - m)
            ls.append(jnp.sum(pj, axis=0, keepdims=True) + jnp.sum(pcj, axis=0, keepdims=True))
            pieces = [jnp.zeros((r0, LANES), BF16), pj.astype(BF16), jnp.zeros((nk - r0 - span, LANES), BF16)]
            ps.append(jnp.concatenate([x for x in pieces if x.shape[0]], axis=0))
            pcs.append(pcj.astype(BF16))
        p = jnp.concatenate(ps, axis=1)
        pc = jnp.concatenate(pcs, axis=1)
        rows = slice(hh * NA_HEAD_DIM, (hh + 1) * NA_HEAD_DIM)
        acc = _dot(jnp.concatenate([vt_ref[rows, :], vc_ref[rows, :]], axis=1),
                   jnp.concatenate([p, pc], axis=0))
        outs.append(acc / jnp.concatenate(ls, axis=1))
        if nxt is not None:
            st, sc = nxt
    o_ref[...] = jnp.transpose(jnp.concatenate(outs, axis=0)).astype(BF16)


def _na_kernel(vi_ref, sb_ref, q_ref, k_ref, vt_ref, kc_ref, vc_ref, bias_ref, o_ref, *, pair_lows):
    del sb_ref
    variant = vi_ref[pl.program_id(1)]
    n_lp = NA_ROWS_PER_BLOCK // 2
    for vv in range(len(pair_lows) // n_lp):
        pl.when(variant == vv)(functools.partial(
            _na_body, q_ref, k_ref, vt_ref, kc_ref, vc_ref, bias_ref, o_ref, pair_lows[vv * n_lp:(vv + 1) * n_lp]))


def _neighbourhood_attention(q, k, vt, kc, vct, bias, vi, start_blk, pair_lows, batch):
    n_pairs, t, _ = q.shape
    s = t // batch
    rows = s // GRID_W
    nq = NA_ROWS_PER_BLOCK * GRID_W
    nk = NA_WIN_ROWS * GRID_W
    n_blocks = rows // NA_ROWS_PER_BLOCK
    lc = kc.shape[1] // batch
    kblk_per_batch = s // NA_KBLK
    heads_per_pair = LANES // NA_HEAD_DIM
    bias6 = bias.reshape(bias.shape[0], n_pairs, heads_per_pair, *bias.shape[2:])

    def win_start(b, i, sb_r):
        return (b * kblk_per_batch + sb_r[i]) * NA_KBLK

    grid_spec = pltpu.PrefetchScalarGridSpec(
        num_scalar_prefetch=2,
        grid=(n_pairs, n_blocks, batch),
        in_specs=[pl.BlockSpec((None, nq, LANES), lambda hp, i, b, vi_r, sb_r: (hp, b * n_blocks + i, 0)),
                  pl.BlockSpec((None, pl.Element(nk), pl.Element(LANES)),
                               lambda hp, i, b, vi_r, sb_r: (hp, win_start(b, i, sb_r), 0)),
                  pl.BlockSpec((pl.Element(LANES), pl.Element(nk)),
                               lambda hp, i, b, vi_r, sb_r: (hp * LANES, win_start(b, i, sb_r)))]
        + [pl.BlockSpec((None, lc, LANES), lambda hp, i, b, vi_r, sb_r: (hp, b, 0)),
           pl.BlockSpec((LANES, lc), lambda hp, i, b, vi_r, sb_r: (hp, b)),
           pl.BlockSpec((None, None, heads_per_pair, *bias.shape[2:]),
                        lambda hp, i, b, vi_r, sb_r: (vi_r[i], hp, 0, 0, 0, 0))],
        out_specs=pl.BlockSpec((None, nq, LANES), lambda hp, i, b, vi_r, sb_r: (hp, b * n_blocks + i, 0)),
    )
    return pl.pallas_call(
        functools.partial(_na_kernel, pair_lows=tuple(pair_lows)),
        grid_spec=grid_spec,
        out_shape=jax.ShapeDtypeStruct((n_pairs, t, LANES), BF16),
        compiler_params=_params(3),
        name="neighbourhood_attention",
    )(jnp.asarray(vi, jnp.int32), jnp.asarray(start_blk, jnp.int32),
      q, k, vt, kc, vct, bias6)


def _ctx_attn_kernel(q_ref, k_ref, vt_ref, o_ref):
    q = q_ref[...]
    k = k_ref[...]
    outs = []
    for hh, hm in enumerate(_pair_masks()):
        st = _dot_nt(k, q * hm)
        m = jnp.max(st, axis=0, keepdims=True)
        p = jnp.exp2(st - m)
        l = jnp.sum(p, axis=0, keepdims=True)
        outs.append(_dot(vt_ref[hh * NA_HEAD_DIM:(hh + 1) * NA_HEAD_DIM, :], p.astype(BF16)) / l)
    o_ref[...] = jnp.transpose(jnp.concatenate(outs, axis=0)).astype(BF16)


def _ctx_dense_attention(q, k, vt, batch):
    n_pairs, t, _ = q.shape
    lc = t // batch
    spec = pl.BlockSpec((None, lc, LANES), lambda b, hp: (hp, b, 0))
    return pl.pallas_call(
        _ctx_attn_kernel,
        grid=(batch, n_pairs),
        in_specs=[spec, spec, pl.BlockSpec((LANES, lc), lambda b, hp: (hp, b))],
        out_specs=spec,
        out_shape=jax.ShapeDtypeStruct((n_pairs, t, LANES), BF16),
        compiler_params=_params(2),
        name="ctx_dense_attention",
    )(q, k, vt)


MLA_QK_PAD = 256
MLA_LOOKAHEAD = 1


def _rope(x, cos, sin_next, sin_prev):
    quarter = MLA_ROPE // 4
    return (x * cos + pltpu.roll(x, LANES - quarter, 1) * sin_next + pltpu.roll(x, quarter, 1) * sin_prev)


def _proj_mla_body(x, mod_ref, in_refs, out_refs, *, rope, need_q):
    g_ref, win_ref, gq_ref, gkv_ref, wqn_ref, wqr_ref, wuk_ref, wuv_ref, *rope_refs = in_refs
    if rope:
        cos_ref, sa_ref, sb_ref = rope_refs
    if need_q:
        q_ref, k_ref, v_ref = out_refs
    else:
        k_ref, v_ref = out_refs
    hb = _norm_mod(x, g_ref[...], mod_ref, 3).astype(BF16)
    z = _dot(hb, win_ref[...])
    ckv = (_rms(z[:, MLA_Q_RANK:MLA_Q_RANK + MLA_KV_RANK]) * gkv_ref[...]).astype(BF16)
    kr = z[:, MLA_Q_RANK + MLA_KV_RANK:]
    if rope:
        cos, sa, sb = cos_ref[...], sa_ref[...], sb_ref[...]
        kr = _rope(kr, cos, sa, sb)
    kr = kr.astype(BF16)
    kn = _dot(ckv, wuk_ref[...])
    v_ref[...] = _dot_nt(wuv_ref[...], ckv).astype(BF16)
    for h in range(MLA_HEADS):
        k_ref[h, :, :LANES] = kn[:, h * LANES:(h + 1) * LANES].astype(BF16)
        k_ref[h, :, LANES:] = kr
    if need_q:
        cq = (_rms(z[:, :MLA_Q_RANK]) * gq_ref[...]).astype(BF16)
        scale = (MLA_NOPE + MLA_ROPE) ** -0.5 * LOG2_E
        qn = _dot(cq, wqn_ref[...])
        qr = _dot(cq, wqr_ref[...])
        for h in range(MLA_HEADS):
            qrh = qr[:, h * LANES:(h + 1) * LANES]
            if rope:
                qrh = _rope(qrh, cos, sa, sb)
            q_ref[h, :, :LANES] = (qn[:, h * LANES:(h + 1) * LANES] * scale).astype(BF16)
            q_ref[h, :, LANES:] = (qrh * scale).astype(BF16)
    return x


def _rope_tables(n_tok):
    pos = jnp.arange(n_tok)
    row = (pos // GRID_W).astype(F32)
    col = (pos % GRID_W).astype(F32)
    half = MLA_ROPE // 2
    inv = ROPE_BASE ** (-jnp.arange(0, half, 2, dtype=F32) / half)
    ang_r = row[:, None] * inv[None]
    ang_c = col[:, None] * inv[None]
    ang = jnp.concatenate([ang_r, ang_r, ang_c, ang_c], axis=-1)
    cos, sin = jnp.cos(ang), jnp.sin(ang)
    quarter = MLA_ROPE // 4
    even_q = ((jnp.arange(MLA_ROPE) // quarter) % 2 == 0)[None, :]
    pad = LANES - MLA_ROPE
    cos_p = jnp.pad(cos, ((0, 0), (0, pad)), constant_values=1.0)
    sin_next = jnp.pad(jnp.where(even_q, -sin, 0.0), ((0, 0), (0, pad)))
    sin_prev = jnp.pad(jnp.where(even_q, 0.0, sin), ((0, 0), (0, pad)))
    return cos_p, sin_next, sin_prev


def _sec_proj_mla(g, w, rope_tabs, need_q):
    rope = rope_tabs is not None
    inputs = [(g.reshape(1, -1), "res")] + [(a, "res") for a in w]
    if rope:
        inputs += [(tab, "rope") for tab in rope_tabs]
    qk = ((MLA_HEADS, MLA_QK_PAD), "head")
    outputs = ([qk] if need_q else []) + [qk, (MLA_HEADS * MLA_V, "col")]
    return _Section(functools.partial(_proj_mla_body, rope=rope, need_q=need_q), inputs, outputs)


def _mla_attn_kernel(q_ref, *rest, n_src, chunk):
    k_refs, vt_refs, o_ref = rest[:n_src], rest[n_src:2 * n_src], rest[2 * n_src]
    q = q_ref[...]
    chunks = [(k_ref, vt_ref, c0, min(k_ref.shape[0], c0 + chunk))
              for k_ref, vt_ref in zip(k_refs, vt_refs) for c0 in range(0, k_ref.shape[0], chunk)]

    def scores(j):
        k_ref, _, c0, c1 = chunks[j]
        return _dot_nt(k_ref[c0:c1, :], q)

    m = l = acc = None
    ahead =[scores(j) for j in range(min(MLA_LOOKAHEAD, len(chunks)))]
    for j, (_, vt_ref, c0, c1) in enumerate(chunks):
        if j + MLA_LOOKAHEAD < len(chunks):
            ahead.append(scores(j + MLA_LOOKAHEAD))
        st = ahead.pop(0)
        mj = jnp.max(st, axis=0, keepdims=True)
        if m is None:
            m_new = mj
        else:
            m_new = jnp.maximum(m, mj)
            alpha = jnp.exp2(m - m_new)
        p = jnp.exp2(st - m_new)
        lj = jnp.sum(p, axis=0, keepdims=True)
        oj = _dot(vt_ref[:, c0:c1], p.astype(BF16))
        if m is None:
            l, acc = lj, oj
        else:
            l, acc = alpha * l + lj, alpha * acc + oj
        m = m_new
    o_ref[...] = jnp.transpose(acc / l).astype(BF16)


def _mla_attention(q, sources, batch, tq=2048, chunk=1024):
    t = q.shape[1]
    s = t // batch
    tq = min(tq, s)
    nq = s // tq
    k_specs, v_specs = [], []
    for k, _ in sources:
        n = k.shape[1] // batch
        k_specs.append(pl.BlockSpec((None, n, MLA_QK_PAD), lambda b, h, i: (h, b, 0)))
        v_specs.append(pl.BlockSpec((MLA_V, n), lambda b, h, i: (h, b)))
    return pl.pallas_call(
        functools.partial(_mla_attn_kernel, n_src=len(sources), chunk=chunk),
        grid=(batch, MLA_HEADS, nq),
        in_specs=[pl.BlockSpec((None, tq, MLA_QK_PAD), lambda b, h, i: (h, b * nq + i, 0))] + k_specs + v_specs,
        out_specs=pl.BlockSpec((None, tq, MLA_V), lambda b, h, i: (h, b * nq + i, 0)),
        out_shape=jax.ShapeDtypeStruct((MLA_HEADS, t, MLA_V), BF16),
        compiler_params=_params(3),
        name="mla_attention",
    )(q, *[k for k, _ in sources], *[v for _, v in sources])


def _mla_weights(w_in, g_q, g_kv, w_uq, w_uk, w_uv):
    w_in_p = jnp.pad(w_in, ((0, 0), (0, 5 * LANES - w_in.shape[1]))).astype(BF16)
    wq = w_uq.reshape(MLA_Q_RANK, MLA_HEADS, MLA_NOPE + MLA_ROPE)
    wqn = wq[:, :, :MLA_NOPE].reshape(MLA_Q_RANK, MLA_HEADS * MLA_NOPE).astype(BF16)
    wqr = jnp.pad(wq[:, :, MLA_NOPE:], ((0, 0), (0, 0), (0, LANES - MLA_ROPE)))
    wqr = wqr.reshape(MLA_Q_RANK, MLA_HEADS * LANES).astype(BF16)
    return (w_in_p, g_q.reshape(1, -1), g_kv.reshape(1, -1), wqn, wqr, w_uk.astype(BF16), w_uv.T.astype(BF16))


def kernel(x, c, ctx, c_ctx, ada_w, ada_b, norm_g, ffn_w_gate, ffn_w_up, ffn_w_down, ab_w_in, ab_rpb, ab_w_out,
           mla_w_in, mla_g_q, mla_g_kv, mla_w_uq, mla_w_uk, mla_w_uv, mla_w_o, final_g):
    batch, seq, d = x.shape
    lc = ctx.shape[1]
    depth = ada_w.shape[0]
    assert seq % (GRID_W * NA_WIN_ROWS) == 0 and lc % LANES == 0
    t_lat, t_ctx = batch * seq, batch * lc

    n_rows = -(-(batch + 1) // 8) * 8
    cvec = jnp.concatenate([c, c_ctx[None, :], jnp.zeros((n_rows - batch - 1, d), F32)], axis=0)
    mod = _modulation(cvec, ada_w, ada_b)

    x_l = x.reshape(t_lat, d)
    x_c = ctx.reshape(t_ctx, d)
    lat = dict(rows_per_mod=seq, mod_off=0)
    cx = dict(rows_per_mod=t_ctx, mod_off=batch)

    wg, wu, wd = ffn_w_gate.astype(BF16), ffn_w_up.astype(BF16), ffn_w_down.astype(BF16)
    rope_tabs = None

    for layer in range(depth):
        last = layer == depth - 1
        m = mod[layer]
        g = norm_g[layer]
        ffn1 = _sec_half_ffn(0, g[0], wg, wu, wd, (layer, 0))
        ffn2 = _sec_half_ffn(6, g[2], wg, wu, wd, (layer, 1))
        i = layer // 2
        if layer % 2 == 0:
            proj = _sec_proj_ab(g[1], ab_w_in[i])
            w_out = ab_w_out[i].astype(BF16)
            x_l, uc_l, us_l, q_l, k_l, vt_l = _token_stage(x_l, m, sections=[ffn1, proj], name="ffn_proj_ab", **lat)
            x_c, uc_c, us_c, q_c, k_c, vt_c = _token_stage(x_c, m, sections=[ffn1, proj], name="ffn_proj_ab", **cx)
            a_l = _fnet_positions(uc_l, us_l, batch)
            vi, start_blk, variants = _na_geometry(seq // GRID_W)
            pair_lows = _na_pair_windows(variants)
            bias = _na_bias_tables(ab_rpb[i], variants, pair_lows)
            b_l = _neighbourhood_attention(q_l, k_l, vt_l, k_c, vt_c, bias, vi, start_blk, pair_lows, batch)
            ys_l = [a_l, b_l]
            if not last:
                a_c = _fnet_positions(uc_c, us_c, batch)
                b_c = _ctx_dense_attention(q_c, k_c, vt_c, batch)
                ys_c = [a_c, b_c]
        else:
            if rope_tabs is None:
                rope_tabs = _rope_tables(seq)
            w = _mla_weights(mla_w_in[i], mla_g_q[i], mla_g_kv[i], mla_w_uq[i], mla_w_uk[i], mla_w_uv[i])
            w_out = mla_w_o[i].astype(BF16)
            x_l, q_l, k_l, vt_l = _token_stage(
                x_l, m, sections=[ffn1, _sec_proj_mla(g[1], w, rope_tabs, True)], name="ffn_proj_mla", seq=seq, **lat)
            x_c, *q_c, k_c, vt_c = _token_stage(
                x_c, m, sections=[ffn1, _sec_proj_mla(g[1], w, None, not last)], name="ffn_proj_mla", **cx)
            ys_l = [_mla_attention(q_l, [(k_l, vt_l), (k_c, vt_c)], batch)]
            if not last:
                ys_c = [_mla_attention(q_c[0], [(k_c, vt_c)], batch)]
        tail = [_sec_final_norm(final_g)] if last else []
        (x_l,) = _token_stage(x_l, m, sections=[_sec_outproj(w_out, ys_l), ffn2] + tail, name="outproj_ffn", **lat)
        if not last:
            (x_c,) = _token_stage(x_c, m, sections=[_sec_outproj(w_out, ys_c), ffn2], name="outproj_ffn", **cx)
    return x_l.reshape(batch, seq, d)
```

```python
import functools
import math

import numpy as np
import jax
import jax.numpy as jnp
from jax import lax
from jax.experimental import pallas as pl
from jax.experimental.pallas import tpu as pltpu

F32 = jnp.float32
BF16 = jnp.bfloat16

GRID_W = 64
EPS = 1e-6
N_MOD = 9
FNET_GROUPS = 4
FNET_GROUP_DIM = 128
FNET_DIM = FNET_GROUPS * FNET_GROUP_DIM
NA_HEADS = 8
NA_HEAD_DIM = 64
NA_DIM = NA_HEADS * NA_HEAD_DIM
NA_KH = 8
NA_KW = 16
MLA_HEADS = 8
MLA_NOPE = 128
MLA_ROPE = 64
MLA_V = 128
MLA_Q_RANK = 384
MLA_KV_RANK = 128
ROPE_BASE = 10000.0

LANES = 128
VMEM_LIMIT_BYTES = 56 * 1024 * 1024
NEG_BIG = -1e30
LOG2_E = math.log2(math.e)

NA_ROWS_PER_BLOCK = 8
NA_WIN_ROWS = 16
NA_KBLK = 256
NA_PAIR_ROWS = NA_KH + 1


def _params(n_axes):
    return pltpu.CompilerParams(
        dimension_semantics=("arbitrary",) * n_axes,
        vmem_limit_bytes=VMEM_LIMIT_BYTES,
    )


def _resident(shape, lead=()):
    nd = len(shape) - len(lead)
    return pl.BlockSpec((None,) * len(lead) + tuple(shape[len(lead):]), lambda *_: tuple(lead) + (0,) * nd,
                        pipeline_mode=pl.Buffered(1))


def _dot(a, b):
    return jnp.dot(a, b, preferred_element_type=F32)


def _dot_nt(a, b):
    return lax.dot_general(a, b, (((1,), (1,)), ((), ())), preferred_element_type=F32)


def _rms(x):
    return x * lax.rsqrt(jnp.mean(x * x, axis=-1, keepdims=True) + EPS)


def _norm_mod(x, g, mod_ref, k0):
    shift = mod_ref[k0:k0 + 1, :]
    scale = mod_ref[k0 + 1:k0 + 2, :]
    return (_rms(x) * g) * (1.0 + scale) + shift


def _mod_kernel(c_ref, w_ref, b_ref, o_ref):
    c = c_ref[...]
    sc = c * jax.nn.sigmoid(c)
    w = w_ref[...]
    sc_hi, w_hi = sc.astype(BF16), w.astype(BF16)
    sc_lo = (sc - sc_hi.astype(F32)).astype(BF16)
    w_lo = (w - w_hi.astype(F32)).astype(BF16)
    r = sc.shape[0]
    a = _dot(jnp.concatenate([sc_hi, sc_lo], axis=0), w_hi)
    o_ref[...] = a[:r] + a[r:] + _dot(sc_hi, w_lo) + b_ref[...]


def _modulation(cvec, ada_w, ada_b):
    depth, d, n = ada_w.shape
    r = cvec.shape[0]
    tn = 1536
    assert n % tn == 0
    out = pl.pallas_call(
        _mod_kernel,
        grid=(depth, n // tn),
        in_specs=[
            pl.BlockSpec((r, d), lambda l, j: (0, 0)),
            pl.BlockSpec((None, d, tn), lambda l, j: (l, 0, j)),
            pl.BlockSpec((None, 1, tn), lambda l, j: (l, 0, j)),
        ],
        out_specs=pl.BlockSpec((None, r, tn), lambda l, j: (l, 0, j)),
        out_shape=jax.ShapeDtypeStruct((depth, r, n), F32),
        compiler_params=_params(2),
        name="adaln_modulation",
    )(cvec, ada_w, ada_b.reshape(depth, 1, n))
    return out.reshape(depth, r, N_MOD, d)


def _mod_spec(d, tm, rows_per_mod, mod_off):
    return pl.BlockSpec((None, N_MOD, d), lambda i: (mod_off + (i * tm) // rows_per_mod, 0, 0))


class _Section:
    def __init__(self, body, inputs=(), outputs=(), writes_x=False):
        self.body, self.inputs, self.outputs, self.writes_x = body, list(inputs), list(outputs), writes_x


def _stage_kernel(*refs, sections, writes_x):
    x_ref, mod_ref = refs[0], refs[1]
    n_in = sum(len(s.inputs) for s in sections)
    in_refs = refs[2:2 + n_in]
    out_refs = refs[2 + n_in:]
    x = x_ref[...]
    xo_ref = None
    if writes_x:
        xo_ref, out_refs = out_refs[0], out_refs[1:]
    i0 = o0 = 0
    for s in sections:
        x = s.body(x, mod_ref, in_refs[i0:i0 + len(s.inputs)], out_refs[o0:o0 + len(s.outputs)])
        i0 += len(s.inputs)
        o0 += len(s.outputs)
    if writes_x:
        xo_ref[...] = x


def _token_stage(x, mod, rows_per_mod, mod_off, sections, name, seq=None, tm=512):
    t, d = x.shape
    tm = min(tm, t)
    assert t % tm == 0 and rows_per_mod % tm == 0
    writes_x = any(s.writes_x for s in sections)
    in_specs = [pl.BlockSpec((tm, d), lambda i: (i, 0)), _mod_spec(d, tm, rows_per_mod, mod_off)]
    args = [x, mod]
    for s in sections:
        for arr, kind in s.inputs:
            if kind == "res":
                in_specs.append(_resident(arr.shape))
            elif isinstance(kind, tuple):
                in_specs.append(_resident(arr.shape, kind[1]))
            elif kind == "row":
                in_specs.append(pl.BlockSpec((tm, arr.shape[1]), lambda i: (i, 0)))
            elif kind == "head":
                in_specs.append(pl.BlockSpec((arr.shape[0], tm, arr.shape[2]), lambda i: (0, i, 0)))
            else:
                assert kind == "rope" and seq % tm == 0
                n_rt = seq // tm
                in_specs.append(pl.BlockSpec((tm, arr.shape[1]), lambda i: (i % n_rt, 0)))
            args.append(arr)
    out_specs, out_shape = [], []
    if writes_x:
        out_specs.append(pl.BlockSpec((tm, d), lambda i: (i, 0)))
        out_shape.append(jax.ShapeDtypeStruct((t, d), F32))
    for s in sections:
        for w, kind in s.outputs:
            if kind == "row":
                out_specs.append(pl.BlockSpec((tm, w), lambda i: (i, 0)))
                out_shape.append(jax.ShapeDtypeStruct((t, w), BF16))
            elif kind == "head":
                heads, hw = w
                out_specs.append(pl.BlockSpec((heads, tm, hw), lambda i: (0, i, 0)))
                out_shape.append(jax.ShapeDtypeStruct((heads, t, hw), BF16))
            else:
                out_specs.append(pl.BlockSpec((w, tm), lambda i: (0, i)))
                out_shape.append(jax.ShapeDtypeStruct((w, t), BF16))
    return pl.pallas_call(
        functools.partial(_stage_kernel, sections=sections, writes_x=writes_x),
        grid=(t // tm,),
        in_specs=in_specs,
        out_specs=out_specs,
        out_shape=out_shape,
        compiler_params=_params(1),
        name=name,
    )(*args)


FFN_CHUNKS = 1


def _sec_half_ffn(k0, g, wg, wu, wd, lead):
    def body(x, mod_ref, in_refs, out_refs):
        g_ref, wg_ref, wu_ref, wd_ref = in_refs
        hb = _norm_mod(x, g_ref[...], mod_ref, k0).astype(BF16)
        tf = wg_ref.shape[1] // FFN_CHUNKS
        y = None
        for j in range(FFN_CHUNKS):
            a = _dot(hb, wg_ref[:, j * tf:(j + 1) * tf])
            u = _dot(hb, wu_ref[:, j * tf:(j + 1) * tf])
            act = ((a * jax.nn.sigmoid(a)) * u).astype(BF16)
            yj = _dot(act, wd_ref[j * tf:(j + 1) * tf, :])
            y = yj if y is None else y + yj
        return x + (0.5 * mod_ref[k0 + 2:k0 + 3, :]) * y

    assert wg.shape[-1] % (FFN_CHUNKS * LANES) == 0
    at = ("res_at", tuple(lead))
    return _Section(body, [(g.reshape(1, -1), "res"), (wg, at), (wu, at), (wd, at)], writes_x=True)


def _sec_outproj(w, ys):
    def body(x, mod_ref, in_refs, out_refs):
        w_ref, *y_refs = in_refs
        pieces = []
        for y_ref in y_refs:
            if len(y_ref.shape) == 3:
                pieces += [y_ref[h] for h in range(y_ref.shape[0])]
            else:
                pieces.append(y_ref[...])
        return x + mod_ref[5:6, :] * _dot(jnp.concatenate(pieces, axis=1), w_ref[...])

    return _Section(body, [(w, "res")] + [(y, "head" if y.ndim == 3 else "row") for y in ys], writes_x=True)


def _sec_final_norm(g):
    def body(x, mod_ref, in_refs, out_refs):
        return _rms(x) * in_refs[0][...]

    return _Section(body, [(g.reshape(1, -1), "res")], writes_x=True)


def _proj_ab_body(x, mod_ref, in_refs, out_refs):
    g_ref, w_ref, wvt_ref, cc_ref, cs_ref = in_refs
    uc_ref, us_ref, q_ref, k_ref, vt_ref = out_refs
    hb = _norm_mod(x, g_ref[...], mod_ref, 3).astype(BF16)
    z = _dot(hb, w_ref[...])
    u = z[:, :FNET_DIM].astype(BF16)
    for g in range(FNET_GROUPS):
        sl = slice(g * FNET_GROUP_DIM, (g + 1) * FNET_GROUP_DIM)
        uc_ref[:, sl] = _dot(u[:, sl], cc_ref[...]).astype(BF16)
        us_ref[:, sl] = _dot(u[:, sl], cs_ref[...]).astype(BF16)
    for hp in range(NA_DIM // LANES):
        q0, k0 = FNET_DIM + hp * LANES, FNET_DIM + NA_DIM + hp * LANES
        q_ref[hp] = (z[:, q0:q0 + LANES] * (NA_HEAD_DIM ** -0.5 * LOG2_E)).astype(BF16)
        k_ref[hp] = z[:, k0:k0 + LANES].astype(BF16)
    vt_ref[...] = _dot_nt(wvt_ref[...], hb).astype(BF16)
    return x


def _dft_real_imag(n, scale):
    j = np.arange(n, dtype=np.int64)
    ang = 2.0 * np.pi * ((j[:, None] * j[None, :]) % n).astype(np.float64) / n
    return (np.cos(ang) * scale).astype(np.float32), (np.sin(ang) * scale).astype(np.float32)


def _sec_proj_ab(g, w_in):
    cc, cs = _dft_real_imag(FNET_GROUP_DIM, FNET_GROUP_DIM ** -0.5)
    n_row = FNET_DIM + 2 * NA_DIM
    w_row = w_in[:, :n_row].astype(BF16)
    w_vt = w_in[:, n_row:].T.astype(BF16)
    inputs = [(g.reshape(1, -1), "res"), (w_row, "res"), (w_vt, "res"),
              (jnp.asarray(cc).astype(BF16), "res"), (jnp.asarray(cs).astype(BF16), "res")]
    pairs = ((NA_DIM // LANES, LANES), "head")
    outputs = [(FNET_DIM, "row"), (FNET_DIM, "row"), pairs, pairs, (NA_DIM, "col")]
    return _Section(_proj_ab_body, inputs, outputs)


def _fnet_kernel(cn_ref, sn_ref, uc_ref, us_ref, o_ref):
    o = _dot(cn_ref[...], uc_ref[...]) - _dot(sn_ref[...], us_ref[...])
    o_ref[...] = o.astype(BF16)


def _position_dft_matrices(n):
    if n <= 512:
        c, s = _dft_real_imag(n, n ** -0.5)
        return jnp.asarray(c).astype(BF16), jnp.asarray(s).astype(BF16)
    n1 = GRID_W
    n2 = n // n1
    k = np.arange(n, dtype=np.int64)
    hi = np.arange(n2, dtype=np.int64)[:, None] * n1
    lo = np.arange(n1, dtype=np.int64)[:, None]
    ang_hi = 2.0 * np.pi * ((hi * k[None, :]) % n).astype(np.float64) / n
    ang_lo = 2.0 * np.pi * ((lo * k[None, :]) % n).astype(np.float64) / n
    ch, sh = jnp.asarray(np.cos(ang_hi), F32)[:, None, :], jnp.asarray(np.sin(ang_hi), F32)[:, None, :]
    cl, sl = jnp.asarray(np.cos(ang_lo), F32)[None, :, :], jnp.asarray(np.sin(ang_lo), F32)[None, :, :]
    scale = n ** -0.5
    c = ((ch * cl - sh * sl) * scale).reshape(n, n).astype(BF16)
    s = ((sh * cl + ch * sl) * scale).reshape(n, n).astype(BF16)
    return c, s


def _fnet_positions(uc, us, batch):
    t, w = uc.shape
    n = t // batch
    cn, sn = _position_dft_matrices(n)
    tn = min(512, n)
    nt = n // tn
    return pl.pallas_call(
        _fnet_kernel,
        grid=(nt, batch),
        in_specs=[
            pl.BlockSpec((tn, n), lambda i, b: (i, 0)),
            pl.BlockSpec((tn, n), lambda i, b: (i, 0)),
            pl.BlockSpec((n, w), lambda i, b: (b, 0)),
            pl.BlockSpec((n, w), lambda i, b: (b, 0)),
        ],
        out_specs=pl.BlockSpec((tn, w), lambda i, b: (b * nt + i, 0)),
        out_shape=jax.ShapeDtypeStruct((t, w), BF16),
        compiler_params=_params(2),
        name="fnet_positions",
    )(cn, sn, uc, us)


def _na_geometry(rows):
    n_blocks = rows // NA_ROWS_PER_BLOCK
    kh = min(NA_KH, rows)
    variants, vi, start_blk = [], [], []
    for blk in range(n_blocks):
        start = int(np.clip(NA_ROWS_PER_BLOCK * blk - kh // 2, 0, rows - NA_WIN_ROWS))
        table = []
        for a in range(NA_ROWS_PER_BLOCK):
            r = NA_ROWS_PER_BLOCK * blk + a
            r0 = int(np.clip(r - kh // 2, 0, rows - kh))
            table.append(tuple((start + i) - r + (NA_KH - 1) if 0 <= (start + i) - r0 < kh else -1
                               for i in range(NA_WIN_ROWS)))
            assert sum(e >= 0 for e in table[-1]) == kh
        table = tuple(table)
        if table not in variants:
            variants.append(table)
        vi.append(variants.index(table))
        start_blk.append(start * GRID_W // NA_KBLK)
    return vi, start_blk, variants


def _na_pair_windows(variants):
    lows = []
    for table in variants:
        for j in range(NA_ROWS_PER_BLOCK // 2):
            valid = [i for i in range(NA_WIN_ROWS) if table[2 * j][i] >= 0 or table[2 * j + 1][i] >= 0]
            lo = min(min(valid), NA_WIN_ROWS - NA_PAIR_ROWS)
            assert max(valid) < lo + NA_PAIR_ROWS
            lows.append(lo)
    return lows


def _na_bias_kernel(rpb_ref, o_ref, *, variants, pair_lows):
    h = pl.program_id(0)
    n_ro = 2 * NA_KH - 1
    n_co = 2 * NA_KW - 1
    kc = lax.broadcasted_iota(jnp.int32, (GRID_W, LANES), 0)
    lane = lax.broadcasted_iota(jnp.int32, (GRID_W, LANES), 1)
    c = lane & (GRID_W - 1)
    co = kc - c + (NA_KW - 1)
    c0 = jnp.clip(c - NA_KW // 2, 0, GRID_W - NA_KW)
    win = (kc >= c0) & (kc < c0 + NA_KW)
    neg = jnp.full((GRID_W, LANES), NEG_BIG, F32)
    tiles = []
    for ro in range(n_ro):
        t = neg
        for tt in range(n_co):
            t = jnp.where(win & (co == tt), rpb_ref[(h * n_ro + ro) * n_co + tt] * LOG2_E, t)
        tiles.append(t)
    left = lane < GRID_W
    for vv, table in enumerate(variants):
        for j in range(NA_ROWS_PER_BLOCK // 2):
            lo = pair_lows[vv * (NA_ROWS_PER_BLOCK // 2) + j]
            for ii in range(NA_PAIR_ROWS):
                r1, r2 = table[2 * j][lo + ii], table[2 * j + 1][lo + ii]
                t1 = tiles[r1] if r1 >= 0 else neg
                t2 = tiles[r2] if r2 >= 0 else neg
                o_ref[vv, j, ii * GRID_W:(ii + 1) * GRID_W, :] = jnp.where(left, t1, t2)


def _na_bias_tables(rpb, variants, pair_lows):
    heads = rpb.shape[0]
    n_pairs = NA_ROWS_PER_BLOCK // 2
    nk = NA_PAIR_ROWS * GRID_W
    return pl.pallas_call(
        functools.partial(_na_bias_kernel, variants=variants, pair_lows=pair_lows),
        grid=(heads,),
        in_specs=[pl.BlockSpec(memory_space=pltpu.SMEM)],
        out_specs=pl.BlockSpec((len(variants), None, n_pairs, nk, LANES), lambda h: (0, h, 0, 0, 0)),
        out_shape=jax.ShapeDtypeStruct((len(variants), heads, n_pairs, nk, LANES), F32),
        compiler_params=_params(1),
        name="na_bias_tables",
    )(rpb.reshape(-1))


def _pair_masks():
    lane = lax.broadcasted_iota(jnp.int32, (1, LANES), 1)
    return [(lane // NA_HEAD_DIM == hh).astype(BF16) for hh in range(LANES // NA_HEAD_DIM)]


def _na_body(q_ref, k_ref, vt_ref, kc_ref, vc_ref, bias_ref, o_ref, lows):
    span = NA_PAIR_ROWS * GRID_W
    nk = NA_WIN_ROWS * GRID_W
    q = q_ref[...]
    masks = _pair_masks()

    def scores(hh):
        qh = q * masks[hh]
        st = _dot_nt(k_ref[...], qh)
        return st, _dot_nt(kc_ref[...], qh)

    outs = []
    st, sc = scores(0)
    for hh in range(len(masks)):
        nxt = scores(hh + 1) if hh + 1 < len(masks) else None
        ps, pcs, ls = [], [], []
        for j, lo in enumerate(lows):
            lanes = slice(j * LANES, (j + 1) * LANES)
            r0 = lo * GRID_W
            sj = st[r0:r0 + span, lanes] + bias_ref[hh, j]
            scj = sc[:, lanes]
            m = jnp.maximum(jnp.max(sj, axis=0, keepdims=True), jnp.max(scj, axis=0, keepdims=True))
            pj = jnp.exp2(sj - m)
            pcj = jnp.exp2(scj - m)
            ls.append(jnp.sum(pj, axis=0, keepdims=True) + jnp.sum(pcj, axis=0, keepdims=True))
            pieces = [jnp.zeros((r0, LANES), BF16), pj.astype(BF16), jnp.zeros((nk - r0 - span, LANES), BF16)]
            ps.append(jnp.concatenate([x for x in pieces if x.shape[0]], axis=0))
            pcs.append(pcj.astype(BF16))
        p = jnp.concatenate(ps, axis=1)
        pc = jnp.concatenate(pcs, axis=1)
        rows = slice(hh * NA_HEAD_DIM, (hh + 1) * NA_HEAD_DIM)
        acc = _dot(jnp.concatenate([vt_ref[rows, :], vc_ref[rows, :]], axis=1),
                   jnp.concatenate([p, pc], axis=0))
        outs.append(acc / jnp.concatenate(ls, axis=1))
        if nxt is not None:
            st, sc = nxt
    o_ref[...] = jnp.transpose(jnp.concatenate(outs, axis=0)).astype(BF16)


def _na_kernel(vi_ref, sb_ref, q_ref, k_ref, vt_ref, kc_ref, vc_ref, bias_ref, o_ref, *, pair_lows):
    del sb_ref
    variant = vi_ref[pl.program_id(1)]
    n_lp = NA_ROWS_PER_BLOCK // 2
    for vv in range(len(pair_lows) // n_lp):
        pl.when(variant == vv)(functools.partial(
            _na_body, q_ref, k_ref, vt_ref, kc_ref, vc_ref, bias_ref, o_ref, pair_lows[vv * n_lp:(vv + 1) * n_lp]))


def _neighbourhood_attention(q, k, vt, kc, vct, bias, vi, start_blk, pair_lows, batch):
    n_pairs, t, _ = q.shape
    s = t // batch
    rows = s // GRID_W
    nq = NA_ROWS_PER_BLOCK * GRID_W
    nk = NA_WIN_ROWS * GRID_W
    n_blocks = rows // NA_ROWS_PER_BLOCK
    lc = kc.shape[1] // batch
    kblk_per_batch = s // NA_KBLK
    heads_per_pair = LANES // NA_HEAD_DIM
    bias6 = bias.reshape(bias.shape[0], n_pairs, heads_per_pair, *bias.shape[2:])

    def win_start(b, i, sb_r):
        return (b * kblk_per_batch + sb_r[i]) * NA_KBLK

    grid_spec = pltpu.PrefetchScalarGridSpec(
        num_scalar_prefetch=2,
        grid=(n_pairs, n_blocks, batch),
        in_specs=[pl.BlockSpec((None, nq, LANES), lambda hp, i, b, vi_r, sb_r: (hp, b * n_blocks + i, 0)),
                  pl.BlockSpec((None, pl.Element(nk), pl.Element(LANES)),
                               lambda hp, i, b, vi_r, sb_r: (hp, win_start(b, i, sb_r), 0)),
                  pl.BlockSpec((pl.Element(LANES), pl.Element(nk)),
                               lambda hp, i, b, vi_r, sb_r: (hp * LANES, win_start(b, i, sb_r)))]
        + [pl.BlockSpec((None, lc, LANES), lambda hp, i, b, vi_r, sb_r: (hp, b, 0)),
           pl.BlockSpec((LANES, lc), lambda hp, i, b, vi_r, sb_r: (hp, b)),
           pl.BlockSpec((None, None, heads_per_pair, *bias.shape[2:]),
                        lambda hp, i, b, vi_r, sb_r: (vi_r[i], hp, 0, 0, 0, 0))],
        out_specs=pl.BlockSpec((None, nq, LANES), lambda hp, i, b, vi_r, sb_r: (hp, b * n_blocks + i, 0)),
    )
    return pl.pallas_call(
        functools.partial(_na_kernel, pair_lows=tuple(pair_lows)),
        grid_spec=grid_spec,
        out_shape=jax.ShapeDtypeStruct((n_pairs, t, LANES), BF16),
        compiler_params=_params(3),
        name="neighbourhood_attention",
    )(jnp.asarray(vi, jnp.int32), jnp.asarray(start_blk, jnp.int32),
      q, k, vt, kc, vct, bias6)


def _ctx_attn_kernel(q_ref, k_ref, vt_ref, o_ref):
    q = q_ref[...]
    k = k_ref[...]
    outs = []
    for hh, hm in enumerate(_pair_masks()):
        st = _dot_nt(k, q * hm)
        m = jnp.max(st, axis=0, keepdims=True)
        p = jnp.exp2(st - m)
        l = jnp.sum(p, axis=0, keepdims=True)
        outs.append(_dot(vt_ref[hh * NA_HEAD_DIM:(hh + 1) * NA_HEAD_DIM, :], p.astype(BF16)) / l)
    o_ref[...] = jnp.transpose(jnp.concatenate(outs, axis=0)).astype(BF16)


def _ctx_dense_attention(q, k, vt, batch):
    n_pairs, t, _ = q.shape
    lc = t // batch
    spec = pl.BlockSpec((None, lc, LANES), lambda b, hp: (hp, b, 0))
    return pl.pallas_call(
        _ctx_attn_kernel,
        grid=(batch, n_pairs),
        in_specs=[spec, spec, pl.BlockSpec((LANES, lc), lambda b, hp: (hp, b))],
        out_specs=spec,
        out_shape=jax.ShapeDtypeStruct((n_pairs, t, LANES), BF16),
        compiler_params=_params(2),
        name="ctx_dense_attention",
    )(q, k, vt)


MLA_QK_PAD = 256
MLA_LOOKAHEAD = 1


def _rope(x, cos, sin_next, sin_prev):
    quarter = MLA_ROPE // 4
    return (x * cos + pltpu.roll(x, LANES - quarter, 1) * sin_next + pltpu.roll(x, quarter, 1) * sin_prev)


def _proj_mla_body(x, mod_ref, in_refs, out_refs, *, rope, need_q):
    g_ref, win_ref, gq_ref, gkv_ref, wqn_ref, wqr_ref, wuk_ref, wuv_ref, *rope_refs = in_refs
    if rope:
        cos_ref, sa_ref, sb_ref = rope_refs
    if need_q:
        q_ref, k_ref, v_ref = out_refs
    else:
        k_ref, v_ref = out_refs
    hb = _norm_mod(x, g_ref[...], mod_ref, 3).astype(BF16)
    z = _dot(hb, win_ref[...])
    ckv = (_rms(z[:, MLA_Q_RANK:MLA_Q_RANK + MLA_KV_RANK]) * gkv_ref[...]).astype(BF16)
    kr = z[:, MLA_Q_RANK + MLA_KV_RANK:]
    if rope:
        cos, sa, sb = cos_ref[...], sa_ref[...], sb_ref[...]
        kr = _rope(kr, cos, sa, sb)
    kr = kr.astype(BF16)
    kn = _dot(ckv, wuk_ref[...])
    v_ref[...] = _dot_nt(wuv_ref[...], ckv).astype(BF16)
    for h in range(MLA_HEADS):
        k_ref[h, :, :LANES] = kn[:, h * LANES:(h + 1) * LANES].astype(BF16)
        k_ref[h, :, LANES:] = kr
    if need_q:
        cq = (_rms(z[:, :MLA_Q_RANK]) * gq_ref[...]).astype(BF16)
        scale = (MLA_NOPE + MLA_ROPE) ** -0.5 * LOG2_E
        qn = _dot(cq, wqn_ref[...])
        qr = _dot(cq, wqr_ref[...])
        for h in range(MLA_HEADS):
            qrh = qr[:, h * LANES:(h + 1) * LANES]
            if rope:
                qrh = _rope(qrh, cos, sa, sb)
            q_ref[h, :, :LANES] = (qn[:, h * LANES:(h + 1) * LANES] * scale).astype(BF16)
            q_ref[h, :, LANES:] = (qrh * scale).astype(BF16)
    return x


def _rope_tables(n_tok):
    pos = jnp.arange(n_tok)
    row = (pos // GRID_W).astype(F32)
    col = (pos % GRID_W).astype(F32)
    half = MLA_ROPE // 2
    inv = ROPE_BASE ** (-jnp.arange(0, half, 2, dtype=F32) / half)
    ang_r = row[:, None] * inv[None]
    ang_c = col[:, None] * inv[None]
    ang = jnp.concatenate([ang_r, ang_r, ang_c, ang_c], axis=-1)
    cos, sin = jnp.cos(ang), jnp.sin(ang)
    quarter = MLA_ROPE // 4
    even_q = ((jnp.arange(MLA_ROPE) // quarter) % 2 == 0)[None, :]
    pad = LANES - MLA_ROPE
    cos_p = jnp.pad(cos, ((0, 0), (0, pad)), constant_values=1.0)
    sin_next = jnp.pad(jnp.where(even_q, -sin, 0.0), ((0, 0), (0, pad)))
    sin_prev = jnp.pad(jnp.where(even_q, 0.0, sin), ((0, 0), (0, pad)))
    return cos_p, sin_next, sin_prev


def _sec_proj_mla(g, w, rope_tabs, need_q):
    rope = rope_tabs is not None
    inputs = [(g.reshape(1, -1), "res")] + [(a, "res") for a in w]
    if rope:
        inputs += [(tab, "rope") for tab in rope_tabs]
    qk = ((MLA_HEADS, MLA_QK_PAD), "head")
    outputs = ([qk] if need_q else []) + [qk, (MLA_HEADS * MLA_V, "col")]
    return _Section(functools.partial(_proj_mla_body, rope=rope, need_q=need_q), inputs, outputs)


def _mla_attn_kernel(q_ref, *rest, n_src, chunk):
    k_refs, vt_refs, o_ref = rest[:n_src], rest[n_src:2 * n_src], rest[2 * n_src]
    n_heads = q_ref.shape[0]
    dv = o_ref.shape[-1]
    qs = [q_ref[h] for h in range(n_heads)]
    chunks = [(k_ref, vt_ref, c0, min(k_ref.shape[1], c0 + chunk))
              for k_ref, vt_ref in zip(k_refs, vt_refs) for c0 in range(0, k_ref.shape[1], chunk)]

    def scores(h, j):
        k_ref, _, c0, c1 = chunks[j]
        return _dot_nt(k_ref[h, c0:c1, :], qs[h])

    m, l, acc = [None] * n_heads, [None] * n_heads, [None] * n_heads
    st = [scores(h, 0) for h in range(n_heads)]
    for j, (_, vt_ref, c0, c1) in enumerate(chunks):
        st_next = [scores(h, j + 1) for h in range(n_heads)] if j + 1 < len(chunks) else None
        for h in range(n_heads):
            mj = jnp.max(st[h], axis=0, keepdims=True)
            if m[h] is None:
                m_new = mj
            else:
                m_new = jnp.maximum(m[h], mj)
                alpha = jnp.exp2(m[h] - m_new)
            p = jnp.exp2(st[h] - m_new)
            lj = jnp.sum(p, axis=0, keepdims=True)
            oj = _dot(vt_ref[h * dv:(h + 1) * dv, c0:c1], p.astype(BF16))
            if m[h] is None:
                l[h], acc[h] = lj, oj
            else:
                l[h], acc[h] = alpha * l[h] + lj, alpha * acc[h] + oj
            m[h] = m_new
        st = st_next
    for h in range(n_heads):
        o_ref[h] = jnp.transpose(acc[h] / l[h]).astype(BF16)


MLA_HEADS_PER_STEP = 2


def _mla_attention(q, sources, batch, tq=1024, chunk=1024):
    t = q.shape[1]
    s = t // batch
    tq = min(tq, s)
    nq = s // tq
    hs = MLA_HEADS_PER_STEP
    k_specs, v_specs = [], []
    for k, _ in sources:
        n = k.shape[1] // batch
        k_specs.append(pl.BlockSpec((hs, n, MLA_QK_PAD), lambda b, h, i: (h, b, 0)))
        v_specs.append(pl.BlockSpec((hs * MLA_V, n), lambda b, h, i: (h, b)))
    return pl.pallas_call(
        functools.partial(_mla_attn_kernel, n_src=len(sources), chunk=chunk),
        grid=(batch, MLA_HEADS // hs, nq),
        in_specs=[pl.BlockSpec((hs, tq, MLA_QK_PAD), lambda b, h, i: (h, b * nq + i, 0))] + k_specs + v_specs,
        out_specs=pl.BlockSpec((hs, tq, MLA_V), lambda b, h, i: (h, b * nq + i, 0)),
        out_shape=jax.ShapeDtypeStruct((MLA_HEADS, t, MLA_V), BF16),
        compiler_params=_params(3),
        name="mla_attention",
    )(q, *[k for k, _ in sources], *[v for _, v in sources])


def _mla_weights(w_in, g_q, g_kv, w_uq, w_uk, w_uv):
    w_in_p = jnp.pad(w_in, ((0, 0), (0, 5 * LANES - w_in.shape[1]))).astype(BF16)
    wq = w_uq.reshape(MLA_Q_RANK, MLA_HEADS, MLA_NOPE + MLA_ROPE)
    wqn = wq[:, :, :MLA_NOPE].reshape(MLA_Q_RANK, MLA_HEADS * MLA_NOPE).astype(BF16)
    wqr = jnp.pad(wq[:, :, MLA_NOPE:], ((0, 0), (0, 0), (0, LANES - MLA_ROPE)))
    wqr = wqr.reshape(MLA_Q_RANK, MLA_HEADS * LANES).astype(BF16)
    return (w_in_p, g_q.reshape(1, -1), g_kv.reshape(1, -1), wqn, wqr, w_uk.astype(BF16), w_uv.T.astype(BF16))


def kernel(x, c, ctx, c_ctx, ada_w, ada_b, norm_g, ffn_w_gate, ffn_w_up, ffn_w_down, ab_w_in, ab_rpb, ab_w_out,
           mla_w_in, mla_g_q, mla_g_kv, mla_w_uq, mla_w_uk, mla_w_uv, mla_w_o, final_g):
    batch, seq, d = x.shape
    lc = ctx.shape[1]
    depth = ada_w.shape[0]
    assert seq % (GRID_W * NA_WIN_ROWS) == 0 and lc % LANES == 0
    t_lat, t_ctx = batch * seq, batch * lc

    n_rows = -(-(batch + 1) // 8) * 8
    cvec = jnp.concatenate([c, c_ctx[None, :], jnp.zeros((n_rows - batch - 1, d), F32)], axis=0)
    mod = _modulation(cvec, ada_w, ada_b)

    x_l = x.reshape(t_lat, d)
    x_c = ctx.reshape(t_ctx, d)
    lat = dict(rows_per_mod=seq, mod_off=0)
    cx = dict(rows_per_mod=t_ctx, mod_off=batch)

    wg, wu, wd = ffn_w_gate.astype(BF16), ffn_w_up.astype(BF16), ffn_w_down.astype(BF16)
    rope_tabs = None

    for layer in range(depth):
        last = layer == depth - 1
        m = mod[layer]
        g = norm_g[layer]
        ffn1 = _sec_half_ffn(0, g[0], wg, wu, wd, (layer, 0))
        ffn2 = _sec_half_ffn(6, g[2], wg, wu, wd, (layer, 1))
        i = layer // 2
        if layer % 2 == 0:
            proj = _sec_proj_ab(g[1], ab_w_in[i])
            w_out = ab_w_out[i].astype(BF16)
            x_l, uc_l, us_l, q_l, k_l, vt_l = _token_stage(x_l, m, sections=[ffn1, proj], name="ffn_proj_ab", **lat)
            x_c, uc_c, us_c, q_c, k_c, vt_c = _token_stage(x_c, m, sections=[ffn1, proj], name="ffn_proj_ab", **cx)
            a_l = _fnet_positions(uc_l, us_l, batch)
            vi, start_blk, variants = _na_geometry(seq // GRID_W)
            pair_lows = _na_pair_windows(variants)
            bias = _na_bias_tables(ab_rpb[i], variants, pair_lows)
            b_l = _neighbourhood_attention(q_l, k_l, vt_l, k_c, vt_c, bias, vi, start_blk, pair_lows, batch)
            ys_l = [a_l, b_l]
            if not last:
                a_c = _fnet_positions(uc_c, us_c, batch)
                b_c = _ctx_dense_attention(q_c, k_c, vt_c, batch)
                ys_c = [a_c, b_c]
        else:
            if rope_tabs is None:
                rope_tabs = _rope_tables(seq)
            w = _mla_weights(mla_w_in[i], mla_g_q[i], mla_g_kv[i], mla_w_uq[i], mla_w_uk[i], mla_w_uv[i])
            w_out = mla_w_o[i].astype(BF16)
            x_l, q_l, k_l, vt_l = _token_stage(
                x_l, m, sections=[ffn1, _sec_proj_mla(g[1], w, rope_tabs, True)], name="ffn_proj_mla", seq=seq, **lat)
            x_c, *q_c, k_c, vt_c = _token_stage(
                x_c, m, sections=[ffn1, _sec_proj_mla(g[1], w, None, not last)], name="ffn_proj_mla", **cx)
            ys_l = [_mla_attention(q_l, [(k_l, vt_l), (k_c, vt_c)], batch)]
            if not last:
                ys_c = [_mla_attention(q_c[0], [(k_c, vt_c)], batch)]
        tail = [_sec_final_norm(final_g)] if last else []
        (x_l,) = _token_stage(x_l, m, sections=[_sec_outproj(w_out, ys_l), ffn2] + tail, name="outproj_ffn", **lat)
        if not last:
            (x_c,) = _token_stage(x_c, m, sections=[_sec_outproj(w_out, ys_c), ffn2], name="outproj_ffn", **cx)
    return x_l.reshape(batch, seq, d)
```
